```python
import math
import jax
import jax.numpy as jnp
from jax import lax
import numpy as np

D_MODEL = 1024
BATCH = 4
SEQ = 8192
DEPTH = 2

GRID_W = 64
CTX_LEN = 256
N_MOD = 6
BRANCH_WIDTH = 512
N_BRANCH = 3
DA_HEADS = 4
DA_QK_DIM = 64
DA_V_DIM = 2 * DA_QK_DIM
DA_QK_WIDTH = DA_HEADS * 2 * DA_QK_DIM
ROPE_BASE = 10000.0
Q_BLOCK = 128
LRU_BLOCKS = 8
LRU_BLOCK_DIM = BRANCH_WIDTH // LRU_BLOCKS
LRU_C = 8.0
CONV_K = 4
GDN_HEADS = 4
GDN_DK = 128
GDN_DV = 128
GDN_QKV_WIDTH = GDN_HEADS * (2 * GDN_DK + GDN_DV)
GDN_CHUNK = 64
N_DIR = 2
D_FF = 4 * D_MODEL
EPS = 1e-6
IN_SPLITS = (DA_QK_WIDTH, DA_QK_WIDTH, BRANCH_WIDTH,
             BRANCH_WIDTH, BRANCH_WIDTH,
             GDN_QKV_WIDTH, BRANCH_WIDTH,
             N_DIR * GDN_HEADS, N_DIR * GDN_HEADS,
             N_BRANCH * D_MODEL)
N_IN = sum(IN_SPLITS)

kernel_name = 'hybrid_prefix_dit_block'


def rms_norm(x, g):
    xf = x.astype(jnp.float32)
    y = xf * lax.rsqrt(jnp.mean(xf * xf, axis=-1, keepdims=True) + EPS)
    return (y * g.astype(jnp.float32)).astype(x.dtype)


def l2_norm(x):
    xf = x.astype(jnp.float32)
    return (xf * lax.rsqrt(jnp.sum(xf * xf, axis=-1, keepdims=True) + EPS)).astype(x.dtype)


def modulate(h, shift, scale):
    return h * (1 + scale) + shift


def flip_t(t):
    return t[:, ::-1]


def dwconv_centred(x, w, b=None):
    K = w.shape[0]
    T = x.shape[1]
    lo = (K - 1) // 2
    xp = jnp.pad(x, ((0, 0), (lo, K - 1 - lo), (0, 0)))
    y = sum(xp[:, k:k + T] * w[k] for k in range(K))
    return y if b is None else y + b


def split_proj(p):
    offs = np.cumsum(IN_SPLITS)[:-1].tolist()
    return jnp.split(p, offs, axis=-1)


def axial_rope_tables(row, col):
    n_freq = DA_QK_DIM // 4
    inv = ROPE_BASE ** (-jnp.arange(n_freq, dtype=jnp.float32) / n_freq)
    ang_r = row.astype(jnp.float32)[:, None] * inv
    ang_c = col.astype(jnp.float32)[:, None] * inv
    ang = jnp.concatenate([ang_r, ang_r, ang_c, ang_c], axis=-1)
    return jnp.cos(ang), jnp.sin(ang)


def apply_axial_rope(x, cos, sin):
    xf = x.astype(jnp.float32)
    xs = xf.reshape(x.shape[:-1] + (2, 2, DA_QK_DIM // 4))
    rot = jnp.stack([-xs[..., 1, :], xs[..., 0, :]], axis=-2).reshape(x.shape)
    c = cos[None, :, None, None, :]
    s = sin[None, :, None, None, :]
    return (xf * c + rot * s).astype(x.dtype)


def diff_attend(q, k, v, lam):
    s = jnp.einsum('bqhcd,bkhcd->bhcqk', q, k, preferred_element_type=jnp.float32) * (DA_QK_DIM ** -0.5)
    p = jax.nn.softmax(s, axis=-1)
    w = p[:, :, 0] - lam * p[:, :, 1]
    return jnp.einsum('bhqk,bkhv->bqhv', w.astype(v.dtype), v)


def diff_attend_blocks(q, k, v, lam):
    B, T = q.shape[:2]
    nb = T // Q_BLOCK
    qb = jnp.moveaxis(q.reshape((B, nb, Q_BLOCK) + q.shape[2:]), 1, 0)
    ob = lax.map(lambda qq: diff_attend(qq, k, v, lam), qb)
    return jnp.moveaxis(ob, 0, 1).reshape((B, T) + ob.shape[3:])


def da_branch(q_l, k_l, v_l, q_c, k_c, v_c, q_g, k_g, lam_vec, sub_g, lam_init, cos, sin, need_ctx):
    def heads(q, k, v):
        B, T, _ = q.shape
        q = rms_norm(q.reshape(B, T, DA_HEADS, 2, DA_QK_DIM), q_g)
        k = rms_norm(k.reshape(B, T, DA_HEADS, 2, DA_QK_DIM), k_g)
        return q, k, v.reshape(B, T, DA_HEADS, DA_V_DIM)

    def post(o):
        B, T = o.shape[:2]
        return (rms_norm(o, sub_g) * (1.0 - lam_init)).reshape(B, T, BRANCH_WIDTH)

    ql, kl, vl = heads(q_l, k_l, v_l)
    ql = apply_axial_rope(ql, cos, sin)
    kl = apply_axial_rope(kl, cos, sin)
    qc, kc, vc = heads(q_c, k_c, v_c)
    lv = lam_vec.astype(jnp.float32)
    lam = jnp.exp(jnp.sum(lv[0] * lv[1])) - jnp.exp(jnp.sum(lv[2] * lv[3])) + lam_init
    k_all = jnp.concatenate([kl, kc], axis=1)
    v_all = jnp.concatenate([vl, vc], axis=1)
    out_l = post(diff_attend_blocks(ql, k_all, v_all, lam))
    out_c = post(diff_attend(qc, kc, vc, lam)) if need_ctx else None
    return out_l, out_c


def rglru_coeffs(xc, gate_w, gate_b, lam):
    B, T, W = xc.shape
    xf = xc.astype(jnp.float32)
    xb = xf.reshape(B, T, LRU_BLOCKS, LRU_BLOCK_DIM)
    g = jnp.einsum('btnd,gnde->gbtne', xb, gate_w.astype(jnp.float32)).reshape(2, B, T, W)
    g = g + gate_b.astype(jnp.float32)[:, None, None, :]
    r = jax.nn.sigmoid(g[0])
    i = jax.nn.sigmoid(g[1])
    log_a = -LRU_C * r * jax.nn.softplus(-lam.astype(jnp.float32))
    a = jnp.exp(log_a)
    b = jnp.sqrt(-jnp.expm1(2.0 * log_a)) * (i * xf)
    return a, b


def linear_scan(a, b, h0):
    b = b.at[:, 0].add(a[:, 0] * h0)

    def combine(e1, e2):
        return e1[0] * e2[0], e2[0] * e1[1] + e2[1]

    _, h = lax.associative_scan(combine, (a, b), axis=1)
    return h


def rglru_branch(x_l, y_l, x_c, y_c, conv_w, conv_b, gate_w, gate_b, lam, need_ctx):
    x_l = dwconv_centred(x_l, conv_w, conv_b)
    x_c = dwconv_centred(x_c, conv_w, conv_b)
    h0 = jnp.zeros((x_c.shape[0], BRANCH_WIDTH), jnp.float32)
    h_l = 0.0
    h_c = 0.0
    for d in range(N_DIR):
        a_l, b_l = rglru_coeffs(x_l, gate_w[d], gate_b[d], lam[d])
        a_c, b_c = rglru_coeffs(x_c, gate_w[d], gate_b[d], lam[d])
        if d == 1:
            a_l, b_l, a_c, b_c = flip_t(a_l), flip_t(b_l), flip_t(a_c), flip_t(b_c)
        hc = linear_scan(a_c, b_c, h0)
        hl = linear_scan(a_l, b_l, hc[:, -1])
        if d == 1:
            hc, hl = flip_t(hc), flip_t(hl)
        h_l = h_l + hl
        h_c = h_c + hc
    out_l = jax.nn.gelu(y_l) * h_l.astype(y_l.dtype)
    out_c = jax.nn.gelu(y_c) * h_c.astype(y_c.dtype) if need_ctx else None
    return out_l, out_c


def gdn_chunked(q, k, v, g, beta, s0):
    B, T, H, dk = q.shape
    dv = v.shape[-1]
    C = GDN_CHUNK
    N = T // C
    f32 = jnp.float32

    def chunks(t):
        t = t.astype(f32).reshape((B, N, C, H) + t.shape[3:])
        return jnp.moveaxis(t, (1, 3), (0, 2))

    qc, kc, vc = chunks(q), chunks(k), chunks(v)
    gc = jnp.cumsum(chunks(g), axis=-1)
    bc = chunks(beta)
    idx = jnp.arange(C)
    incl = idx[:, None] >= idx[None, :]
    strict = (idx[:, None] > idx[None, :]).astype(f32)
    decay = jnp.exp(jnp.where(incl, gc[..., :, None] - gc[..., None, :], -jnp.inf))
    kb = kc * bc[..., None]
    lower = jnp.einsum('nbhid,nbhjd->nbhij', kb, kc) * decay * strict
    a_mat = lower + jnp.eye(C, dtype=f32)
    rhs = jnp.concatenate([vc * bc[..., None], kb * jnp.exp(gc)[..., None]], axis=-1)
    sol = lax.linalg.triangular_solve(a_mat, rhs, left_side=True, lower=True, unit_diagonal=True)
    u, w = sol[..., :dv], sol[..., dv:]
    attn = jnp.einsum('nbhid,nbhjd->nbhij', qc, kc) * decay
    q_dec = qc * jnp.exp(gc)[..., None]
    k_dec = kc * jnp.exp(gc[..., -1:] - gc)[..., None]
    g_end = jnp.exp(gc[..., -1])

    def step(s, xs):
        u_n, w_n, qd_n, kd_n, at_n, ge_n = xs
        v_new = u_n - jnp.einsum('bhcd,bhdv->bhcv', w_n, s)
        o_n = jnp.einsum('bhcd,bhdv->bhcv', qd_n, s) + jnp.einsum('bhij,bhjv->bhiv', at_n, v_new)
        s = s * ge_n[..., None, None] + jnp.einsum('bhcd,bhcv->bhdv', kd_n, v_new)
        return s, o_n

    s_fin, o = lax.scan(step, s0.astype(f32), (u, w, q_dec, k_dec, attn, g_end))
    o = jnp.moveaxis(o, (0, 2), (1, 3)).reshape(B, T, H, dv)
    return o, s_fin


def gdn_branch(qkv_l, z_l, b_l, a_l, qkv_c, z_c, b_c, a_c, conv_w, a_log, dt_bias, norm_g, need_ctx):
    def prep(qkv, b_raw, a_raw):
        B, T, _ = qkv.shape
        qkv = jax.nn.silu(dwconv_centred(qkv, conv_w))
        q, k, v = jnp.split(qkv, [GDN_HEADS * GDN_DK, 2 * GDN_HEADS * GDN_DK], axis=-1)
        q = l2_norm(q.reshape(B, T, GDN_HEADS, GDN_DK)) * (GDN_DK ** -0.5)
        k = l2_norm(k.reshape(B, T, GDN_HEADS, GDN_DK))
        v = v.reshape(B, T, GDN_HEADS, GDN_DV)
        b_raw = b_raw.astype(jnp.float32).reshape(B, T, N_DIR, GDN_HEADS)
        a_raw = a_raw.astype(jnp.float32).reshape(B, T, N_DIR, GDN_HEADS)
        return q, k, v, b_raw, a_raw

    def gates(b_raw, a_raw, d):
        rate = jnp.exp(a_log[d].astype(jnp.float32))
        g = -rate * jax.nn.softplus(a_raw[:, :, d] + dt_bias[d].astype(jnp.float32))
        return g, jax.nn.sigmoid(b_raw[:, :, d])

    def out(o, z):
        B, T = z.shape[:2]
        zh = z.astype(jnp.float32).reshape(B, T, GDN_HEADS, GDN_DV)
        return (rms_norm(o, norm_g) * jax.nn.silu(zh)).reshape(B, T, BRANCH_WIDTH).astype(z.dtype)

    ql, kl, vl, bl, al = prep(qkv_l, b_l, a_l)
    qc, kc, vc, bc, ac = prep(qkv_c, b_c, a_c)
    s0 = jnp.zeros((qc.shape[0], GDN_HEADS, GDN_DK, GDN_DV), jnp.float32)
    o_l = 0.0
    o_c = 0.0
    for d in range(N_DIR):
        g_l, beta_l = gates(bl, al, d)
        g_c, beta_c = gates(bc, ac, d)
        args_l = (ql, kl, vl, g_l, beta_l)
        args_c = (qc, kc, vc, g_c, beta_c)
        if d == 1:
            args_l = tuple(flip_t(t) for t in args_l)
            args_c = tuple(flip_t(t) for t in args_c)
        oc, s_ctx = gdn_chunked(*args_c, s0)
        ol, _ = gdn_chunked(*args_l, s_ctx)
        if d == 1:
            oc, ol = flip_t(oc), flip_t(ol)
        o_l = o_l + ol
        o_c = o_c + oc
    out_l = out(o_l, z_l)
    out_c = out(o_c, z_c) if need_ctx else None
    return out_l, out_c


def merge_branches(outs, gate_cols, w_branch, w_out):
    gates = jax.nn.sigmoid(gate_cols.astype(jnp.float32)).astype(gate_cols.dtype)
    gates = gates.reshape(gates.shape[:-1] + (N_BRANCH, D_MODEL))
    merged = sum(gates[..., i, :] * (outs[i] @ w_branch[i]) for i in range(N_BRANCH))
    return merged @ w_out


def sq_relu_mlp(h, w1, w2):
    return jnp.square(jax.nn.relu(h @ w1)) @ w2


def hybrid_layer(x, ctx, c_act, cctx_act, ada_w, ada_b, norm1_g, norm2_g, w_in,
                 da_q_norm_g, da_k_norm_g, da_lambda, da_sub_norm_g,
                 lru_conv_w, lru_conv_b, lru_gate_w, lru_gate_b, lru_lambda,
                 gdn_conv_w, gdn_A_log, gdn_dt_bias, gdn_norm_g,
                 w_branch, w_out, mlp_w1, mlp_w2, cos, sin, lam_init, need_ctx):
    mod_l = (c_act @ ada_w + ada_b)[:, None, :]
    mod_c = cctx_act @ ada_w + ada_b
    sh1_l, sc1_l, g1_l, sh2_l, sc2_l, g2_l = jnp.split(mod_l, N_MOD, axis=-1)
    sh1_c, sc1_c, g1_c, sh2_c, sc2_c, g2_c = jnp.split(mod_c, N_MOD, axis=-1)

    h_l = modulate(rms_norm(x, norm1_g), sh1_l, sc1_l)
    h_c = modulate(rms_norm(ctx, norm1_g), sh1_c, sc1_c)
    daq_l, dak_l, dav_l, lx_l, ly_l, gqkv_l, gz_l, gb_l, ga_l, gate_l = split_proj(h_l @ w_in)
    daq_c, dak_c, dav_c, lx_c, ly_c, gqkv_c, gz_c, gb_c, ga_c, gate_c = split_proj(h_c @ w_in)

    da_l, da_c = da_branch(daq_l, dak_l, dav_l, daq_c, dak_c, dav_c, da_q_norm_g, da_k_norm_g,
                           da_lambda, da_sub_norm_g, lam_init, cos, sin, need_ctx)
    lru_l, lru_c = rglru_branch(lx_l, ly_l, lx_c, ly_c, lru_conv_w, lru_conv_b, lru_gate_w,
                                lru_gate_b, lru_lambda, need_ctx)
    gdn_l, gdn_c = gdn_branch(gqkv_l, gz_l, gb_l, ga_l, gqkv_c, gz_c, gb_c, ga_c, gdn_conv_w,
                              gdn_A_log, gdn_dt_bias, gdn_norm_g, need_ctx)

    x = x + g1_l * merge_branches((da_l, lru_l, gdn_l), gate_l, w_branch, w_out)
    h2_l = modulate(rms_norm(x, norm2_g), sh2_l, sc2_l)
    x = x + g2_l * sq_relu_mlp(h2_l, mlp_w1, mlp_w2)
    if need_ctx:
        ctx = ctx + g1_c * merge_branches((da_c, lru_c, gdn_c), gate_c, w_branch, w_out)
        h2_c = modulate(rms_norm(ctx, norm2_g), sh2_c, sc2_c)
        ctx = ctx + g2_c * sq_relu_mlp(h2_c, mlp_w1, mlp_w2)
    return x, ctx


def setup_inputs(seed: int = 0) -> dict:
    key = jax.random.key(seed)
    ks = jax.random.split(key, 32)
    f32 = jnp.float32
    L = DEPTH

    def nrm(k, shape, scale):
        return jax.random.normal(k, shape, f32) * scale

    x = nrm(ks[0], (BATCH, SEQ, D_MODEL), 1.0)
    c = nrm(ks[1], (BATCH, D_MODEL), 1.0)
    ctx = nrm(ks[2], (BATCH, CTX_LEN, D_MODEL), 1.0)
    c_ctx = nrm(ks[3], (D_MODEL,), 1.0)
    ada_w = nrm(ks[4], (L, D_MODEL, N_MOD * D_MODEL), 0.5 * D_MODEL ** -0.5)
    ada_b = nrm(ks[5], (L, N_MOD * D_MODEL), 0.02)
    norm1_g = 1.0 + nrm(ks[6], (L, D_MODEL), 0.02)
    norm2_g = 1.0 + nrm(ks[7], (L, D_MODEL), 0.02)
    w_in = nrm(ks[8], (L, D_MODEL, N_IN), D_MODEL ** -0.5)
    da_q_norm_g = 1.0 + nrm(ks[9], (L, DA_QK_DIM), 0.02)
    da_k_norm_g = 1.0 + nrm(ks[10], (L, DA_QK_DIM), 0.02)
    da_lambda = nrm(ks[11], (L, 4, DA_QK_DIM), 0.1)
    da_sub_norm_g = 1.0 + nrm(ks[12], (L, DA_V_DIM), 0.02)
    lru_conv_w = nrm(ks[13], (L, CONV_K, BRANCH_WIDTH), CONV_K ** -0.5)
    lru_conv_b = nrm(ks[14], (L, BRANCH_WIDTH), 0.02)
    lru_gate_w = nrm(ks[15], (L, N_DIR, 2, LRU_BLOCKS, LRU_BLOCK_DIM, LRU_BLOCK_DIM), LRU_BLOCK_DIM ** -0.5)
    lru_gate_b = nrm(ks[16], (L, N_DIR, 2, BRANCH_WIDTH), 0.1)
    a0 = jax.random.uniform(ks[17], (L, N_DIR, BRANCH_WIDTH), f32, 0.9, 0.999) ** (1.0 / LRU_C)
    lru_lambda = jnp.log(a0) - jnp.log1p(-a0)
    gdn_conv_w = nrm(ks[18], (L, CONV_K, GDN_QKV_WIDTH), CONV_K ** -0.5)
    gdn_A_log = jnp.log(jax.random.uniform(ks[19], (L, N_DIR, GDN_HEADS), f32, 1.0, 16.0))
    dt = jnp.exp(jax.random.uniform(ks[20], (L, N_DIR, GDN_HEADS), f32, math.log(1e-3), math.log(1e-1)))
    gdn_dt_bias = dt + jnp.log(-jnp.expm1(-dt))
    gdn_norm_g = 1.0 + nrm(ks[21], (L, GDN_DV), 0.02)
    w_branch = nrm(ks[22], (L, N_BRANCH, BRANCH_WIDTH, D_MODEL), BRANCH_WIDTH ** -0.5)
    w_out = nrm(ks[23], (L, D_MODEL, D_MODEL), D_MODEL ** -0.5)
    mlp_w1 = nrm(ks[24], (L, D_MODEL, D_FF), D_MODEL ** -0.5)
    mlp_w2 = nrm(ks[25], (L, D_FF, D_MODEL), D_FF ** -0.5)
    return {'x': x, 'c': c, 'ctx': ctx, 'c_ctx': c_ctx, 'ada_w': ada_w, 'ada_b': ada_b,
            'norm1_g': norm1_g, 'norm2_g': norm2_g, 'w_in': w_in,
            'da_q_norm_g': da_q_norm_g, 'da_k_norm_g': da_k_norm_g, 'da_lambda': da_lambda,
            'da_sub_norm_g': da_sub_norm_g, 'lru_conv_w': lru_conv_w, 'lru_conv_b': lru_conv_b,
            'lru_gate_w': lru_gate_w, 'lru_gate_b': lru_gate_b, 'lru_lambda': lru_lambda,
            'gdn_conv_w': gdn_conv_w, 'gdn_A_log': gdn_A_log, 'gdn_dt_bias': gdn_dt_bias,
            'gdn_norm_g': gdn_norm_g, 'w_branch': w_branch, 'w_out': w_out,
            'mlp_w1': mlp_w1, 'mlp_w2': mlp_w2}


def reference(x, c, ctx, c_ctx, ada_w, ada_b, norm1_g, norm2_g, w_in,
              da_q_norm_g, da_k_norm_g, da_lambda, da_sub_norm_g,
              lru_conv_w, lru_conv_b, lru_gate_w, lru_gate_b, lru_lambda,
              gdn_conv_w, gdn_A_log, gdn_dt_bias, gdn_norm_g,
              w_branch, w_out, mlp_w1, mlp_w2):
    ROWS = x.shape[1] // GRID_W
    row = jnp.repeat(jnp.arange(ROWS, dtype=jnp.int32), GRID_W)
    col = jnp.tile(jnp.arange(GRID_W, dtype=jnp.int32), ROWS)
    cos, sin = axial_rope_tables(row, col)
    c_act = jax.nn.silu(c)
    cctx_act = jax.nn.silu(c_ctx)
    for layer in range(DEPTH):
        lam_init = 0.8 - 0.6 * math.exp(-0.3 * layer)
        x, ctx = hybrid_layer(x, ctx, c_act, cctx_act, ada_w[layer], ada_b[layer],
                              norm1_g[layer], norm2_g[layer], w_in[layer],
                              da_q_norm_g[layer], da_k_norm_g[layer], da_lambda[layer], da_sub_norm_g[layer],
                              lru_conv_w[layer], lru_conv_b[layer], lru_gate_w[layer], lru_gate_b[layer],
                              lru_lambda[layer], gdn_conv_w[layer], gdn_A_log[layer], gdn_dt_bias[layer],
                              gdn_norm_g[layer], w_branch[layer], w_out[layer], mlp_w1[layer], mlp_w2[layer],
                              cos, sin, lam_init, layer < DEPTH - 1)
    return x
```

```python
import functools
import math

import jax
import jax.numpy as jnp
from jax import lax
from jax.experimental import pallas as pl
from jax.experimental.pallas import tpu as pltpu

F32 = jnp.float32
BF16 = jnp.bfloat16
HIGHEST = lax.Precision.HIGHEST

N_MOD = 6
BRANCH_WIDTH = 512
N_BRANCH = 3
DA_HEADS = 4
DA_QK_DIM = 64
DA_V_DIM = 128
ROPE_BASE = 10000.0
LRU_BLOCKS = 8
LRU_C = 8.0
GDN_HEADS = 4
GDN_DK = 128
GDN_DV = 128
GDN_CHUNK = 64
N_DIR = 2
GRID_W = 64
EPS = 1e-6
NEG_BIG = -1e30

ROW_TILE = 256
HALO_ROWS = 8
VMEM_LIMIT = 48 * 1024 * 1024

COL_GATE = 0


def _cparams(sem):
    return pltpu.CompilerParams(dimension_semantics=sem, vmem_limit_bytes=VMEM_LIMIT)


def _softplus(z):
    return jnp.maximum(z, 0.0) + jnp.log1p(jnp.exp(-jnp.abs(z)))


def _adaln_kernel(c_ref, w_ref, b_ref, o_ref):
    c = c_ref[...]
    a = c * jax.nn.sigmoid(c)
    o_ref[...] = jnp.dot(a, w_ref[...], preferred_element_type=F32, precision=HIGHEST) + b_ref[...]


def _adaln(c_all, w, b):
    R, D = c_all.shape
    N = w.shape[1]
    tn = 1536 if N % 1536 == 0 else N
    return pl.pallas_call(
        _adaln_kernel,
        grid=(N // tn,),
        in_specs=[pl.BlockSpec((R, D), lambda j: (0, 0)),
                  pl.BlockSpec((D, tn), lambda j: (0, j)),
                  pl.BlockSpec((1, tn), lambda j: (0, j))],
        out_specs=pl.BlockSpec((R, tn), lambda j: (0, j)),
        out_shape=jax.ShapeDtypeStruct((R, N), F32),
        compiler_params=_cparams(("arbitrary",)),
        name="adaln",
    )(c_all, w, b.reshape(1, N))


def _norm_mod_kernel(x_ref, mod_ref, g_ref, o_ref, *, which):
    x = x_ref[0]
    ms = jnp.mean(x * x, axis=-1, keepdims=True)
    y = x * lax.rsqrt(ms + EPS) * g_ref[...]
    sh = mod_ref[0, 3 * which:3 * which + 1, :]
    sc = mod_ref[0, 3 * which + 1:3 * which + 2, :]
    o_ref[0] = (y * (1.0 + sc) + sh).astype(BF16)


def _norm_mod(xa, mod3, g, which, n_ctx_tiles, ctx_row):
    B, Tt, D = xa.shape
    tm = ROW_TILE
    return pl.pallas_call(
        functools.partial(_norm_mod_kernel, which=which),
        grid=(B, Tt // tm),
        in_specs=[pl.BlockSpec((1, tm, D), lambda b, t: (b, t, 0)),
                  pl.BlockSpec((1, N_MOD, D), lambda b, t: (jnp.where(t < n_ctx_tiles, ctx_row, b), 0, 0)),
                  pl.BlockSpec((1, D), lambda b, t: (0, 0))],
        out_specs=pl.BlockSpec((1, tm, D), lambda b, t: (b, t, 0)),
        out_shape=jax.ShapeDtypeStruct((B, Tt, D), BF16),
        compiler_params=_cparams(("parallel", "parallel")),
        name="norm_mod",
    )(xa, mod3, g.reshape(1, D))


def _matmul_kernel(h_ref, w_ref, o_ref):
    o_ref[...] = jnp.dot(h_ref[...], w_ref[...], preferred_element_type=F32)


def _pick(n, cands):
    for c in cands:
        if n % c == 0:
            return c
    return n


def _matmul(h2, w, name):
    R, D = h2.shape
    N = w.shape[1]
    tm = _pick(R, (1024, 512, 256))
    tn = _pick(N, (1536, 1280, 1024, 512, 128))
    return pl.pallas_call(
        _matmul_kernel,
        grid=(N // tn, R // tm),
        in_specs=[pl.BlockSpec((tm, D), lambda j, i: (i, 0)),
                  pl.BlockSpec((D, tn), lambda j, i: (0, j))],
        out_specs=pl.BlockSpec((tm, tn), lambda j, i: (i, j)),
        out_shape=jax.ShapeDtypeStruct((R, N), F32),
        compiler_params=_cparams(("parallel", "parallel")),
        name=name,
    )(h2, w)


def _seg_mean(x2, s_ref):
    hi = x2.astype(BF16)
    lo = (x2 - hi.astype(F32)).astype(BF16)
    s = s_ref[...]
    return jnp.dot(hi, s, preferred_element_type=F32) + jnp.dot(lo, s, preferred_element_type=F32)


def _rope(y, cos, sin_signed, first_half):
    rot = jnp.where(first_half, pltpu.roll(y, 128 - 16, 1), pltpu.roll(y, 16, 1))
    return y * cos + rot * sin_signed


def _qkv_prep_kernel(q_ref, k_ref, v_ref, cos_ref, sin_ref, gq_ref, gk_ref, s_ref, qo_ref, ko_ref, vo_ref,
                     *, q_scale):
    cos = cos_ref[...]
    sin = sin_ref[...]
    tm = cos.shape[0]
    lane = lax.broadcasted_iota(jnp.int32, (tm, 128), 1)
    first_half = (lane % 32) < 16
    comp0 = lane < DA_QK_DIM

    q = q_ref[0]
    qn = q * lax.rsqrt(_seg_mean(q * q, s_ref) + EPS) * gq_ref[...]
    k = k_ref[0]
    kn = k * lax.rsqrt(_seg_mean(k * k, s_ref) + EPS) * gk_ref[...]
    for h in range(DA_HEADS):
        qh = _rope(qn[:, h * 128:(h + 1) * 128], cos, sin, first_half) * q_scale
        qo_ref[0, :, h * 256:h * 256 + 128] = jnp.where(comp0, qh, 0.0).astype(BF16)
        qo_ref[0, :, h * 256 + 128:(h + 1) * 256] = jnp.where(comp0, 0.0, qh).astype(BF16)
        kh = _rope(kn[:, h * 128:(h + 1) * 128], cos, sin, first_half)
        ko_ref[0, :, h * 128:(h + 1) * 128] = kh.astype(BF16)
    vo_ref[0] = v_ref[0].astype(BF16)


def _qkv_prep(P, cols, cos_t, sin_t, gq, gk):
    B, Tt, _ = P.shape
    tm = ROW_TILE
    W = BRANCH_WIDTH
    seg = (jnp.arange(W)[:, None] // DA_QK_DIM == jnp.arange(W)[None, :] // DA_QK_DIM)
    smat = (seg.astype(F32) / DA_QK_DIM).astype(BF16)
    gq_t = jnp.tile(gq, W // DA_QK_DIM).reshape(1, W)
    gk_t = jnp.tile(gk, W // DA_QK_DIM).reshape(1, W)
    qb, kb, vb = cols["q"] // W, cols["k"] // W, cols["v"] // W
    q_scale = (DA_QK_DIM ** -0.5) * math.log2(math.e)
    return pl.pallas_call(
        functools.partial(_qkv_prep_kernel, q_scale=q_scale),
        grid=(B, Tt // tm),
        in_specs=[pl.BlockSpec((1, tm, W), lambda b, t: (b, t, qb)),
                  pl.BlockSpec((1, tm, W), lambda b, t: (b, t, kb)),
                  pl.BlockSpec((1, tm, W), lambda b, t: (b, t, vb)),
                  pl.BlockSpec((tm, 128), lambda b, t: (t, 0)),
                  pl.BlockSpec((tm, 128), lambda b, t: (t, 0)),
                  pl.BlockSpec((1, W), lambda b, t: (0, 0)),
                  pl.BlockSpec((1, W), lambda b, t: (0, 0)),
                  pl.BlockSpec((W, W), lambda b, t: (0, 0))],
        out_specs=[pl.BlockSpec((1, tm, 2 * W), lambda b, t: (b, t, 0)),
                   pl.BlockSpec((1, tm, W), lambda b, t: (b, t, 0)),
                   pl.BlockSpec((1, tm, W), lambda b, t: (b, t, 0))],
        out_shape=[jax.ShapeDtypeStruct((B, Tt, 2 * W), BF16),
                   jax.ShapeDtypeStruct((B, Tt, W), BF16),
                   jax.ShapeDtypeStruct((B, Tt, W), BF16)],
        compiler_params=_cparams(("parallel", "parallel")),
        name="qkv_prep",
    )(P, P, P, cos_t, sin_t, gq_t, gk_t, smat)


def _flash_kernel(q_ref, k_ref, v_ref, lam_ref, subg_ref, o_ref, m_s, l_s, acc_s, *, tq, nk, lam_init):
    j = pl.program_id(3)

    @pl.when(j == 0)
    def _():
        m_s[...] = jnp.full(m_s.shape, -jnp.inf, F32)
        l_s[...] = jnp.zeros(l_s.shape, F32)
        acc_s[...] = jnp.zeros(acc_s.shape, F32)

    q = q_ref[0]
    q2 = jnp.concatenate([q[:, :128], q[:, 128:]], axis=0)
    s = lax.dot_general(q2, k_ref[0], (((1,), (1,)), ((), ())), preferred_element_type=F32)
    m_old = m_s[...]
    m_new = jnp.maximum(m_old, jnp.max(s, axis=-1, keepdims=True))
    alpha = jnp.exp2(m_old - m_new)
    p = jnp.exp2(s - m_new[:, :1])
    l_s[...] = alpha * l_s[...] + jnp.sum(p, axis=-1, keepdims=True)
    acc_s[...] = alpha * acc_s[...] + jnp.dot(p.astype(BF16), v_ref[0], preferred_element_type=F32)
    m_s[...] = m_new

    @pl.when(j == nk - 1)
    def _():
        o = acc_s[...] / l_s[...]
        lv = lam_ref[...]
        s01 = jnp.sum(lv[0:1] * lv[1:2], axis=-1, keepdims=True)
        s23 = jnp.sum(lv[2:3] * lv[3:4], axis=-1, keepdims=True)
        lam = jnp.exp(s01) - jnp.exp(s23) + lam_init
        d = o[:tq] - lam * o[tq:]
        ms = jnp.mean(d * d, axis=-1, keepdims=True)
        o_ref[0] = d * lax.rsqrt(ms + EPS) * subg_ref[...] * (1.0 - lam_init)


def _flash(Qs, Ks, Vs, lam_vec, sub_g, lam_init, q_row0, n_q_rows, n_k_rows, tq, tk):
    B = Qs.shape[0]
    nq, nk = n_q_rows // tq, n_k_rows // tk
    q0 = q_row0 // tq
    return pl.pallas_call(
        functools.partial(_flash_kernel, tq=tq, nk=nk, lam_init=lam_init),
        grid=(B, DA_HEADS, nq, nk),
        in_specs=[pl.BlockSpec((1, tq, 256), lambda b, h, i, j: (b, q0 + i, h)),
                  pl.BlockSpec((1, tk, 128), lambda b, h, i, j: (b, j, h)),
                  pl.BlockSpec((1, tk, 128), lambda b, h, i, j: (b, j, h)),
                  pl.BlockSpec((4, DA_QK_DIM), lambda b, h, i, j: (0, 0)),
                  pl.BlockSpec((1, DA_V_DIM), lambda b, h, i, j: (0, 0))],
        out_specs=pl.BlockSpec((1, tq, 128), lambda b, h, i, j: (b, i, h)),
        out_shape=jax.ShapeDtypeStruct((B, n_q_rows, BRANCH_WIDTH), F32),
        scratch_shapes=[pltpu.VMEM((2 * tq, 128), F32), pltpu.VMEM((2 * tq, 128), F32),
                        pltpu.VMEM((2 * tq, 128), F32)],
        compiler_params=_cparams(("parallel", "parallel", "parallel", "arbitrary")),
        name="diff_attn",
    )(Qs, Ks, Vs, lam_vec, sub_g.reshape(1, DA_V_DIM))


def _scan_tile(j, n_ctx, nt, reverse):
    if not reverse:
        return j
    return jnp.where(j < n_ctx, n_ctx - 1 - j, nt - 1 - (j - n_ctx))


def _dwconv4(x, prev8, next8, w_ref, is_first, is_last):
    tm = x.shape[0]
    row = lax.broadcasted_iota(jnp.int32, x.shape, 0)
    pf = jnp.where(is_first, 0.0, 1.0)
    nf = jnp.where(is_last, 0.0, 1.0)
    prow = prev8[HALO_ROWS - 1:HALO_ROWS, :] * pf
    n0 = next8[0:1, :] * nf
    n1 = next8[1:2, :] * nf
    xm1 = jnp.where(row == 0, prow, pltpu.roll(x, 1, 0))
    xp1 = jnp.where(row == tm - 1, n0, pltpu.roll(x, tm - 1, 0))
    xp2 = jnp.where(row == tm - 2, n0, jnp.where(row == tm - 1, n1, pltpu.roll(x, tm - 2, 0)))
    return w_ref[0:1, :] * xm1 + w_ref[1:2, :] * x + w_ref[2:3, :] * xp1 + w_ref[3:4, :] * xp2


def _halo_specs(tm, width, col_blk, n_ctx, nt, reverse):
    r8 = tm // HALO_ROWS
    last8 = nt * r8 - 1

    def cur(b, j):
        return (b, _scan_tile(j, n_ctx, nt, reverse), col_blk)

    def prev(b, j):
        return (b, jnp.maximum(_scan_tile(j, n_ctx, nt, reverse) * r8 - 1, 0), col_blk)

    def nxt(b, j):
        return (b, jnp.minimum((_scan_tile(j, n_ctx, nt, reverse) + 1) * r8, last8), col_blk)

    return [pl.BlockSpec((1, tm, width), cur),
            pl.BlockSpec((1, HALO_ROWS, width), prev),
            pl.BlockSpec((1, HALO_ROWS, width), nxt)]


def _linear_scan_tile(a, b, reverse):
    tm = a.shape[0]
    row = lax.broadcasted_iota(jnp.int32, a.shape, 0)
    s = 1
    while s < tm:
        if not reverse:
            keep = row >= s
            ap = jnp.where(keep, pltpu.roll(a, s, 0), 1.0)
            bp = jnp.where(keep, pltpu.roll(b, s, 0), 0.0)
        else:
            keep = row < tm - s
            ap = jnp.where(keep, pltpu.roll(a, tm - s, 0), 1.0)
            bp = jnp.where(keep, pltpu.roll(b, tm - s, 0), 0.0)
        b = a * bp + b
        a = a * ap
        s *= 2
    return a, b


def _lru_kernel(*refs, n_ctx, nt, reverse, final):
    if final:
        (x_ref, xp_ref, xn_ref, y_ref, hf_ref, cw_ref, cb_ref, wg_ref, gb_ref, lam_ref, o_ref, carry) = refs
    else:
        (x_ref, xp_ref, xn_ref, cw_ref, cb_ref, wg_ref, gb_ref, lam_ref, o_ref, carry) = refs
    j = pl.program_id(1)
    t = _scan_tile(j, n_ctx, nt, reverse)
    is_first = jnp.logical_or(t == 0, t == n_ctx)
    is_last = jnp.logical_or(t == n_ctx - 1, t == nt - 1)

    @pl.when(j == 0)
    def _():
        carry[...] = jnp.zeros(carry.shape, F32)

    x = x_ref[0]
    tm, W = x.shape
    xc = _dwconv4(x, xp_ref[0], xn_ref[0], cw_ref, is_first, is_last) + cb_ref[...]
    g = jnp.dot(xc.astype(BF16), wg_ref[...], preferred_element_type=F32) + gb_ref[...]
    r = jax.nn.sigmoid(g[:, :W])
    i = jax.nn.sigmoid(g[:, W:])
    log_a = (-LRU_C) * r * _softplus(-lam_ref[...])
    a = jnp.exp(log_a)
    th = jnp.tanh(log_a)
    bb = jnp.sqrt(-2.0 * th / (1.0 - th)) * (i * xc)
    a_c, b_c = _linear_scan_tile(a, bb, reverse)
    h = b_c + a_c * carry[0:1, :]
    carry[0:1, :] = h[0:1, :] if reverse else h[tm - 1:tm, :]
    if final:
        o_ref[0] = jax.nn.gelu(y_ref[0]) * (hf_ref[0] + h)
    else:
        o_ref[0] = h


def _lru_pass(P, cols, conv_w, conv_b, wg, gb, lam, n_ctx, reverse, hf=None):
    B, Tt, _ = P.shape
    tm = ROW_TILE
    W = BRANCH_WIDTH
    nt = Tt // tm
    final = hf is not None

    def cur(b, j):
        return (b, _scan_tile(j, n_ctx, nt, reverse), 0)

    in_specs = _halo_specs(tm, W, cols["lx"] // W, n_ctx, nt, reverse)
    args = [P, P, P]
    if final:
        yb = cols["ly"] // W
        in_specs += [pl.BlockSpec((1, tm, W), lambda b, j: (b, _scan_tile(j, n_ctx, nt, reverse), yb)),
                     pl.BlockSpec((1, tm, W), cur)]
        args += [P, hf]
    in_specs += [pl.BlockSpec((4, W), lambda b, j: (0, 0)),
                 pl.BlockSpec((1, W), lambda b, j: (0, 0)),
                 pl.BlockSpec((W, 2 * W), lambda b, j: (0, 0)),
                 pl.BlockSpec((1, 2 * W), lambda b, j: (0, 0)),
                 pl.BlockSpec((1, W), lambda b, j: (0, 0))]
    args += [conv_w, conv_b.reshape(1, W), wg, gb.reshape(1, 2 * W), lam.reshape(1, W)]
    return pl.pallas_call(
        functools.partial(_lru_kernel, n_ctx=n_ctx, nt=nt, reverse=reverse, final=final),
        grid=(B, nt),
        in_specs=in_specs,
        out_specs=pl.BlockSpec((1, tm, W), cur),
        out_shape=jax.ShapeDtypeStruct((B, Tt, W), F32),
        scratch_shapes=[pltpu.VMEM((HALO_ROWS, W), F32)],
        compiler_params=_cparams(("parallel", "arbitrary")),
        name="rglru_bwd" if reverse else "rglru_fwd",
    )(*args)


def _lru_gate_weights(gate_w_d):
    mats = []
    for g in range(2):
        mats.append(jax.scipy.linalg.block_diag(*[gate_w_d[g, n] for n in range(LRU_BLOCKS)]))
    return jnp.concatenate(mats, axis=1).astype(BF16)


def _gdn_kernel(*refs, n_ctx, nt, reverse, final, d):
    if final:
        (x_ref, xp_ref, xn_ref, sm_ref, z_ref, of_ref, cw_ref, rate_ref, dtb_ref, ng_ref, o_ref, S_s, qkv_s) = refs
    else:
        (x_ref, xp_ref, xn_ref, sm_ref, cw_ref, rate_ref, dtb_ref, o_ref, S_s, qkv_s) = refs
    j = pl.program_id(1)
    t = _scan_tile(j, n_ctx, nt, reverse)
    is_first = jnp.logical_or(t == 0, t == n_ctx)
    is_last = jnp.logical_or(t == n_ctx - 1, t == nt - 1)

    @pl.when(j == 0)
    def _():
        S_s[...] = jnp.zeros(S_s.shape, F32)

    x = x_ref[0]
    tm = x.shape[0]
    C = GDN_CHUNK
    nC = tm // C
    HK = GDN_HEADS * GDN_DK
    qkv = _dwconv4(x, xp_ref[0], xn_ref[0], cw_ref, is_first, is_last)
    qkv_s[...] = qkv * jax.nn.sigmoid(qkv)

    sm = sm_ref[0]
    beta_all = jax.nn.sigmoid(sm)
    g_all = -jnp.exp(rate_ref[...]) * _softplus(sm + dtb_ref[...])

    ii = lax.broadcasted_iota(jnp.int32, (tm, tm), 0)
    jj = lax.broadcasted_iota(jnp.int32, (tm, tm), 1)
    same = (ii // C) == (jj // C)
    if not reverse:
        incl = jnp.logical_and(same, ii >= jj)
        strict = jnp.logical_and(same, ii > jj)
    else:
        incl = jnp.logical_and(same, ii <= jj)
        strict = jnp.logical_and(same, ii < jj)
    cum_mat = jnp.concatenate([incl.astype(F32), same.astype(F32)], axis=0)
    eye = (ii == jj).astype(F32)

    for h in range(GDN_HEADS):
        q = qkv_s[:, h * GDN_DK:(h + 1) * GDN_DK]
        k = qkv_s[:, HK + h * GDN_DK:HK + (h + 1) * GDN_DK]
        v = qkv_s[:, 2 * HK + h * GDN_DV:2 * HK + (h + 1) * GDN_DV]
        qn = q * lax.rsqrt(jnp.sum(q * q, axis=-1, keepdims=True) + EPS) * (GDN_DK ** -0.5)
        kn = k * lax.rsqrt(jnp.sum(k * k, axis=-1, keepdims=True) + EPS)
        cb = d * GDN_HEADS + h
        cg = N_DIR * GDN_HEADS + d * GDN_HEADS + h
        beta_b = jnp.broadcast_to(beta_all[:, cb:cb + 1], (tm, 128))
        g_b = jnp.broadcast_to(g_all[:, cg:cg + 1], (tm, 128))
        cums = jnp.dot(cum_mat, g_b, preferred_element_type=F32, precision=HIGHEST)
        gc_b = cums[:tm]
        gtot_b = cums[tm:]
        eg = jnp.exp(gc_b)
        kb = kn * beta_b
        rhs = jnp.concatenate([v * beta_b, kb * eg], axis=1)
        q_dec = qn * eg
        k_dec = kn * jnp.exp(gtot_b - gc_b)
        g_end = jnp.exp(gtot_b)

        gc_cols = jnp.concatenate([gc_b] * (tm // 128), axis=1)
        gc_rows = gc_b.T[0:1, :]
        decay = jnp.exp(jnp.where(incl, gc_cols - gc_rows, NEG_BIG))
        kq = lax.dot_general(jnp.concatenate([kb, qn], axis=0).astype(BF16), kn.astype(BF16),
                             (((1,), (1,)), ((), ())), preferred_element_type=F32)
        m_k = jnp.where(strict, -(kq[:tm] * decay), 0.0)
        attn = kq[tm:] * decay

        t_inv = eye + m_k
        p_k = jnp.dot(m_k, m_k, preferred_element_type=F32, precision=HIGHEST)
        for _ in range(4):
            r2 = jnp.dot(jnp.concatenate([t_inv, p_k], axis=0), p_k, preferred_element_type=F32,
                         precision=HIGHEST)
            t_inv = t_inv + r2[:tm]
            p_k = r2[tm:]
        t_inv = t_inv + jnp.dot(t_inv, p_k, preferred_element_type=F32, precision=HIGHEST)
        sol = jnp.dot(t_inv, rhs, preferred_element_type=F32, precision=HIGHEST)
        u = sol[:, :GDN_DV]
        w = sol[:, GDN_DV:]

        S = S_s[h]
        for cc in range(nC):
            c = nC - 1 - cc if reverse else cc
            r0 = c * C
            wq = jnp.concatenate([w[r0:r0 + C], q_dec[r0:r0 + C]], axis=0).astype(BF16)
            ws = jnp.dot(wq, S.astype(BF16), preferred_element_type=F32)
            v_new = u[r0:r0 + C] - ws[:C]
            o_c = ws[C:] + jnp.dot(attn[r0:r0 + C, r0:r0 + C].astype(BF16), v_new.astype(BF16),
                                   preferred_element_type=F32)
            S = S * _col(g_end[r0:r0 + 1, :], GDN_DK) + lax.dot_general(
                k_dec[r0:r0 + C].astype(BF16), v_new.astype(BF16), (((0,), (0,)), ((), ())),
                preferred_element_type=F32)
            if final:
                o_t = of_ref[0, r0:r0 + C, h * GDN_DV:(h + 1) * GDN_DV] + o_c
                ms = jnp.mean(o_t * o_t, axis=-1, keepdims=True)
                z = z_ref[0, r0:r0 + C, h * GDN_DV:(h + 1) * GDN_DV]
                o_ref[0, r0:r0 + C, h * GDN_DV:(h + 1) * GDN_DV] = (
                    o_t * lax.rsqrt(ms + EPS) * ng_ref[...] * (z * jax.nn.sigmoid(z)))
            else:
                o_ref[0, r0:r0 + C, h * GDN_DV:(h + 1) * GDN_DV] = o_c
        S_s[h] = S


def _col(row_vec, n):
    return jnp.broadcast_to(row_vec, (n, row_vec.shape[1]))


def _gdn_pass(P, Psm, cols, conv_w, rate_row, dtb_row, n_ctx, reverse, d, of=None, norm_g=None):
    B, Tt, _ = P.shape
    tm = ROW_TILE
    W = BRANCH_WIDTH
    QW = GDN_HEADS * (2 * GDN_DK + GDN_DV)
    nt = Tt // tm
    final = of is not None

    def cur(b, j):
        return (b, _scan_tile(j, n_ctx, nt, reverse), 0)

    in_specs = _halo_specs(tm, QW, cols["gqkv"] // QW, n_ctx, nt, reverse)
    in_specs += [pl.BlockSpec((1, tm, 128), cur)]
    args = [P, P, P, Psm]
    if final:
        zb = cols["gz"] // W
        in_specs += [pl.BlockSpec((1, tm, W), lambda b, j: (b, _scan_tile(j, n_ctx, nt, reverse), zb)),
                     pl.BlockSpec((1, tm, W), cur)]
        args += [P, of]
    in_specs += [pl.BlockSpec((4, QW), lambda b, j: (0, 0)),
                 pl.BlockSpec((1, 128), lambda b, j: (0, 0)),
                 pl.BlockSpec((1, 128), lambda b, j: (0, 0))]
    args += [conv_w, rate_row, dtb_row]
    if final:
        in_specs += [pl.BlockSpec((1, GDN_DV), lambda b, j: (0, 0))]
        args += [norm_g.reshape(1, GDN_DV)]
    return pl.pallas_call(
        functools.partial(_gdn_kernel, n_ctx=n_ctx, nt=nt, reverse=reverse, final=final, d=d),
        grid=(B, nt),
        in_specs=in_specs,
        out_specs=pl.BlockSpec((1, tm, W), cur),
        out_shape=jax.ShapeDtypeStruct((B, Tt, W), F32),
        scratch_shapes=[pltpu.VMEM((GDN_HEADS, GDN_DK, GDN_DV), F32), pltpu.VMEM((tm, QW), F32)],
        compiler_params=_cparams(("parallel", "arbitrary")),
        name="gdn_bwd" if reverse else "gdn_fwd",
    )(*args)


def _merge_kernel(x_ref, da_ref, lru_ref, gdn_ref, g0_ref, g1_ref, g2_ref, mod_ref, wb_ref, wo_ref, o_ref):
    outs = (da_ref, lru_ref, gdn_ref)
    gates = (g0_ref, g1_ref, g2_ref)
    merged = None
    for i in range(N_BRANCH):
        y = jnp.dot(outs[i][0].astype(BF16), wb_ref[i], preferred_element_type=F32)
        term = jax.nn.sigmoid(gates[i][0]) * y
        merged = term if merged is None else merged + term
    proj = jnp.dot(merged.astype(BF16), wo_ref[...], preferred_element_type=F32)
    o_ref[0] = x_ref[0] + mod_ref[0, 2:3, :] * proj


def _merge(xa, da, lru, gdn, P, mod3, wb, wo, n_ctx_tiles, ctx_row):
    B, Tt, D = xa.shape
    tm = ROW_TILE
    W = BRANCH_WIDTH
    row = lambda b, t: (b, t, 0)
    return pl.pallas_call(
        _merge_kernel,
        grid=(B, Tt // tm),
        in_specs=[pl.BlockSpec((1, tm, D), row),
                  pl.BlockSpec((1, tm, W), row), pl.BlockSpec((1, tm, W), row), pl.BlockSpec((1, tm, W), row),
                  pl.BlockSpec((1, tm, D), lambda b, t: (b, t, 0)),
                  pl.BlockSpec((1, tm, D), lambda b, t: (b, t, 1)),
                  pl.BlockSpec((1, tm, D), lambda b, t: (b, t, 2)),
                  pl.BlockSpec((1, N_MOD, D), lambda b, t: (jnp.where(t < n_ctx_tiles, ctx_row, b), 0, 0)),
                  pl.BlockSpec((N_BRANCH, W, D), lambda b, t: (0, 0, 0)),
                  pl.BlockSpec((D, D), lambda b, t: (0, 0))],
        out_specs=pl.BlockSpec((1, tm, D), row),
        out_shape=jax.ShapeDtypeStruct((B, Tt, D), F32),
        compiler_params=_cparams(("parallel", "parallel")),
        name="merge",
    )(xa, da, lru, gdn, P, P, P, mod3, wb, wo)


def _mlp_kernel(x_ref, mod_ref, g_ref, w1_ref, w2_ref, o_ref, *, ff_chunk):
    x = x_ref[0]
    ms = jnp.mean(x * x, axis=-1, keepdims=True)
    y = x * lax.rsqrt(ms + EPS) * g_ref[...]
    h = (y * (1.0 + mod_ref[0, 4:5, :]) + mod_ref[0, 3:4, :]).astype(BF16)
    n_ff = w1_ref.shape[1]
    acc = None
    for c0 in range(0, n_ff, ff_chunk):
        a = jnp.dot(h, w1_ref[:, c0:c0 + ff_chunk], preferred_element_type=F32)
        a = jnp.square(jnp.maximum(a, 0.0)).astype(BF16)
        part = jnp.dot(a, w2_ref[c0:c0 + ff_chunk, :], preferred_element_type=F32)
        acc = part if acc is None else acc + part
    o_ref[0] = x + mod_ref[0, 5:6, :] * acc


def _mlp(xa, mod3, g, w1, w2, n_ctx_tiles, ctx_row, row0, n_rows):
    B, Tt, D = xa.shape
    tm = ROW_TILE
    F = w1.shape[1]
    t0 = row0 // tm
    return pl.pallas_call(
        functools.partial(_mlp_kernel, ff_chunk=min(F, 1024)),
        grid=(B, n_rows // tm),
        in_specs=[pl.BlockSpec((1, tm, D), lambda b, t: (b, t0 + t, 0)),
                  pl.BlockSpec((1, N_MOD, D), lambda b, t: (jnp.where(t0 + t < n_ctx_tiles, ctx_row, b), 0, 0)),
                  pl.BlockSpec((1, D), lambda b, t: (0, 0)),
                  pl.BlockSpec((D, F), lambda b, t: (0, 0)),
                  pl.BlockSpec((F, D), lambda b, t: (0, 0))],
        out_specs=pl.BlockSpec((1, tm, D), lambda b, t: (b, t, 0)),
        out_shape=jax.ShapeDtypeStruct((B, n_rows, D), F32),
        compiler_params=_cparams(("parallel", "parallel")),
        name="mlp",
    )(xa, mod3, g.reshape(1, D), w1, w2)


def _proj_layout(D):
    W = BRANCH_WIDTH
    QW = GDN_HEADS * (2 * GDN_DK + GDN_DV)
    src = {}
    off = 0
    for name, width in (("q", W), ("k", W), ("v", W), ("lx", W), ("ly", W), ("gqkv", QW), ("gz", W),
                        ("gb", N_DIR * GDN_HEADS), ("ga", N_DIR * GDN_HEADS), ("gate", N_BRANCH * D)):
        src[name] = (off, width)
        off += width
    order = ("gate", "gqkv", "q", "k", "v", "lx", "ly", "gz")
    cols = {}
    idx = []
    o = 0
    for name in order:
        s, w = src[name]
        cols[name] = o
        idx.append(jnp.arange(s, s + w))
        o += w
    small_idx = jnp.concatenate([jnp.arange(src["gb"][0], src["gb"][0] + src["gb"][1]),
                                 jnp.arange(src["ga"][0], src["ga"][0] + src["ga"][1])])
    return cols, jnp.concatenate(idx), small_idx


def _rope_tables(n_ctx_rows, n_lat_rows):
    n_freq = DA_QK_DIM // 4
    inv = ROPE_BASE ** (-jnp.arange(n_freq, dtype=F32) / n_freq)
    tpos = jnp.arange(n_lat_rows, dtype=jnp.int32)
    ang_r = (tpos // GRID_W).astype(F32)[:, None] * inv
    ang_c = (tpos % GRID_W).astype(F32)[:, None] * inv
    ang = jnp.concatenate([ang_r, ang_r, ang_c, ang_c], axis=-1)
    ang = jnp.concatenate([ang, ang], axis=-1)
    sign = jnp.where((jnp.arange(128) % 32) < 16, -1.0, 1.0).astype(F32)
    cos = jnp.concatenate([jnp.ones((n_ctx_rows, 128), F32), jnp.cos(ang)], axis=0)
    sin = jnp.concatenate([jnp.zeros((n_ctx_rows, 128), F32), jnp.sin(ang) * sign], axis=0)
    return cos, sin


def kernel(x, c, ctx, c_ctx, ada_w, ada_b, norm1_g, norm2_g, w_in, da_q_norm_g, da_k_norm_g, da_lambda, da_sub_norm_g, lru_conv_w, lru_conv_b, lru_gate_w, lru_gate_b, lru_lambda, gdn_conv_w, gdn_A_log, gdn_dt_bias, gdn_norm_g, w_branch, w_out, mlp_w1, mlp_w2):
    B, T, D = x.shape
    Tc = ctx.shape[1]
    depth = ada_w.shape[0]
    tm = ROW_TILE
    assert Tc % tm == 0 and T % tm == 0 and B + 1 <= 8
    n_ctx = Tc // tm
    Tt = Tc + T
    ctx_row = B

    cols, main_idx, small_idx = _proj_layout(D)
    cos_t, sin_t = _rope_tables(Tc, T)
    xa = jnp.concatenate([ctx, x], axis=1)
    c_all = jnp.concatenate([c, c_ctx[None, :], jnp.zeros((8 - B - 1, D), F32)], axis=0)

    tk = _pick(Tt, (768, 512, 256))
    tq = 256

    out = None
    for layer in range(depth):
        last = layer == depth - 1
        lam_init = 0.8 - 0.6 * math.exp(-0.3 * layer)
        w_main = w_in[layer][:, main_idx].astype(BF16)
        w_small = jnp.pad(w_in[layer][:, small_idx], ((0, 0), (0, 128 - small_idx.shape[0]))).astype(BF16)

        mod3 = _adaln(c_all, ada_w[layer], ada_b[layer]).reshape(8, N_MOD, D)
        h1 = _norm_mod(xa, mod3, norm1_g[layer], 0, n_ctx, ctx_row).reshape(B * Tt, D)
        P = _matmul(h1, w_main, "in_proj").reshape(B, Tt, -1)
        Psm = _matmul(h1, w_small, "in_proj_small").reshape(B, Tt, 128)

        Qs, Ks, Vs = _qkv_prep(P, cols, cos_t, sin_t, da_q_norm_g[layer], da_k_norm_g[layer])
        da_l = _flash(Qs, Ks, Vs, da_lambda[layer], da_sub_norm_g[layer], lam_init, Tc, T, Tt, tq, tk)
        if last:
            da = jnp.pad(da_l, ((0, 0), (Tc, 0), (0, 0)))
        else:
            da_c = _flash(Qs, Ks, Vs, da_lambda[layer], da_sub_norm_g[layer], lam_init, 0, Tc, Tc, tm, tm)
            da = jnp.concatenate([da_c, da_l], axis=1)

        hf = _lru_pass(P, cols, lru_conv_w[layer], lru_conv_b[layer], _lru_gate_weights(lru_gate_w[layer, 0]),
                       lru_gate_b[layer, 0].reshape(-1), lru_lambda[layer, 0], n_ctx, False)
        lru = _lru_pass(P, cols, lru_conv_w[layer], lru_conv_b[layer], _lru_gate_weights(lru_gate_w[layer, 1]),
                        lru_gate_b[layer, 1].reshape(-1), lru_lambda[layer, 1], n_ctx, True, hf=hf)

        rate_row = jnp.zeros((1, 128), F32).at[0, N_DIR * GDN_HEADS:2 * N_DIR * GDN_HEADS].set(
            gdn_A_log[layer].astype(F32).reshape(-1))
        dtb_row = jnp.zeros((1, 128), F32).at[0, N_DIR * GDN_HEADS:2 * N_DIR * GDN_HEADS].set(
            gdn_dt_bias[layer].astype(F32).reshape(-1))
        of = _gdn_pass(P, Psm, cols, gdn_conv_w[layer], rate_row, dtb_row, n_ctx, False, 0)
        gdn = _gdn_pass(P, Psm, cols, gdn_conv_w[layer], rate_row, dtb_row, n_ctx, True, 1, of=of,
                        norm_g=gdn_norm_g[layer])

        xa = _merge(xa, da, lru, gdn, P, mod3, w_branch[layer].astype(BF16), w_out[layer].astype(BF16),
                    n_ctx, ctx_row)
        if last:
            out = _mlp(xa, mod3, norm2_g[layer], mlp_w1[layer].astype(BF16), mlp_w2[layer].astype(BF16),
                       n_ctx, ctx_row, Tc, T)
        else:
            xa = _mlp(xa, mod3, norm2_g[layer], mlp_w1[layer].astype(BF16), mlp_w2[layer].astype(BF16),
                      n_ctx, ctx_row, 0, Tt)
    return out
```

```python
import functools
import math

import jax
import jax.numpy as jnp
from jax import lax
from jax.experimental import pallas as pl
from jax.experimental.pallas import tpu as pltpu

F32 = jnp.float32
BF16 = jnp.bfloat16
HIGHEST = lax.Precision.HIGHEST

N_MOD = 6
BRANCH_WIDTH = 512
N_BRANCH = 3
DA_HEADS = 4
DA_QK_DIM = 64
DA_V_DIM = 128
ROPE_BASE = 10000.0
LRU_BLOCKS = 8
LRU_C = 8.0
GDN_HEADS = 4
GDN_DK = 128
GDN_DV = 128
GDN_CHUNK = 64
N_DIR = 2
GRID_W = 64
EPS = 1e-6
NEG_BIG = -1e30

ROW_TILE = 256
HALO_ROWS = 8
VMEM_LIMIT = 48 * 1024 * 1024

COL_GATE = 0


def _cparams(sem):
    return pltpu.CompilerParams(dimension_semantics=sem, vmem_limit_bytes=VMEM_LIMIT)


def _softplus(z):
    return jnp.maximum(z, 0.0) + jnp.log1p(jnp.exp(-jnp.abs(z)))


def _adaln_kernel(c_ref, w_ref, b_ref, o_ref):
    c = c_ref[...]
    a = c * jax.nn.sigmoid(c)
    o_ref[...] = jnp.dot(a, w_ref[...], preferred_element_type=F32, precision=HIGHEST) + b_ref[...]


def _adaln(c_all, w, b):
    R, D = c_all.shape
    N = w.shape[1]
    tn = 1536 if N % 1536 == 0 else N
    return pl.pallas_call(
        _adaln_kernel,
        grid=(N // tn,),
        in_specs=[pl.BlockSpec((R, D), lambda j: (0, 0)),
                  pl.BlockSpec((D, tn), lambda j: (0, j)),
                  pl.BlockSpec((1, tn), lambda j: (0, j))],
        out_specs=pl.BlockSpec((R, tn), lambda j: (0, j)),
        out_shape=jax.ShapeDtypeStruct((R, N), F32),
        compiler_params=_cparams(("arbitrary",)),
        name="adaln",
    )(c_all, w, b.reshape(1, N))


def _norm_mod_kernel(x_ref, mod_ref, g_ref, o_ref, *, which):
    x = x_ref[0]
    ms = jnp.mean(x * x, axis=-1, keepdims=True)
    y = x * lax.rsqrt(ms + EPS) * g_ref[...]
    sh = mod_ref[0, 3 * which:3 * which + 1, :]
    sc = mod_ref[0, 3 * which + 1:3 * which + 2, :]
    o_ref[0] = (y * (1.0 + sc) + sh).astype(BF16)


def _norm_mod(xa, mod3, g, which, n_ctx_tiles, ctx_row):
    B, Tt, D = xa.shape
    tm = ROW_TILE
    return pl.pallas_call(
        functools.partial(_norm_mod_kernel, which=which),
        grid=(B, Tt // tm),
        in_specs=[pl.BlockSpec((1, tm, D), lambda b, t: (b, t, 0)),
                  pl.BlockSpec((1, N_MOD, D), lambda b, t: (jnp.where(t < n_ctx_tiles, ctx_row, b), 0, 0)),
                  pl.BlockSpec((1, D), lambda b, t: (0, 0))],
        out_specs=pl.BlockSpec((1, tm, D), lambda b, t: (b, t, 0)),
        out_shape=jax.ShapeDtypeStruct((B, Tt, D), BF16),
        compiler_params=_cparams(("parallel", "parallel")),
        name="norm_mod",
    )(xa, mod3, g.reshape(1, D))


def _matmul_kernel(h_ref, w_ref, o_ref):
    o_ref[...] = jnp.dot(h_ref[...], w_ref[...], preferred_element_type=F32)


def _pick(n, cands):
    for c in cands:
        if n % c == 0:
            return c
    return n


def _matmul(h2, w, name):
    R, D = h2.shape
    N = w.shape[1]
    tm = _pick(R, (1024, 512, 256))
    tn = _pick(N, (1536, 1280, 1024, 512, 128))
    return pl.pallas_call(
        _matmul_kernel,
        grid=(N // tn, R // tm),
        in_specs=[pl.BlockSpec((tm, D), lambda j, i: (i, 0)),
                  pl.BlockSpec((D, tn), lambda j, i: (0, j))],
        out_specs=pl.BlockSpec((tm, tn), lambda j, i: (i, j)),
        out_shape=jax.ShapeDtypeStruct((R, N), F32),
        compiler_params=_cparams(("parallel", "parallel")),
        name=name,
    )(h2, w)


def _seg_mean(x2, s_ref):
    hi = x2.astype(BF16)
    lo = (x2 - hi.astype(F32)).astype(BF16)
    s = s_ref[...]
    return jnp.dot(hi, s, preferred_element_type=F32) + jnp.dot(lo, s, preferred_element_type=F32)


def _rope(y, cos, sin_signed, first_half):
    rot = jnp.where(first_half, pltpu.roll(y, 128 - 16, 1), pltpu.roll(y, 16, 1))
    return y * cos + rot * sin_signed


def _qkv_prep_kernel(q_ref, k_ref, v_ref, cos_ref, sin_ref, gq_ref, gk_ref, s_ref, qo_ref, ko_ref, vo_ref,
                     *, q_scale):
    cos = cos_ref[...]
    sin = sin_ref[...]
    tm = cos.shape[0]
    lane = lax.broadcasted_iota(jnp.int32, (tm, 128), 1)
    first_half = (lane % 32) < 16
    comp0 = lane < DA_QK_DIM

    q = q_ref[0]
    qn = q * lax.rsqrt(_seg_mean(q * q, s_ref) + EPS) * gq_ref[...]
    k = k_ref[0]
    kn = k * lax.rsqrt(_seg_mean(k * k, s_ref) + EPS) * gk_ref[...]
    for h in range(DA_HEADS):
        qh = _rope(qn[:, h * 128:(h + 1) * 128], cos, sin, first_half) * q_scale
        qo_ref[0, :, h * 256:h * 256 + 128] = jnp.where(comp0, qh, 0.0).astype(BF16)
        qo_ref[0, :, h * 256 + 128:(h + 1) * 256] = jnp.where(comp0, 0.0, qh).astype(BF16)
        kh = _rope(kn[:, h * 128:(h + 1) * 128], cos, sin, first_half)
        ko_ref[0, :, h * 128:(h + 1) * 128] = kh.astype(BF16)
    vo_ref[0] = v_ref[0].astype(BF16)


def _qkv_prep(P, cols, cos_t, sin_t, gq, gk):
    B, Tt, _ = P.shape
    tm = ROW_TILE
    W = BRANCH_WIDTH
    seg = (jnp.arange(W)[:, None] // DA_QK_DIM == jnp.arange(W)[None, :] // DA_QK_DIM)
    smat = (seg.astype(F32) / DA_QK_DIM).astype(BF16)
    gq_t = jnp.tile(gq, W // DA_QK_DIM).reshape(1, W)
    gk_t = jnp.tile(gk, W // DA_QK_DIM).reshape(1, W)
    qb, kb, vb = cols["q"] // W, cols["k"] // W, cols["v"] // W
    q_scale = (DA_QK_DIM ** -0.5) * math.log2(math.e)
    return pl.pallas_call(
        functools.partial(_qkv_prep_kernel, q_scale=q_scale),
        grid=(B, Tt // tm),
        in_specs=[pl.BlockSpec((1, tm, W), lambda b, t: (b, t, qb)),
                  pl.BlockSpec((1, tm, W), lambda b, t: (b, t, kb)),
                  pl.BlockSpec((1, tm, W), lambda b, t: (b, t, vb)),
                  pl.BlockSpec((tm, 128), lambda b, t: (t, 0)),
                  pl.BlockSpec((tm, 128), lambda b, t: (t, 0)),
                  pl.BlockSpec((1, W), lambda b, t: (0, 0)),
                  pl.BlockSpec((1, W), lambda b, t: (0, 0)),
                  pl.BlockSpec((W, W), lambda b, t: (0, 0))],
        out_specs=[pl.BlockSpec((1, tm, 2 * W), lambda b, t: (b, t, 0)),
                   pl.BlockSpec((1, tm, W), lambda b, t: (b, t, 0)),
                   pl.BlockSpec((1, tm, W), lambda b, t: (b, t, 0))],
        out_shape=[jax.ShapeDtypeStruct((B, Tt, 2 * W), BF16),
                   jax.ShapeDtypeStruct((B, Tt, W), BF16),
                   jax.ShapeDtypeStruct((B, Tt, W), BF16)],
        compiler_params=_cparams(("parallel", "parallel")),
        name="qkv_prep",
    )(P, P, P, cos_t, sin_t, gq_t, gk_t, smat)


def _flash_kernel(q_ref, k_ref, v_ref, lam_ref, subg_ref, o_ref, m_s, l_s, acc_s, *, tq, nk, lam_init):
    j = pl.program_id(3)

    @pl.when(j == 0)
    def _():
        m_s[...] = jnp.full(m_s.shape, -jnp.inf, F32)
        l_s[...] = jnp.zeros(l_s.shape, F32)
        acc_s[...] = jnp.zeros(acc_s.shape, F32)

    q = q_ref[0]
    q2 = jnp.concatenate([q[:, :128], q[:, 128:]], axis=0)
    s = lax.dot_general(q2, k_ref[0], (((1,), (1,)), ((), ())), preferred_element_type=F32)
    m_old = m_s[...]
    m_new = jnp.maximum(m_old, jnp.max(s, axis=-1, keepdims=True))
    alpha = jnp.exp2(m_old - m_new)
    p = jnp.exp2(s - m_new[:, :1])
    l_s[...] = alpha * l_s[...] + jnp.sum(p, axis=-1, keepdims=True)
    acc_s[...] = alpha * acc_s[...] + jnp.dot(p.astype(BF16), v_ref[0], preferred_element_type=F32)
    m_s[...] = m_new

    @pl.when(j == nk - 1)
    def _():
        o = acc_s[...] / l_s[...]
        lv = lam_ref[...]
        s01 = jnp.sum(lv[0:1] * lv[1:2], axis=-1, keepdims=True)
        s23 = jnp.sum(lv[2:3] * lv[3:4], axis=-1, keepdims=True)
        lam = jnp.exp(s01) - jnp.exp(s23) + lam_init
        d = o[:tq] - lam * o[tq:]
        ms = jnp.mean(d * d, axis=-1, keepdims=True)
        o_ref[0] = d * lax.rsqrt(ms + EPS) * subg_ref[...] * (1.0 - lam_init)


def _flash(Qs, Ks, Vs, lam_vec, sub_g, lam_init, q_row0, n_q_rows, n_k_rows, tq, tk):
    B = Qs.shape[0]
    nq, nk = n_q_rows // tq, n_k_rows // tk
    q0 = q_row0 // tq
    return pl.pallas_call(
        functools.partial(_flash_kernel, tq=tq, nk=nk, lam_init=lam_init),
        grid=(B, DA_HEADS, nq, nk),
        in_specs=[pl.BlockSpec((1, tq, 256), lambda b, h, i, j: (b, q0 + i, h)),
                  pl.BlockSpec((1, tk, 128), lambda b, h, i, j: (b, j, h)),
                  pl.BlockSpec((1, tk, 128), lambda b, h, i, j: (b, j, h)),
                  pl.BlockSpec((4, DA_QK_DIM), lambda b, h, i, j: (0, 0)),
                  pl.BlockSpec((1, DA_V_DIM), lambda b, h, i, j: (0, 0))],
        out_specs=pl.BlockSpec((1, tq, 128), lambda b, h, i, j: (b, i, h)),
        out_shape=jax.ShapeDtypeStruct((B, n_q_rows, BRANCH_WIDTH), F32),
        scratch_shapes=[pltpu.VMEM((2 * tq, 128), F32), pltpu.VMEM((2 * tq, 128), F32),
                        pltpu.VMEM((2 * tq, 128), F32)],
        compiler_params=_cparams(("parallel", "parallel", "parallel", "arbitrary")),
        name="diff_attn",
    )(Qs, Ks, Vs, lam_vec, sub_g.reshape(1, DA_V_DIM))


def _scan_tile(j, n_ctx, nt, reverse):
    if not reverse:
        return j
    return jnp.where(j < n_ctx, n_ctx - 1 - j, nt - 1 - (j - n_ctx))


def _dwconv4(x, prev8, next8, w_ref, is_first, is_last):
    tm = x.shape[0]
    row = lax.broadcasted_iota(jnp.int32, x.shape, 0)
    pf = jnp.where(is_first, 0.0, 1.0)
    nf = jnp.where(is_last, 0.0, 1.0)
    prow = prev8[HALO_ROWS - 1:HALO_ROWS, :] * pf
    n0 = next8[0:1, :] * nf
    n1 = next8[1:2, :] * nf
    xm1 = jnp.where(row == 0, prow, pltpu.roll(x, 1, 0))
    xp1 = jnp.where(row == tm - 1, n0, pltpu.roll(x, tm - 1, 0))
    xp2 = jnp.where(row == tm - 2, n0, jnp.where(row == tm - 1, n1, pltpu.roll(x, tm - 2, 0)))
    return w_ref[0:1, :] * xm1 + w_ref[1:2, :] * x + w_ref[2:3, :] * xp1 + w_ref[3:4, :] * xp2


def _halo_specs(tm, width, col_blk, n_ctx, nt, reverse):
    r8 = tm // HALO_ROWS
    last8 = nt * r8 - 1

    def cur(b, j):
        return (b, _scan_tile(j, n_ctx, nt, reverse), col_blk)

    def prev(b, j):
        return (b, jnp.maximum(_scan_tile(j, n_ctx, nt, reverse) * r8 - 1, 0), col_blk)

    def nxt(b, j):
        return (b, jnp.minimum((_scan_tile(j, n_ctx, nt, reverse) + 1) * r8, last8), col_blk)

    return [pl.BlockSpec((1, tm, width), cur),
            pl.BlockSpec((1, HALO_ROWS, width), prev),
            pl.BlockSpec((1, HALO_ROWS, width), nxt)]


def _linear_scan_tile(a, b, reverse):
    tm = a.shape[0]
    row = lax.broadcasted_iota(jnp.int32, a.shape, 0)
    s = 1
    while s < tm:
        if not reverse:
            keep = row >= s
            ap = jnp.where(keep, pltpu.roll(a, s, 0), 1.0)
            bp = jnp.where(keep, pltpu.roll(b, s, 0), 0.0)
        else:
            keep = row < tm - s
            ap = jnp.where(keep, pltpu.roll(a, tm - s, 0), 1.0)
            bp = jnp.where(keep, pltpu.roll(b, tm - s, 0), 0.0)
        b = a * bp + b
        a = a * ap
        s *= 2
    return a, b


def _lru_kernel(*refs, n_ctx, nt, reverse, final):
    if final:
        (x_ref, xp_ref, xn_ref, y_ref, hf_ref, cw_ref, cb_ref, wg_ref, gb_ref, lam_ref, o_ref, carry) = refs
    else:
        (x_ref, xp_ref, xn_ref, cw_ref, cb_ref, wg_ref, gb_ref, lam_ref, o_ref, carry) = refs
    j = pl.program_id(1)
    t = _scan_tile(j, n_ctx, nt, reverse)
    is_first = jnp.logical_or(t == 0, t == n_ctx)
    is_last = jnp.logical_or(t == n_ctx - 1, t == nt - 1)

    @pl.when(j == 0)
    def _():
        carry[...] = jnp.zeros(carry.shape, F32)

    x = x_ref[0]
    tm, W = x.shape
    xc = _dwconv4(x, xp_ref[0], xn_ref[0], cw_ref, is_first, is_last) + cb_ref[...]
    g = jnp.dot(xc.astype(BF16), wg_ref[...], preferred_element_type=F32) + gb_ref[...]
    r = jax.nn.sigmoid(g[:, :W])
    i = jax.nn.sigmoid(g[:, W:])
    log_a = (-LRU_C) * r * _softplus(-lam_ref[...])
    a = jnp.exp(log_a)
    th = jnp.tanh(log_a)
    bb = jnp.sqrt(-2.0 * th / (1.0 - th)) * (i * xc)
    a_c, b_c = _linear_scan_tile(a, bb, reverse)
    h = b_c + a_c * carry[0:1, :]
    carry[0:1, :] = h[0:1, :] if reverse else h[tm - 1:tm, :]
    if final:
        o_ref[0] = jax.nn.gelu(y_ref[0]) * (hf_ref[0] + h)
    else:
        o_ref[0] = h


def _lru_pass(P, cols, conv_w, conv_b, wg, gb, lam, n_ctx, reverse, hf=None):
    B, Tt, _ = P.shape
    tm = ROW_TILE
    W = BRANCH_WIDTH
    nt = Tt // tm
    final = hf is not None

    def cur(b, j):
        return (b, _scan_tile(j, n_ctx, nt, reverse), 0)

    in_specs = _halo_specs(tm, W, cols["lx"] // W, n_ctx, nt, reverse)
    args = [P, P, P]
    if final:
        yb = cols["ly"] // W
        in_specs += [pl.BlockSpec((1, tm, W), lambda b, j: (b, _scan_tile(j, n_ctx, nt, reverse), yb)),
                     pl.BlockSpec((1, tm, W), cur)]
        args += [P, hf]
    in_specs += [pl.BlockSpec((4, W), lambda b, j: (0, 0)),
                 pl.BlockSpec((1, W), lambda b, j: (0, 0)),
                 pl.BlockSpec((W, 2 * W), lambda b, j: (0, 0)),
                 pl.BlockSpec((1, 2 * W), lambda b, j: (0, 0)),
                 pl.BlockSpec((1, W), lambda b, j: (0, 0))]
    args += [conv_w, conv_b.reshape(1, W), wg, gb.reshape(1, 2 * W), lam.reshape(1, W)]
    return pl.pallas_call(
        functools.partial(_lru_kernel, n_ctx=n_ctx, nt=nt, reverse=reverse, final=final),
        grid=(B, nt),
        in_specs=in_specs,
        out_specs=pl.BlockSpec((1, tm, W), cur),
        out_shape=jax.ShapeDtypeStruct((B, Tt, W), F32),
        scratch_shapes=[pltpu.VMEM((HALO_ROWS, W), F32)],
        compiler_params=_cparams(("parallel", "arbitrary")),
        name="rglru_bwd" if reverse else "rglru_fwd",
    )(*args)


def _lru_gate_weights(gate_w_d):
    mats = []
    for g in range(2):
        mats.append(jax.scipy.linalg.block_diag(*[gate_w_d[g, n] for n in range(LRU_BLOCKS)]))
    return jnp.concatenate(mats, axis=1).astype(BF16)


def _split_bf16(x):
    hi = x.astype(BF16)
    lo = (x - hi.astype(F32)).astype(BF16)
    return hi, lo


def _dot3(a_hi, a_lo, b_hi, b_lo):
    return (jnp.dot(a_hi, b_hi, preferred_element_type=F32)
            + (jnp.dot(a_hi, b_lo, preferred_element_type=F32) + jnp.dot(a_lo, b_hi, preferred_element_type=F32)))


def _gdn_kernel(*refs, n_ctx, nt, reverse, final, d):
    if final:
        (x_ref, xp_ref, xn_ref, sm_ref, z_ref, of_ref, cw_ref, rate_ref, dtb_ref, ng_ref, o_ref, S_s, qkv_s) = refs
    else:
        (x_ref, xp_ref, xn_ref, sm_ref, cw_ref, rate_ref, dtb_ref, o_ref, S_s, qkv_s) = refs
    j = pl.program_id(1)
    t = _scan_tile(j, n_ctx, nt, reverse)
    is_first = jnp.logical_or(t == 0, t == n_ctx)
    is_last = jnp.logical_or(t == n_ctx - 1, t == nt - 1)

    @pl.when(j == 0)
    def _():
        S_s[...] = jnp.zeros(S_s.shape, F32)

    x = x_ref[0]
    tm = x.shape[0]
    C = GDN_CHUNK
    nC = tm // C
    HK = GDN_HEADS * GDN_DK
    qkv = _dwconv4(x, xp_ref[0], xn_ref[0], cw_ref, is_first, is_last)
    qkv_s[...] = qkv * jax.nn.sigmoid(qkv)

    sm = sm_ref[0]
    beta_all = jax.nn.sigmoid(sm)
    g_all = -jnp.exp(rate_ref[...]) * _softplus(sm + dtb_ref[...])

    ii = lax.broadcasted_iota(jnp.int32, (tm, tm), 0)
    jj = lax.broadcasted_iota(jnp.int32, (tm, tm), 1)
    same = (ii // C) == (jj // C)
    if not reverse:
        incl = jnp.logical_and(same, ii >= jj)
        strict = jnp.logical_and(same, ii > jj)
    else:
        incl = jnp.logical_and(same, ii <= jj)
        strict = jnp.logical_and(same, ii < jj)
    cum_mat = jnp.concatenate([incl.astype(F32), same.astype(F32)], axis=0).astype(BF16)
    eye = (ii == jj).astype(F32)

    for h in range(GDN_HEADS):
        q = qkv_s[:, h * GDN_DK:(h + 1) * GDN_DK]
        k = qkv_s[:, HK + h * GDN_DK:HK + (h + 1) * GDN_DK]
        v = qkv_s[:, 2 * HK + h * GDN_DV:2 * HK + (h + 1) * GDN_DV]
        qn = q * lax.rsqrt(jnp.sum(q * q, axis=-1, keepdims=True) + EPS) * (GDN_DK ** -0.5)
        kn = k * lax.rsqrt(jnp.sum(k * k, axis=-1, keepdims=True) + EPS)
        cb = d * GDN_HEADS + h
        cg = N_DIR * GDN_HEADS + d * GDN_HEADS + h
        beta_b = jnp.broadcast_to(beta_all[:, cb:cb + 1], (tm, 128))
        g_b = jnp.broadcast_to(g_all[:, cg:cg + 1], (tm, 128))
        g1 = g_b.astype(BF16)
        g2 = (g_b - g1.astype(F32)).astype(BF16)
        g3 = (g_b - g1.astype(F32) - g2.astype(F32)).astype(BF16)
        cums = (jnp.dot(cum_mat, g1, preferred_element_type=F32)
                + (jnp.dot(cum_mat, g2, preferred_element_type=F32)
                   + jnp.dot(cum_mat, g3, preferred_element_type=F32)))
        gc_b = cums[:tm]
        gtot_b = cums[tm:]
        eg = jnp.exp(gc_b)
        kb = kn * beta_b
        rhs = jnp.concatenate([v * beta_b, kb * eg], axis=1)
        q_dec = qn * eg
        k_dec = kn * jnp.exp(gtot_b - gc_b)
        g_end = jnp.exp(gtot_b)

        gc_cols = jnp.concatenate([gc_b] * (tm // 128), axis=1)
        gc_rows = gc_b.T[0:1, :]
        decay = jnp.exp(jnp.where(incl, gc_cols - gc_rows, NEG_BIG))
        kq = lax.dot_general(jnp.concatenate([kb, qn], axis=0).astype(BF16), kn.astype(BF16),
                             (((1,), (1,)), ((), ())), preferred_element_type=F32)
        m_k = jnp.where(strict, -(kq[:tm] * decay), 0.0)
        attn = kq[tm:] * decay

        t_inv = eye + m_k
        m_hi, m_lo = _split_bf16(m_k)
        p_k = _dot3(m_hi, m_lo, m_hi, m_lo)
        for _ in range(4):
            p_hi, p_lo = _split_bf16(p_k)
            t_hi, t_lo = _split_bf16(t_inv)
            r2 = _dot3(jnp.concatenate([t_hi, p_hi], axis=0), jnp.concatenate([t_lo, p_lo], axis=0), p_hi, p_lo)
            t_inv = t_inv + r2[:tm]
            p_k = r2[tm:]
        p_hi, p_lo = _split_bf16(p_k)
        t_hi, t_lo = _split_bf16(t_inv)
        t_inv = t_inv + _dot3(t_hi, t_lo, p_hi, p_lo)
        sol = jnp.dot(t_inv.astype(BF16), rhs.astype(BF16), preferred_element_type=F32)
        u = sol[:, :GDN_DV]
        w = sol[:, GDN_DV:]

        S = S_s[h]
        for cc in range(nC):
            c = nC - 1 - cc if reverse else cc
            r0 = c * C
            wq = jnp.concatenate([w[r0:r0 + C], q_dec[r0:r0 + C]], axis=0).astype(BF16)
            ws = jnp.dot(wq, S.astype(BF16), preferred_element_type=F32)
            v_new = u[r0:r0 + C] - ws[:C]
            o_c = ws[C:] + jnp.dot(attn[r0:r0 + C, r0:r0 + C].astype(BF16), v_new.astype(BF16),
                                   preferred_element_type=F32)
            S = S * _col(g_end[r0:r0 + 1, :], GDN_DK) + lax.dot_general(
                k_dec[r0:r0 + C].astype(BF16), v_new.astype(BF16), (((0,), (0,)), ((), ())),
                preferred_element_type=F32)
            if final:
                o_t = of_ref[0, r0:r0 + C, h * GDN_DV:(h + 1) * GDN_DV] + o_c
                ms = jnp.mean(o_t * o_t, axis=-1, keepdims=True)
                z = z_ref[0, r0:r0 + C, h * GDN_DV:(h + 1) * GDN_DV]
                o_ref[0, r0:r0 + C, h * GDN_DV:(h + 1) * GDN_DV] = (
                    o_t * lax.rsqrt(ms + EPS) * ng_ref[...] * (z * jax.nn.sigmoid(z)))
            else:
                o_ref[0, r0:r0 + C, h * GDN_DV:(h + 1) * GDN_DV] = o_c
        S_s[h] = S


def _col(row_vec, n):
    return jnp.broadcast_to(row_vec, (n, row_vec.shape[1]))


def _gdn_pass(P, Psm, cols, conv_w, rate_row, dtb_row, n_ctx, reverse, d, of=None, norm_g=None):
    B, Tt, _ = P.shape
    tm = ROW_TILE
    W = BRANCH_WIDTH
    QW = GDN_HEADS * (2 * GDN_DK + GDN_DV)
    nt = Tt // tm
    final = of is not None

    def cur(b, j):
        return (b, _scan_tile(j, n_ctx, nt, reverse), 0)

    in_specs = _halo_specs(tm, QW, cols["gqkv"] // QW, n_ctx, nt, reverse)
    in_specs += [pl.BlockSpec((1, tm, 128), cur)]
    args = [P, P, P, Psm]
    if final:
        zb = cols["gz"] // W
        in_specs += [pl.BlockSpec((1, tm, W), lambda b, j: (b, _scan_tile(j, n_ctx, nt, reverse), zb)),
                     pl.BlockSpec((1, tm, W), cur)]
        args += [P, of]
    in_specs += [pl.BlockSpec((4, QW), lambda b, j: (0, 0)),
                 pl.BlockSpec((1, 128), lambda b, j: (0, 0)),
                 pl.BlockSpec((1, 128), lambda b, j: (0, 0))]
    args += [conv_w, rate_row, dtb_row]
    if final:
        in_specs += [pl.BlockSpec((1, GDN_DV), lambda b, j: (0, 0))]
        args += [norm_g.reshape(1, GDN_DV)]
    return pl.pallas_call(
        functools.partial(_gdn_kernel, n_ctx=n_ctx, nt=nt, reverse=reverse, final=final, d=d),
        grid=(B, nt),
        in_specs=in_specs,
        out_specs=pl.BlockSpec((1, tm, W), cur),
        out_shape=jax.ShapeDtypeStruct((B, Tt, W), F32),
        scratch_shapes=[pltpu.VMEM((GDN_HEADS, GDN_DK, GDN_DV), F32), pltpu.VMEM((tm, QW), F32)],
        compiler_params=_cparams(("parallel", "arbitrary")),
        name="gdn_bwd" if reverse else "gdn_fwd",
    )(*args)


def _merge_kernel(x_ref, da_ref, lru_ref, gdn_ref, g0_ref, g1_ref, g2_ref, mod_ref, wb_ref, wo_ref, o_ref):
    outs = (da_ref, lru_ref, gdn_ref)
    gates = (g0_ref, g1_ref, g2_ref)
    merged = None
    for i in range(N_BRANCH):
        y = jnp.dot(outs[i][0].astype(BF16), wb_ref[i], preferred_element_type=F32)
        term = jax.nn.sigmoid(gates[i][0]) * y
        merged = term if merged is None else merged + term
    proj = jnp.dot(merged.astype(BF16), wo_ref[...], preferred_element_type=F32)
    o_ref[0] = x_ref[0] + mod_ref[0, 2:3, :] * proj


def _merge(xa, da, lru, gdn, P, mod3, wb, wo, n_ctx_tiles, ctx_row):
    B, Tt, D = xa.shape
    tm = ROW_TILE
    W = BRANCH_WIDTH
    row = lambda b, t: (b, t, 0)
    return pl.pallas_call(
        _merge_kernel,
        grid=(B, Tt // tm),
        in_specs=[pl.BlockSpec((1, tm, D), row),
                  pl.BlockSpec((1, tm, W), row), pl.BlockSpec((1, tm, W), row), pl.BlockSpec((1, tm, W), row),
                  pl.BlockSpec((1, tm, D), lambda b, t: (b, t, 0)),
                  pl.BlockSpec((1, tm, D), lambda b, t: (b, t, 1)),
                  pl.BlockSpec((1, tm, D), lambda b, t: (b, t, 2)),
                  pl.BlockSpec((1, N_MOD, D), lambda b, t: (jnp.where(t < n_ctx_tiles, ctx_row, b), 0, 0)),
                  pl.BlockSpec((N_BRANCH, W, D), lambda b, t: (0, 0, 0)),
                  pl.BlockSpec((D, D), lambda b, t: (0, 0))],
        out_specs=pl.BlockSpec((1, tm, D), row),
        out_shape=jax.ShapeDtypeStruct((B, Tt, D), F32),
        compiler_params=_cparams(("parallel", "parallel")),
        name="merge",
    )(xa, da, lru, gdn, P, P, P, mod3, wb, wo)


def _mlp_kernel(x_ref, mod_ref, g_ref, w1_ref, w2_ref, o_ref, *, ff_chunk):
    x = x_ref[0]
    ms = jnp.mean(x * x, axis=-1, keepdims=True)
    y = x * lax.rsqrt(ms + EPS) * g_ref[...]
    h = (y * (1.0 + mod_ref[0, 4:5, :]) + mod_ref[0, 3:4, :]).astype(BF16)
    n_ff = w1_ref.shape[1]
    acc = None
    for c0 in range(0, n_ff, ff_chunk):
        a = jnp.dot(h, w1_ref[:, c0:c0 + ff_chunk], preferred_element_type=F32)
        a = jnp.square(jnp.maximum(a, 0.0)).astype(BF16)
        part = jnp.dot(a, w2_ref[c0:c0 + ff_chunk, :], preferred_element_type=F32)
        acc = part if acc is None else acc + part
    o_ref[0] = x + mod_ref[0, 5:6, :] * acc


def _mlp(xa, mod3, g, w1, w2, n_ctx_tiles, ctx_row, row0, n_rows):
    B, Tt, D = xa.shape
    tm = ROW_TILE
    F = w1.shape[1]
    t0 = row0 // tm
    return pl.pallas_call(
        functools.partial(_mlp_kernel, ff_chunk=min(F, 1024)),
        grid=(B, n_rows // tm),
        in_specs=[pl.BlockSpec((1, tm, D), lambda b, t: (b, t0 + t, 0)),
                  pl.BlockSpec((1, N_MOD, D), lambda b, t: (jnp.where(t0 + t < n_ctx_tiles, ctx_row, b), 0, 0)),
                  pl.BlockSpec((1, D), lambda b, t: (0, 0)),
                  pl.BlockSpec((D, F), lambda b, t: (0, 0)),
                  pl.BlockSpec((F, D), lambda b, t: (0, 0))],
        out_specs=pl.BlockSpec((1, tm, D), lambda b, t: (b, t, 0)),
        out_shape=jax.ShapeDtypeStruct((B, n_rows, D), F32),
        compiler_params=_cparams(("parallel", "parallel")),
        name="mlp",
    )(xa, mod3, g.reshape(1, D), w1, w2)


def _proj_layout(D):
    W = BRANCH_WIDTH
    QW = GDN_HEADS * (2 * GDN_DK + GDN_DV)
    src = {}
    off = 0
    for name, width in (("q", W), ("k", W), ("v", W), ("lx", W), ("ly", W), ("gqkv", QW), ("gz", W),
                        ("gb", N_DIR * GDN_HEADS), ("ga", N_DIR * GDN_HEADS), ("gate", N_BRANCH * D)):
        src[name] = (off, width)
        off += width
    order = ("gate", "gqkv", "q", "k", "v", "lx", "ly", "gz")
    cols = {}
    idx = []
    o = 0
    for name in order:
        s, w = src[name]
        cols[name] = o
        idx.append(jnp.arange(s, s + w))
        o += w
    small_idx = jnp.concatenate([jnp.arange(src["gb"][0], src["gb"][0] + src["gb"][1]),
                                 jnp.arange(src["ga"][0], src["ga"][0] + src["ga"][1])])
    return cols, jnp.concatenate(idx), small_idx


def _rope_tables(n_ctx_rows, n_lat_rows):
    n_freq = DA_QK_DIM // 4
    inv = ROPE_BASE ** (-jnp.arange(n_freq, dtype=F32) / n_freq)
    tpos = jnp.arange(n_lat_rows, dtype=jnp.int32)
    ang_r = (tpos // GRID_W).astype(F32)[:, None] * inv
    ang_c = (tpos % GRID_W).astype(F32)[:, None] * inv
    ang = jnp.concatenate([ang_r, ang_r, ang_c, ang_c], axis=-1)
    ang = jnp.concatenate([ang, ang], axis=-1)
    sign = jnp.where((jnp.arange(128) % 32) < 16, -1.0, 1.0).astype(F32)
    cos = jnp.concatenate([jnp.ones((n_ctx_rows, 128), F32), jnp.cos(ang)], axis=0)
    sin = jnp.concatenate([jnp.zeros((n_ctx_rows, 128), F32), jnp.sin(ang) * sign], axis=0)
    return cos, sin


def kernel(x, c, ctx, c_ctx, ada_w, ada_b, norm1_g, norm2_g, w_in, da_q_norm_g, da_k_norm_g, da_lambda, da_sub_norm_g, lru_conv_w, lru_conv_b, lru_gate_w, lru_gate_b, lru_lambda, gdn_conv_w, gdn_A_log, gdn_dt_bias, gdn_norm_g, w_branch, w_out, mlp_w1, mlp_w2):
    B, T, D = x.shape
    Tc = ctx.shape[1]
    depth = ada_w.shape[0]
    tm = ROW_TILE
    assert Tc % tm == 0 and T % tm == 0 and B + 1 <= 8
    n_ctx = Tc // tm
    Tt = Tc + T
    ctx_row = B

    cols, main_idx, small_idx = _proj_layout(D)
    cos_t, sin_t = _rope_tables(Tc, T)
    xa = jnp.concatenate([ctx, x], axis=1)
    c_all = jnp.concatenate([c, c_ctx[None, :], jnp.zeros((8 - B - 1, D), F32)], axis=0)

    tk = _pick(Tt, (768, 512, 256))
    tq = 256

    out = None
    for layer in range(depth):
        last = layer == depth - 1
        lam_init = 0.8 - 0.6 * math.exp(-0.3 * layer)
        w_main = w_in[layer][:, main_idx].astype(BF16)
        w_small = jnp.pad(w_in[layer][:, small_idx], ((0, 0), (0, 128 - small_idx.shape[0]))).astype(BF16)

        mod3 = _adaln(c_all, ada_w[layer], ada_b[layer]).reshape(8, N_MOD, D)
        h1 = _norm_mod(xa, mod3, norm1_g[layer], 0, n_ctx, ctx_row).reshape(B * Tt, D)
        P = _matmul(h1, w_main, "in_proj").reshape(B, Tt, -1)
        Psm = _matmul(h1, w_small, "in_proj_small").reshape(B, Tt, 128)

        Qs, Ks, Vs = _qkv_prep(P, cols, cos_t, sin_t, da_q_norm_g[layer], da_k_norm_g[layer])
        da_l = _flash(Qs, Ks, Vs, da_lambda[layer], da_sub_norm_g[layer], lam_init, Tc, T, Tt, tq, tk)
        if last:
            da = jnp.pad(da_l, ((0, 0), (Tc, 0), (0, 0)))
        else:
            da_c = _flash(Qs, Ks, Vs, da_lambda[layer], da_sub_norm_g[layer], lam_init, 0, Tc, Tc, tm, tm)
            da = jnp.concatenate([da_c, da_l], axis=1)

        hf = _lru_pass(P, cols, lru_conv_w[layer], lru_conv_b[layer], _lru_gate_weights(lru_gate_w[layer, 0]),
                       lru_gate_b[layer, 0].reshape(-1), lru_lambda[layer, 0], n_ctx, False)
        lru = _lru_pass(P, cols, lru_conv_w[layer], lru_conv_b[layer], _lru_gate_weights(lru_gate_w[layer, 1]),
                        lru_gate_b[layer, 1].reshape(-1), lru_lambda[layer, 1], n_ctx, True, hf=hf)

        rate_row = jnp.zeros((1, 128), F32).at[0, N_DIR * GDN_HEADS:2 * N_DIR * GDN_HEADS].set(
            gdn_A_log[layer].astype(F32).reshape(-1))
        dtb_row = jnp.zeros((1, 128), F32).at[0, N_DIR * GDN_HEADS:2 * N_DIR * GDN_HEADS].set(
            gdn_dt_bias[layer].astype(F32).reshape(-1))
        of = _gdn_pass(P, Psm, cols, gdn_conv_w[layer], rate_row, dtb_row, n_ctx, False, 0)
        gdn = _gdn_pass(P, Psm, cols, gdn_conv_w[layer], rate_row, dtb_row, n_ctx, True, 1, of=of,
                        norm_g=gdn_norm_g[layer])

        xa = _merge(xa, da, lru, gdn, P, mod3, w_branch[layer].astype(BF16), w_out[layer].astype(BF16),
                    n_ctx, ctx_row)
        if last:
            out = _mlp(xa, mod3, norm2_g[layer], mlp_w1[layer].astype(BF16), mlp_w2[layer].astype(BF16),
                       n_ctx, ctx_row, Tc, T)
        else:
            xa = _mlp(xa, mod3, norm2_g[layer], mlp_w1[layer].astype(BF16), mlp_w2[layer].astype(BF16),
                      n_ctx, ctx_row, 0, Tt)
    return out
```

```python
import functools
import math

import jax
import jax.numpy as jnp
from jax import lax
from jax.experimental import pallas as pl
from jax.experimental.pallas import tpu as pltpu

F32 = jnp.float32
BF16 = jnp.bfloat16
HIGHEST = lax.Precision.HIGHEST

N_MOD = 6
BRANCH_WIDTH = 512
N_BRANCH = 3
DA_HEADS = 4
DA_QK_DIM = 64
DA_V_DIM = 128
ROPE_BASE = 10000.0
LRU_BLOCKS = 8
LRU_C = 8.0
GDN_HEADS = 4
GDN_DK = 128
GDN_DV = 128
GDN_CHUNK = 64
N_DIR = 2
GRID_W = 64
EPS = 1e-6
NEG_BIG = -1e30

ROW_TILE = 256
HALO_ROWS = 8
VT_PAD = 16
VMEM_LIMIT = 48 * 1024 * 1024

COL_GATE = 0


def _cparams(sem):
    return pltpu.CompilerParams(dimension_semantics=sem, vmem_limit_bytes=VMEM_LIMIT)


def _softplus(z):
    return jnp.maximum(z, 0.0) + jnp.log1p(jnp.exp(-jnp.abs(z)))


def _adaln_kernel(c_ref, w_ref, b_ref, o_ref):
    c = c_ref[...]
    a = c * jax.nn.sigmoid(c)
    o_ref[...] = jnp.dot(a, w_ref[...], preferred_element_type=F32, precision=HIGHEST) + b_ref[...]


def _adaln(c_all, w, b):
    R, D = c_all.shape
    N = w.shape[1]
    tn = 1536 if N % 1536 == 0 else N
    return pl.pallas_call(
        _adaln_kernel,
        grid=(N // tn,),
        in_specs=[pl.BlockSpec((R, D), lambda j: (0, 0)),
                  pl.BlockSpec((D, tn), lambda j: (0, j)),
                  pl.BlockSpec((1, tn), lambda j: (0, j))],
        out_specs=pl.BlockSpec((R, tn), lambda j: (0, j)),
        out_shape=jax.ShapeDtypeStruct((R, N), F32),
        compiler_params=_cparams(("arbitrary",)),
        name="adaln",
    )(c_all, w, b.reshape(1, N))


def _norm_mod_kernel(x_ref, mod_ref, g_ref, o_ref, *, which):
    x = x_ref[0]
    ms = jnp.mean(x * x, axis=-1, keepdims=True)
    y = x * lax.rsqrt(ms + EPS) * g_ref[...]
    sh = mod_ref[0, 3 * which:3 * which + 1, :]
    sc = mod_ref[0, 3 * which + 1:3 * which + 2, :]
    o_ref[0] = (y * (1.0 + sc) + sh).astype(BF16)


def _norm_mod(xa, mod3, g, which, n_lat_tiles, ctx_row):
    B, Tt, D = xa.shape
    tm = ROW_TILE
    return pl.pallas_call(
        functools.partial(_norm_mod_kernel, which=which),
        grid=(B, Tt // tm),
        in_specs=[pl.BlockSpec((1, tm, D), lambda b, t: (b, t, 0)),
                  pl.BlockSpec((1, N_MOD, D), lambda b, t: (jnp.where(t >= n_lat_tiles, ctx_row, b), 0, 0)),
                  pl.BlockSpec((1, D), lambda b, t: (0, 0))],
        out_specs=pl.BlockSpec((1, tm, D), lambda b, t: (b, t, 0)),
        out_shape=jax.ShapeDtypeStruct((B, Tt, D), BF16),
        compiler_params=_cparams(("parallel", "parallel")),
        name="norm_mod",
    )(xa, mod3, g.reshape(1, D))


def _matmul_kernel(h_ref, w_ref, o_ref):
    o_ref[...] = jnp.dot(h_ref[...], w_ref[...], preferred_element_type=F32)


def _pick(n, cands):
    for c in cands:
        if n % c == 0:
            return c
    return n


def _matmul(h2, w, name):
    R, D = h2.shape
    N = w.shape[1]
    tm = _pick(R, (1024, 512, 256))
    tn = _pick(N, (1536, 1280, 1024, 512, 128))
    return pl.pallas_call(
        _matmul_kernel,
        grid=(N // tn, R // tm),
        in_specs=[pl.BlockSpec((tm, D), lambda j, i: (i, 0)),
                  pl.BlockSpec((D, tn), lambda j, i: (0, j))],
        out_specs=pl.BlockSpec((tm, tn), lambda j, i: (i, j)),
        out_shape=jax.ShapeDtypeStruct((R, N), F32),
        compiler_params=_cparams(("parallel", "parallel")),
        name=name,
    )(h2, w)


def _seg_mean(x2, s_ref):
    hi = x2.astype(BF16)
    lo = (x2 - hi.astype(F32)).astype(BF16)
    s = s_ref[...]
    return jnp.dot(hi, s, preferred_element_type=F32) + jnp.dot(lo, s, preferred_element_type=F32)


def _rope(y, cos, sin_signed, first_half):
    rot = jnp.where(first_half, pltpu.roll(y, 128 - 16, 1), pltpu.roll(y, 16, 1))
    return y * cos + rot * sin_signed


def _qkv_prep_kernel(q_ref, k_ref, v_ref, cos_ref, sin_ref, gq_ref, gk_ref, s_ref, qo_ref, ko_ref, vo_ref,
                     *, q_scale):
    cos = cos_ref[...]
    sin = sin_ref[...]
    tm = cos.shape[0]
    lane = lax.broadcasted_iota(jnp.int32, (tm, 128), 1)
    first_half = (lane % 32) < 16
    comp0 = lane < DA_QK_DIM

    q = q_ref[0]
    qn = q * lax.rsqrt(_seg_mean(q * q, s_ref) + EPS) * gq_ref[...]
    k = k_ref[0]
    kn = k * lax.rsqrt(_seg_mean(k * k, s_ref) + EPS) * gk_ref[...]
    v = v_ref[0]
    ones_row = jnp.where(lax.broadcasted_iota(jnp.int32, (VT_PAD, tm), 0) == 0, 1.0, 0.0).astype(BF16)
    for h in range(DA_HEADS):
        qh = _rope(qn[:, h * 128:(h + 1) * 128], cos, sin, first_half) * q_scale
        qo_ref[0, h * 128:(h + 1) * 128, :] = qh.T.astype(BF16)
        kh = _rope(kn[:, h * 128:(h + 1) * 128], cos, sin, first_half)
        ko_ref[0, :, h * 256:h * 256 + 128] = jnp.where(comp0, kh, 0.0).astype(BF16)
        ko_ref[0, :, h * 256 + 128:(h + 1) * 256] = jnp.where(comp0, 0.0, kh).astype(BF16)
        vo_ref[0, h, 0, 0:DA_V_DIM, :] = v[:, h * DA_V_DIM:(h + 1) * DA_V_DIM].T.astype(BF16)
        vo_ref[0, h, 0, DA_V_DIM:DA_V_DIM + VT_PAD, :] = ones_row


def _qkv_prep(P, cols, cos_t, sin_t, gq, gk):
    B, Tt, _ = P.shape
    tm = ROW_TILE
    W = BRANCH_WIDTH
    seg = (jnp.arange(W)[:, None] // DA_QK_DIM == jnp.arange(W)[None, :] // DA_QK_DIM)
    smat = (seg.astype(F32) / DA_QK_DIM).astype(BF16)
    gq_t = jnp.tile(gq, W // DA_QK_DIM).reshape(1, W)
    gk_t = jnp.tile(gk, W // DA_QK_DIM).reshape(1, W)
    qb, kb, vb = cols["q"] // W, cols["k"] // W, cols["v"] // W
    q_scale = (DA_QK_DIM ** -0.5) * math.log2(math.e)
    return pl.pallas_call(
        functools.partial(_qkv_prep_kernel, q_scale=q_scale),
        grid=(B, Tt // tm),
        in_specs=[pl.BlockSpec((1, tm, W), lambda b, t: (b, t, qb)),
                  pl.BlockSpec((1, tm, W), lambda b, t: (b, t, kb)),
                  pl.BlockSpec((1, tm, W), lambda b, t: (b, t, vb)),
                  pl.BlockSpec((tm, 128), lambda b, t: (t, 0)),
                  pl.BlockSpec((tm, 128), lambda b, t: (t, 0)),
                  pl.BlockSpec((1, W), lambda b, t: (0, 0)),
                  pl.BlockSpec((1, W), lambda b, t: (0, 0)),
                  pl.BlockSpec((W, W), lambda b, t: (0, 0))],
        out_specs=[pl.BlockSpec((1, W, tm), lambda b, t: (b, 0, t)),
                   pl.BlockSpec((1, tm, 2 * W), lambda b, t: (b, t, 0)),
                   pl.BlockSpec((1, DA_HEADS, 1, DA_V_DIM + VT_PAD, tm), lambda b, t: (b, 0, t, 0, 0))],
        out_shape=[jax.ShapeDtypeStruct((B, W, Tt), BF16),
                   jax.ShapeDtypeStruct((B, Tt, 2 * W), BF16),
                   jax.ShapeDtypeStruct((B, DA_HEADS, Tt // tm, DA_V_DIM + VT_PAD, tm), BF16)],
        compiler_params=_cparams(("parallel", "parallel")),
        name="qkv_prep",
    )(P, P, P, cos_t, sin_t, gq_t, gk_t, smat)


def _flash_kernel(qt_ref, k_ref, vt_ref, lam_ref, subg_ref, o_ref, acc_s,
                  *, tq, sub, n_sub, nk, chunk0, lam_init, unroll):
    qt = qt_ref[0]
    acc_s[...] = jnp.zeros(acc_s.shape, F32)
    tk = sub * n_sub

    def body(j, ms):
        c0 = chunk0 + j * n_sub
        r0 = pl.multiple_of(c0 * sub, sub)
        kk = k_ref[0, pl.ds(r0, tk), :]
        k2 = jnp.concatenate([kk[:, :128], kk[:, 128:]], axis=0)
        st = jnp.dot(k2, qt, preferred_element_type=F32)
        vt = jnp.concatenate([vt_ref[0, 0, c0 + i] for i in range(n_sub)], axis=1)
        out = []
        for c in range(2):
            s_c = st[c * tk:(c + 1) * tk]
            m_old = ms[c]
            m_new = jnp.maximum(m_old, jnp.max(s_c, axis=0, keepdims=True))
            alpha = jnp.exp2(m_old - m_new)
            p = jnp.exp2(s_c - m_new).astype(BF16)
            acc_s[c] = alpha * acc_s[c] + jnp.dot(vt, p, preferred_element_type=F32)
            out.append(m_new)
        return tuple(out)

    m_init = jnp.full((1, tq), -jnp.inf, F32)
    lax.fori_loop(0, nk, body, (m_init, m_init), unroll=unroll)

    a0 = acc_s[0]
    a1 = acc_s[1]
    o0 = a0[:DA_V_DIM] / a0[DA_V_DIM:DA_V_DIM + 1]
    o1 = a1[:DA_V_DIM] / a1[DA_V_DIM:DA_V_DIM + 1]
    lv = lam_ref[...]
    s01 = jnp.sum(lv[0:1] * lv[1:2], axis=-1, keepdims=True)
    s23 = jnp.sum(lv[2:3] * lv[3:4], axis=-1, keepdims=True)
    lam = jnp.exp(s01) - jnp.exp(s23) + lam_init
    d = o0 - lam * o1
    ms_ = jnp.mean(d * d, axis=0, keepdims=True)
    o_ref[0] = (d * lax.rsqrt(ms_ + EPS)).T * subg_ref[...] * (1.0 - lam_init)


def _flash(QT, Ks, VT, lam_vec, sub_g, lam_init, q_row0, n_q_rows, k_row0, n_k_rows, tq, n_sub):
    B, _, Tt = QT.shape
    sub = VT.shape[-1]
    tk = sub * n_sub
    nq, nk = n_q_rows // tq, n_k_rows // tk
    q0 = q_row0 // tq
    vrows = VT.shape[3]
    return pl.pallas_call(
        functools.partial(_flash_kernel, tq=tq, sub=sub, n_sub=n_sub, nk=nk, chunk0=k_row0 // sub,
                          lam_init=lam_init, unroll=True),
        grid=(B, DA_HEADS, nq),
        in_specs=[pl.BlockSpec((1, 128, tq), lambda b, h, i: (b, h, q0 + i)),
                  pl.BlockSpec((1, Tt, 256), lambda b, h, i: (b, 0, h)),
                  pl.BlockSpec((1, 1, VT.shape[2], vrows, sub), lambda b, h, i: (b, h, 0, 0, 0)),
                  pl.BlockSpec((4, DA_QK_DIM), lambda b, h, i: (0, 0)),
                  pl.BlockSpec((1, DA_V_DIM), lambda b, h, i: (0, 0))],
        out_specs=pl.BlockSpec((1, tq, 128), lambda b, h, i: (b, i, h)),
        out_shape=jax.ShapeDtypeStruct((B, n_q_rows, BRANCH_WIDTH), F32),
        scratch_shapes=[pltpu.VMEM((2, vrows, tq), F32)],
        compiler_params=_cparams(("parallel", "parallel", "arbitrary")),
        name="diff_attn",
    )(QT, Ks, VT, lam_vec, sub_g.reshape(1, DA_V_DIM))


def _scan_tile(j, n_ctx, nt, reverse):
    n_lat = nt - n_ctx
    if not reverse:
        return jnp.where(j < n_ctx, n_lat + j, j - n_ctx)
    return jnp.where(j < n_ctx, nt - 1 - j, n_lat - 1 - (j - n_ctx))


def _stream_edges(t, n_ctx, nt):
    n_lat = nt - n_ctx
    return jnp.logical_or(t == 0, t == n_lat), jnp.logical_or(t == n_lat - 1, t == nt - 1)


def _dwconv4(x, prev8, next8, w_ref, is_first, is_last):
    tm = x.shape[0]
    row = lax.broadcasted_iota(jnp.int32, x.shape, 0)
    pf = jnp.where(is_first, 0.0, 1.0)
    nf = jnp.where(is_last, 0.0, 1.0)
    prow = prev8[HALO_ROWS - 1:HALO_ROWS, :] * pf
    n0 = next8[0:1, :] * nf
    n1 = next8[1:2, :] * nf
    xm1 = jnp.where(row == 0, prow, pltpu.roll(x, 1, 0))
    xp1 = jnp.where(row == tm - 1, n0, pltpu.roll(x, tm - 1, 0))
    xp2 = jnp.where(row == tm - 2, n0, jnp.where(row == tm - 1, n1, pltpu.roll(x, tm - 2, 0)))
    return w_ref[0:1, :] * xm1 + w_ref[1:2, :] * x + w_ref[2:3, :] * xp1 + w_ref[3:4, :] * xp2


def _halo_specs(tm, width, col_blk, n_ctx, nt, reverse):
    r8 = tm // HALO_ROWS
    last8 = nt * r8 - 1

    def cur(b, j):
        return (b, _scan_tile(j, n_ctx, nt, reverse), col_blk)

    def prev(b, j):
        return (b, jnp.maximum(_scan_tile(j, n_ctx, nt, reverse) * r8 - 1, 0), col_blk)

    def nxt(b, j):
        return (b, jnp.minimum((_scan_tile(j, n_ctx, nt, reverse) + 1) * r8, last8), col_blk)

    return [pl.BlockSpec((1, tm, width), cur),
            pl.BlockSpec((1, HALO_ROWS, width), prev),
            pl.BlockSpec((1, HALO_ROWS, width), nxt)]


def _linear_scan_tile(a, b, reverse):
    tm = a.shape[0]
    row = lax.broadcasted_iota(jnp.int32, a.shape, 0)
    s = 1
    while s < tm:
        if not reverse:
            keep = row >= s
            ap = jnp.where(keep, pltpu.roll(a, s, 0), 1.0)
            bp = jnp.where(keep, pltpu.roll(b, s, 0), 0.0)
        else:
            keep = row < tm - s
            ap = jnp.where(keep, pltpu.roll(a, tm - s, 0), 1.0)
            bp = jnp.where(keep, pltpu.roll(b, tm - s, 0), 0.0)
        b = a * bp + b
        a = a * ap
        s *= 2
    return a, b


def _lru_kernel(*refs, n_ctx, nt, reverse, final):
    if final:
        (x_ref, xp_ref, xn_ref, y_ref, hf_ref, cw_ref, cb_ref, wg_ref, gb_ref, lam_ref, o_ref, carry) = refs
    else:
        (x_ref, xp_ref, xn_ref, cw_ref, cb_ref, wg_ref, gb_ref, lam_ref, o_ref, carry) = refs
    j = pl.program_id(1)
    t = _scan_tile(j, n_ctx, nt, reverse)
    is_first, is_last = _stream_edges(t, n_ctx, nt)

    @pl.when(j == 0)
    def _():
        carry[...] = jnp.zeros(carry.shape, F32)

    x = x_ref[0]
    tm, W = x.shape
    xc = _dwconv4(x, xp_ref[0], xn_ref[0], cw_ref, is_first, is_last) + cb_ref[...]
    g = jnp.dot(xc.astype(BF16), wg_ref[...], preferred_element_type=F32) + gb_ref[...]
    r = jax.nn.sigmoid(g[:, :W])
    i = jax.nn.sigmoid(g[:, W:])
    log_a = (-LRU_C) * r * _softplus(-lam_ref[...])
    a = jnp.exp(log_a)
    th = jnp.tanh(log_a)
    bb = jnp.sqrt(-2.0 * th / (1.0 - th)) * (i * xc)
    a_c, b_c = _linear_scan_tile(a, bb, reverse)
    h = b_c + a_c * carry[0:1, :]
    carry[0:1, :] = h[0:1, :] if reverse else h[tm - 1:tm, :]
    if final:
        o_ref[0] = jax.nn.gelu(y_ref[0]) * (hf_ref[0] + h)
    else:
        o_ref[0] = h


def _lru_pass(P, cols, conv_w, conv_b, wg, gb, lam, n_ctx, reverse, hf=None):
    B, Tt, _ = P.shape
    tm = ROW_TILE
    W = BRANCH_WIDTH
    nt = Tt // tm
    final = hf is not None

    def cur(b, j):
        return (b, _scan_tile(j, n_ctx, nt, reverse), 0)

    in_specs = _halo_specs(tm, W, cols["lx"] // W, n_ctx, nt, reverse)
    args = [P, P, P]
    if final:
        yb = cols["ly"] // W
        in_specs += [pl.BlockSpec((1, tm, W), lambda b, j: (b, _scan_tile(j, n_ctx, nt, reverse), yb)),
                     pl.BlockSpec((1, tm, W), cur)]
        args += [P, hf]
    in_specs += [pl.BlockSpec((4, W), lambda b, j: (0, 0)),
                 pl.BlockSpec((1, W), lambda b, j: (0, 0)),
                 pl.BlockSpec((W, 2 * W), lambda b, j: (0, 0)),
                 pl.BlockSpec((1, 2 * W), lambda b, j: (0, 0)),
                 pl.BlockSpec((1, W), lambda b, j: (0, 0))]
    args += [conv_w, conv_b.reshape(1, W), wg, gb.reshape(1, 2 * W), lam.reshape(1, W)]
    return pl.pallas_call(
        functools.partial(_lru_kernel, n_ctx=n_ctx, nt=nt, reverse=reverse, final=final),
        grid=(B, nt),
        in_specs=in_specs,
        out_specs=pl.BlockSpec((1, tm, W), cur),
        out_shape=jax.ShapeDtypeStruct((B, Tt, W), F32),
        scratch_shapes=[pltpu.VMEM((HALO_ROWS, W), F32)],
        compiler_params=_cparams(("parallel", "arbitrary")),
        name="rglru_bwd" if reverse else "rglru_fwd",
    )(*args)


def _lru_gate_weights(gate_w_d):
    mats = []
    for g in range(2):
        mats.append(jax.scipy.linalg.block_diag(*[gate_w_d[g, n] for n in range(LRU_BLOCKS)]))
    return jnp.concatenate(mats, axis=1).astype(BF16)


def _split_bf16(x):
    hi = x.astype(BF16)
    lo = (x - hi.astype(F32)).astype(BF16)
    return hi, lo


def _dot3(a_hi, a_lo, b_hi, b_lo):
    return (jnp.dot(a_hi, b_hi, preferred_element_type=F32)
            + (jnp.dot(a_hi, b_lo, preferred_element_type=F32) + jnp.dot(a_lo, b_hi, preferred_element_type=F32)))


def _gdn_kernel(*refs, n_ctx, nt, reverse, final, d):
    if final:
        (x_ref, xp_ref, xn_ref, sm_ref, z_ref, of_ref, cw_ref, rate_ref, dtb_ref, ng_ref, o_ref, S_s, qkv_s) = refs
    else:
        (x_ref, xp_ref, xn_ref, sm_ref, cw_ref, rate_ref, dtb_ref, o_ref, S_s, qkv_s) = refs
    j = pl.program_id(1)
    t = _scan_tile(j, n_ctx, nt, reverse)
    is_first, is_last = _stream_edges(t, n_ctx, nt)

    @pl.when(j == 0)
    def _():
        S_s[...] = jnp.zeros(S_s.shape, F32)

    x = x_ref[0]
    tm = x.shape[0]
    C = GDN_CHUNK
    nC = tm // C
    HK = GDN_HEADS * GDN_DK
    qkv = _dwconv4(x, xp_ref[0], xn_ref[0], cw_ref, is_first, is_last)
    qkv_s[...] = qkv * jax.nn.sigmoid(qkv)

    sm = sm_ref[0]
    beta_all = jax.nn.sigmoid(sm)
    g_all = -jnp.exp(rate_ref[...]) * _softplus(sm + dtb_ref[...])

    ii = lax.broadcasted_iota(jnp.int32, (tm, tm), 0)
    jj = lax.broadcasted_iota(jnp.int32, (tm, tm), 1)
    same = (ii // C) == (jj // C)
    if not reverse:
        incl = jnp.logical_and(same, ii >= jj)
        strict = jnp.logical_and(same, ii > jj)
    else:
        incl = jnp.logical_and(same, ii <= jj)
        strict = jnp.logical_and(same, ii < jj)
    cum_mat = jnp.concatenate([incl.astype(F32), same.astype(F32)], axis=0).astype(BF16)
    eye = (ii == jj).astype(F32)

    for h in range(GDN_HEADS):
        q = qkv_s[:, h * GDN_DK:(h + 1) * GDN_DK]
        k = qkv_s[:, HK + h * GDN_DK:HK + (h + 1) * GDN_DK]
        v = qkv_s[:, 2 * HK + h * GDN_DV:2 * HK + (h + 1) * GDN_DV]
        qn = q * lax.rsqrt(jnp.sum(q * q, axis=-1, keepdims=True) + EPS) * (GDN_DK ** -0.5)
        kn = k * lax.rsqrt(jnp.sum(k * k, axis=-1, keepdims=True) + EPS)
        cb = d * GDN_HEADS + h
        cg = N_DIR * GDN_HEADS + d * GDN_HEADS + h
        beta_b = jnp.broadcast_to(beta_all[:, cb:cb + 1], (tm, 128))
        g_b = jnp.broadcast_to(g_all[:, cg:cg + 1], (tm, 128))
        g1 = g_b.astype(BF16)
        g2 = (g_b - g1.astype(F32)).astype(BF16)
        g3 = (g_b - g1.astype(F32) - g2.astype(F32)).astype(BF16)
        cums = (jnp.dot(cum_mat, g1, preferred_element_type=F32)
                + (jnp.dot(cum_mat, g2, preferred_element_type=F32)
                   + jnp.dot(cum_mat, g3, preferred_element_type=F32)))
        gc_b = cums[:tm]
        gtot_b = cums[tm:]
        eg = jnp.exp(gc_b)
        kb = kn * beta_b
        rhs = jnp.concatenate([v * beta_b, kb * eg], axis=1)
        q_dec = qn * eg
        k_dec = kn * jnp.exp(gtot_b - gc_b)
        g_end = jnp.exp(gtot_b)

        gc_cols = jnp.concatenate([gc_b] * (tm // 128), axis=1)
        gc_rows = gc_b.T[0:1, :]
        decay = jnp.exp(jnp.where(incl, gc_cols - gc_rows, NEG_BIG))
        kq = lax.dot_general(jnp.concatenate([kb, qn], axis=0).astype(BF16), kn.astype(BF16),
                             (((1,), (1,)), ((), ())), preferred_element_type=F32)
        m_k = jnp.where(strict, -(kq[:tm] * decay), 0.0)
        attn = kq[tm:] * decay

        t_inv = eye + m_k
        m_hi, m_lo = _split_bf16(m_k)
        p_k = _dot3(m_hi, m_lo, m_hi, m_lo)
        for _ in range(4):
            p_hi, p_lo = _split_bf16(p_k)
            t_hi, t_lo = _split_bf16(t_inv)
            r2 = _dot3(jnp.concatenate([t_hi, p_hi], axis=0), jnp.concatenate([t_lo, p_lo], axis=0), p_hi, p_lo)
            t_inv = t_inv + r2[:tm]
            p_k = r2[tm:]
        p_hi, p_lo = _split_bf16(p_k)
        t_hi, t_lo = _split_bf16(t_inv)
        t_inv = t_inv + _dot3(t_hi, t_lo, p_hi, p_lo)
        sol = jnp.dot(t_inv.astype(BF16), rhs.astype(BF16), preferred_element_type=F32)
        u = sol[:, :GDN_DV]
        w = sol[:, GDN_DV:]

        S = S_s[h]
        for cc in range(nC):
            c = nC - 1 - cc if reverse else cc
            r0 = c * C
            wq = jnp.concatenate([w[r0:r0 + C], q_dec[r0:r0 + C]], axis=0).astype(BF16)
            ws = jnp.dot(wq, S.astype(BF16), preferred_element_type=F32)
            v_new = u[r0:r0 + C] - ws[:C]
            o_c = ws[C:] + jnp.dot(attn[r0:r0 + C, r0:r0 + C].astype(BF16), v_new.astype(BF16),
                                   preferred_element_type=F32)
            S = S * _col(g_end[r0:r0 + 1, :], GDN_DK) + lax.dot_general(
                k_dec[r0:r0 + C].astype(BF16), v_new.astype(BF16), (((0,), (0,)), ((), ())),
                preferred_element_type=F32)
            if final:
                o_t = of_ref[0, r0:r0 + C, h * GDN_DV:(h + 1) * GDN_DV] + o_c
                ms = jnp.mean(o_t * o_t, axis=-1, keepdims=True)
                z = z_ref[0, r0:r0 + C, h * GDN_DV:(h + 1) * GDN_DV]
                o_ref[0, r0:r0 + C, h * GDN_DV:(h + 1) * GDN_DV] = (
                    o_t * lax.rsqrt(ms + EPS) * ng_ref[...] * (z * jax.nn.sigmoid(z)))
            else:
                o_ref[0, r0:r0 + C, h * GDN_DV:(h + 1) * GDN_DV] = o_c
        S_s[h] = S


def _col(row_vec, n):
    return jnp.broadcast_to(row_vec, (n, row_vec.shape[1]))


def _gdn_pass(P, Psm, cols, conv_w, rate_row, dtb_row, n_ctx, reverse, d, of=None, norm_g=None):
    B, Tt, _ = P.shape
    tm = ROW_TILE
    W = BRANCH_WIDTH
    QW = GDN_HEADS * (2 * GDN_DK + GDN_DV)
    nt = Tt // tm
    final = of is not None

    def cur(b, j):
        return (b, _scan_tile(j, n_ctx, nt, reverse), 0)

    in_specs = _halo_specs(tm, QW, cols["gqkv"] // QW, n_ctx, nt, reverse)
    in_specs += [pl.BlockSpec((1, tm, 128), cur)]
    args = [P, P, P, Psm]
    if final:
        zb = cols["gz"] // W
        in_specs += [pl.BlockSpec((1, tm, W), lambda b, j: (b, _scan_tile(j, n_ctx, nt, reverse), zb)),
                     pl.BlockSpec((1, tm, W), cur)]
        args += [P, of]
    in_specs += [pl.BlockSpec((4, QW), lambda b, j: (0, 0)),
                 pl.BlockSpec((1, 128), lambda b, j: (0, 0)),
                 pl.BlockSpec((1, 128), lambda b, j: (0, 0))]
    args += [conv_w, rate_row, dtb_row]
    if final:
        in_specs += [pl.BlockSpec((1, GDN_DV), lambda b, j: (0, 0))]
        args += [norm_g.reshape(1, GDN_DV)]
    return pl.pallas_call(
        functools.partial(_gdn_kernel, n_ctx=n_ctx, nt=nt, reverse=reverse, final=final, d=d),
        grid=(B, nt),
        in_specs=in_specs,
        out_specs=pl.BlockSpec((1, tm, W), cur),
        out_shape=jax.ShapeDtypeStruct((B, Tt, W), F32),
        scratch_shapes=[pltpu.VMEM((GDN_HEADS, GDN_DK, GDN_DV), F32), pltpu.VMEM((tm, QW), F32)],
        compiler_params=_cparams(("parallel", "arbitrary")),
        name="gdn_bwd" if reverse else "gdn_fwd",
    )(*args)


def _merge_kernel(x_ref, da_ref, lru_ref, gdn_ref, g0_ref, g1_ref, g2_ref, mod_ref, wb_ref, wo_ref, o_ref):
    outs = (da_ref, lru_ref, gdn_ref)
    gates = (g0_ref, g1_ref, g2_ref)
    merged = None
    for i in range(N_BRANCH):
        y = jnp.dot(outs[i][0].astype(BF16), wb_ref[i], preferred_element_type=F32)
        term = jax.nn.sigmoid(gates[i][0]) * y
        merged = term if merged is None else merged + term
    proj = jnp.dot(merged.astype(BF16), wo_ref[...], preferred_element_type=F32)
    o_ref[0] = x_ref[0] + mod_ref[0, 2:3, :] * proj


def _merge(xa, da, lru, gdn, P, mod3, wb, wo, n_lat_tiles, ctx_row):
    B, Tt, D = xa.shape
    tm = ROW_TILE
    W = BRANCH_WIDTH
    row = lambda b, t: (b, t, 0)
    return pl.pallas_call(
        _merge_kernel,
        grid=(B, Tt // tm),
        in_specs=[pl.BlockSpec((1, tm, D), row),
                  pl.BlockSpec((1, tm, W), row), pl.BlockSpec((1, tm, W), row), pl.BlockSpec((1, tm, W), row),
                  pl.BlockSpec((1, tm, D), lambda b, t: (b, t, 0)),
                  pl.BlockSpec((1, tm, D), lambda b, t: (b, t, 1)),
                  pl.BlockSpec((1, tm, D), lambda b, t: (b, t, 2)),
                  pl.BlockSpec((1, N_MOD, D), lambda b, t: (jnp.where(t >= n_lat_tiles, ctx_row, b), 0, 0)),
                  pl.BlockSpec((N_BRANCH, W, D), lambda b, t: (0, 0, 0)),
                  pl.BlockSpec((D, D), lambda b, t: (0, 0))],
        out_specs=pl.BlockSpec((1, tm, D), row),
        out_shape=jax.ShapeDtypeStruct((B, Tt, D), F32),
        compiler_params=_cparams(("parallel", "parallel")),
        name="merge",
    )(xa, da, lru, gdn, P, P, P, mod3, wb, wo)


def _mlp_kernel(x_ref, mod_ref, g_ref, w1_ref, w2_ref, o_ref, *, ff_chunk):
    x = x_ref[0]
    ms = jnp.mean(x * x, axis=-1, keepdims=True)
    y = x * lax.rsqrt(ms + EPS) * g_ref[...]
    h = (y * (1.0 + mod_ref[0, 4:5, :]) + mod_ref[0, 3:4, :]).astype(BF16)
    n_ff = w1_ref.shape[1]
    acc = None
    for c0 in range(0, n_ff, ff_chunk):
        a = jnp.dot(h, w1_ref[:, c0:c0 + ff_chunk], preferred_element_type=F32)
        a = jnp.square(jnp.maximum(a, 0.0)).astype(BF16)
        part = jnp.dot(a, w2_ref[c0:c0 + ff_chunk, :], preferred_element_type=F32)
        acc = part if acc is None else acc + part
    o_ref[0] = x + mod_ref[0, 5:6, :] * acc


def _mlp(xa, mod3, g, w1, w2, n_lat_tiles, ctx_row, n_rows):
    B, Tt, D = xa.shape
    tm = ROW_TILE
    F = w1.shape[1]
    return pl.pallas_call(
        functools.partial(_mlp_kernel, ff_chunk=min(F, 1024)),
        grid=(B, n_rows // tm),
        in_specs=[pl.BlockSpec((1, tm, D), lambda b, t: (b, t, 0)),
                  pl.BlockSpec((1, N_MOD, D), lambda b, t: (jnp.where(t >= n_lat_tiles, ctx_row, b), 0, 0)),
                  pl.BlockSpec((1, D), lambda b, t: (0, 0)),
                  pl.BlockSpec((D, F), lambda b, t: (0, 0)),
                  pl.BlockSpec((F, D), lambda b, t: (0, 0))],
        out_specs=pl.BlockSpec((1, tm, D), lambda b, t: (b, t, 0)),
        out_shape=jax.ShapeDtypeStruct((B, n_rows, D), F32),
        compiler_params=_cparams(("parallel", "parallel")),
        name="mlp",
    )(xa, mod3, g.reshape(1, D), w1, w2)


def _proj_layout(D):
    W = BRANCH_WIDTH
    QW = GDN_HEADS * (2 * GDN_DK + GDN_DV)
    src = {}
    off = 0
    for name, width in (("q", W), ("k", W), ("v", W), ("lx", W), ("ly", W), ("gqkv", QW), ("gz", W),
                        ("gb", N_DIR * GDN_HEADS), ("ga", N_DIR * GDN_HEADS), ("gate", N_BRANCH * D)):
        src[name] = (off, width)
        off += width
    order = ("gate", "gqkv", "q", "k", "v", "lx", "ly", "gz")
    cols = {}
    idx = []
    o = 0
    for name in order:
        s, w = src[name]
        cols[name] = o
        idx.append(jnp.arange(s, s + w))
        o += w
    small_idx = jnp.concatenate([jnp.arange(src["gb"][0], src["gb"][0] + src["gb"][1]),
                                 jnp.arange(src["ga"][0], src["ga"][0] + src["ga"][1])])
    return cols, jnp.concatenate(idx), small_idx


def _rope_tables(n_ctx_rows, n_lat_rows):
    n_freq = DA_QK_DIM // 4
    inv = ROPE_BASE ** (-jnp.arange(n_freq, dtype=F32) / n_freq)
    tpos = jnp.arange(n_lat_rows, dtype=jnp.int32)
    ang_r = (tpos // GRID_W).astype(F32)[:, None] * inv
    ang_c = (tpos % GRID_W).astype(F32)[:, None] * inv
    ang = jnp.concatenate([ang_r, ang_r, ang_c, ang_c], axis=-1)
    ang = jnp.concatenate([ang, ang], axis=-1)
    sign = jnp.where((jnp.arange(128) % 32) < 16, -1.0, 1.0).astype(F32)
    cos = jnp.concatenate([jnp.cos(ang), jnp.ones((n_ctx_rows, 128), F32)], axis=0)
    sin = jnp.concatenate([jnp.sin(ang) * sign, jnp.zeros((n_ctx_rows, 128), F32)], axis=0)
    return cos, sin


def kernel(x, c, ctx, c_ctx, ada_w, ada_b, norm1_g, norm2_g, w_in, da_q_norm_g, da_k_norm_g, da_lambda, da_sub_norm_g, lru_conv_w, lru_conv_b, lru_gate_w, lru_gate_b, lru_lambda, gdn_conv_w, gdn_A_log, gdn_dt_bias, gdn_norm_g, w_branch, w_out, mlp_w1, mlp_w2):
    B, T, D = x.shape
    Tc = ctx.shape[1]
    depth = ada_w.shape[0]
    tm = ROW_TILE
    assert Tc % tm == 0 and T % tm == 0 and B + 1 <= 8
    n_ctx = Tc // tm
    n_lat = T // tm
    Tt = Tc + T
    ctx_row = B

    cols, main_idx, small_idx = _proj_layout(D)
    cos_t, sin_t = _rope_tables(Tc, T)
    xa = jnp.concatenate([x, ctx], axis=1)
    c_all = jnp.concatenate([c, c_ctx[None, :], jnp.zeros((8 - B - 1, D), F32)], axis=0)

    tq = _pick(T, (512, 256))
    n_sub = _pick(Tt // tm, (3, 2, 1))

    out = None
    for layer in range(depth):
        last = layer == depth - 1
        lam_init = 0.8 - 0.6 * math.exp(-0.3 * layer)
        w_main = w_in[layer][:, main_idx].astype(BF16)
        w_small = jnp.pad(w_in[layer][:, small_idx], ((0, 0), (0, 128 - small_idx.shape[0]))).astype(BF16)

        mod3 = _adaln(c_all, ada_w[layer], ada_b[layer]).reshape(8, N_MOD, D)
        h1 = _norm_mod(xa, mod3, norm1_g[layer], 0, n_lat, ctx_row).reshape(B * Tt, D)
        P = _matmul(h1, w_main, "in_proj").reshape(B, Tt, -1)
        Psm = _matmul(h1, w_small, "in_proj_small").reshape(B, Tt, 128)

        QT, Ks, VT = _qkv_prep(P, cols, cos_t, sin_t, da_q_norm_g[layer], da_k_norm_g[layer])
        da_l = _flash(QT, Ks, VT, da_lambda[layer], da_sub_norm_g[layer], lam_init, 0, T, 0, Tt, tq, n_sub)
        if last:
            da = jnp.pad(da_l, ((0, 0), (0, Tc), (0, 0)))
        else:
            da_c = _flash(QT, Ks, VT, da_lambda[layer], da_sub_norm_g[layer], lam_init, T, Tc, T, Tc, tm, 1)
            da = jnp.concatenate([da_l, da_c], axis=1)

        hf = _lru_pass(P, cols, lru_conv_w[layer], lru_conv_b[layer], _lru_gate_weights(lru_gate_w[layer, 0]),
                       lru_gate_b[layer, 0].reshape(-1), lru_lambda[layer, 0], n_ctx, False)
        lru = _lru_pass(P, cols, lru_conv_w[layer], lru_conv_b[layer], _lru_gate_weights(lru_gate_w[layer, 1]),
                        lru_gate_b[layer, 1].reshape(-1), lru_lambda[layer, 1], n_ctx, True, hf=hf)

        rate_row = jnp.zeros((1, 128), F32).at[0, N_DIR * GDN_HEADS:2 * N_DIR * GDN_HEADS].set(
            gdn_A_log[layer].astype(F32).reshape(-1))
        dtb_row = jnp.zeros((1, 128), F32).at[0, N_DIR * GDN_HEADS:2 * N_DIR * GDN_HEADS].set(
            gdn_dt_bias[layer].astype(F32).reshape(-1))
        of = _gdn_pass(P, Psm, cols, gdn_conv_w[layer], rate_row, dtb_row, n_ctx, False, 0)
        gdn = _gdn_pass(P, Psm, cols, gdn_conv_w[layer], rate_row, dtb_row, n_ctx, True, 1, of=of,
                        norm_g=gdn_norm_g[layer])

        xa = _merge(xa, da, lru, gdn, P, mod3, w_branch[layer].astype(BF16), w_out[layer].astype(BF16),
                    n_lat, ctx_row)
        if last:
            out = _mlp(xa, mod3, norm2_g[layer], mlp_w1[layer].astype(BF16), mlp_w2[layer].astype(BF16),
                       n_lat, ctx_row, T)
        else:
            xa = _mlp(xa, mod3, norm2_g[layer], mlp_w1[layer].astype(BF16), mlp_w2[layer].astype(BF16),
                      n_lat, ctx_row, Tt)
    return out
```

```python
import functools
import math

import jax
import jax.numpy as jnp
from jax import lax
from jax.experimental import pallas as pl
from jax.experimental.pallas import tpu as pltpu

F32 = jnp.float32
BF16 = jnp.bfloat16
HIGHEST = lax.Precision.HIGHEST

N_MOD = 6
BRANCH_WIDTH = 512
N_BRANCH = 3
DA_HEADS = 4
DA_QK_DIM = 64
DA_V_DIM = 128
ROPE_BASE = 10000.0
LRU_BLOCKS = 8
LRU_C = 8.0
GDN_HEADS = 4
GDN_DK = 128
GDN_DV = 128
GDN_CHUNK = 64
N_DIR = 2
GRID_W = 64
EPS = 1e-6
NEG_BIG = -1e30

ROW_TILE = 256
HALO_ROWS = 8
VT_PAD = 16
VMEM_LIMIT = 48 * 1024 * 1024

COL_GATE = 0


def _cparams(sem):
    return pltpu.CompilerParams(dimension_semantics=sem, vmem_limit_bytes=VMEM_LIMIT)


def _softplus(z):
    return jnp.maximum(z, 0.0) + jnp.log1p(jnp.exp(-jnp.abs(z)))


def _adaln_kernel(c_ref, w_ref, b_ref, o_ref):
    c = c_ref[...]
    a = c * jax.nn.sigmoid(c)
    o_ref[...] = jnp.dot(a, w_ref[...], preferred_element_type=F32, precision=HIGHEST) + b_ref[...]


def _adaln(c_all, w, b):
    R, D = c_all.shape
    N = w.shape[1]
    tn = 1536 if N % 1536 == 0 else N
    return pl.pallas_call(
        _adaln_kernel,
        grid=(N // tn,),
        in_specs=[pl.BlockSpec((R, D), lambda j: (0, 0)),
                  pl.BlockSpec((D, tn), lambda j: (0, j)),
                  pl.BlockSpec((1, tn), lambda j: (0, j))],
        out_specs=pl.BlockSpec((R, tn), lambda j: (0, j)),
        out_shape=jax.ShapeDtypeStruct((R, N), F32),
        compiler_params=_cparams(("arbitrary",)),
        name="adaln",
    )(c_all, w, b.reshape(1, N))


def _norm_mod_kernel(x_ref, mod_ref, g_ref, o_ref, *, which):
    x = x_ref[0]
    ms = jnp.mean(x * x, axis=-1, keepdims=True)
    y = x * lax.rsqrt(ms + EPS) * g_ref[...]
    sh = mod_ref[0, 3 * which:3 * which + 1, :]
    sc = mod_ref[0, 3 * which + 1:3 * which + 2, :]
    o_ref[0] = (y * (1.0 + sc) + sh).astype(BF16)


def _norm_mod(xa, mod3, g, which, n_lat_tiles, ctx_row):
    B, Tt, D = xa.shape
    tm = ROW_TILE
    return pl.pallas_call(
        functools.partial(_norm_mod_kernel, which=which),
        grid=(B, Tt // tm),
        in_specs=[pl.BlockSpec((1, tm, D), lambda b, t: (b, t, 0)),
                  pl.BlockSpec((1, N_MOD, D), lambda b, t: (jnp.where(t >= n_lat_tiles, ctx_row, b), 0, 0)),
                  pl.BlockSpec((1, D), lambda b, t: (0, 0))],
        out_specs=pl.BlockSpec((1, tm, D), lambda b, t: (b, t, 0)),
        out_shape=jax.ShapeDtypeStruct((B, Tt, D), BF16),
        compiler_params=_cparams(("parallel", "parallel")),
        name="norm_mod",
    )(xa, mod3, g.reshape(1, D))


def _matmul_kernel(h_ref, w_ref, o_ref):
    o_ref[...] = jnp.dot(h_ref[...], w_ref[...], preferred_element_type=F32)


def _pick(n, cands):
    for c in cands:
        if n % c == 0:
            return c
    return n


def _matmul(h2, w, name):
    R, D = h2.shape
    N = w.shape[1]
    tm = _pick(R, (1024, 512, 256))
    tn = _pick(N, (1536, 1280, 1024, 512, 128))
    return pl.pallas_call(
        _matmul_kernel,
        grid=(N // tn, R // tm),
        in_specs=[pl.BlockSpec((tm, D), lambda j, i: (i, 0)),
                  pl.BlockSpec((D, tn), lambda j, i: (0, j))],
        out_specs=pl.BlockSpec((tm, tn), lambda j, i: (i, j)),
        out_shape=jax.ShapeDtypeStruct((R, N), F32),
        compiler_params=_cparams(("parallel", "parallel")),
        name=name,
    )(h2, w)


def _seg_mean(x2, s_ref):
    hi = x2.astype(BF16)
    lo = (x2 - hi.astype(F32)).astype(BF16)
    s = s_ref[...]
    return jnp.dot(hi, s, preferred_element_type=F32) + jnp.dot(lo, s, preferred_element_type=F32)


def _rope(y, cos, sin_signed, first_half):
    rot = jnp.where(first_half, pltpu.roll(y, 128 - 16, 1), pltpu.roll(y, 16, 1))
    return y * cos + rot * sin_signed


def _qkv_prep_kernel(q_ref, k_ref, v_ref, cos_ref, sin_ref, gq_ref, gk_ref, s_ref, qo_ref, ko_ref, vo_ref,
                     *, q_scale):
    cos = cos_ref[...]
    sin = sin_ref[...]
    tm = cos.shape[0]
    lane = lax.broadcasted_iota(jnp.int32, (tm, 128), 1)
    first_half = (lane % 32) < 16
    comp0 = lane < DA_QK_DIM

    q = q_ref[0]
    qn = q * lax.rsqrt(_seg_mean(q * q, s_ref) + EPS) * gq_ref[...]
    k = k_ref[0]
    kn = k * lax.rsqrt(_seg_mean(k * k, s_ref) + EPS) * gk_ref[...]
    v = v_ref[0]
    ones_row = jnp.where(lax.broadcasted_iota(jnp.int32, (VT_PAD, tm), 0) == 0, 1.0, 0.0).astype(BF16)
    for h in range(DA_HEADS):
        qh = _rope(qn[:, h * 128:(h + 1) * 128], cos, sin, first_half) * q_scale
        qo_ref[0, h * 128:(h + 1) * 128, :] = qh.T.astype(BF16)
        kh = _rope(kn[:, h * 128:(h + 1) * 128], cos, sin, first_half)
        ko_ref[0, :, h * 256:h * 256 + 128] = jnp.where(comp0, kh, 0.0).astype(BF16)
        ko_ref[0, :, h * 256 + 128:(h + 1) * 256] = jnp.where(comp0, 0.0, kh).astype(BF16)
        vo_ref[0, h, 0, 0:DA_V_DIM, :] = v[:, h * DA_V_DIM:(h + 1) * DA_V_DIM].T.astype(BF16)
        vo_ref[0, h, 0, DA_V_DIM:DA_V_DIM + VT_PAD, :] = ones_row


def _qkv_prep(P, cols, cos_t, sin_t, gq, gk):
    B, Tt, _ = P.shape
    tm = ROW_TILE
    W = BRANCH_WIDTH
    seg = (jnp.arange(W)[:, None] // DA_QK_DIM == jnp.arange(W)[None, :] // DA_QK_DIM)
    smat = (seg.astype(F32) / DA_QK_DIM).astype(BF16)
    gq_t = jnp.tile(gq, W // DA_QK_DIM).reshape(1, W)
    gk_t = jnp.tile(gk, W // DA_QK_DIM).reshape(1, W)
    qb, kb, vb = cols["q"] // W, cols["k"] // W, cols["v"] // W
    q_scale = (DA_QK_DIM ** -0.5) * math.log2(math.e)
    return pl.pallas_call(
        functools.partial(_qkv_prep_kernel, q_scale=q_scale),
        grid=(B, Tt // tm),
        in_specs=[pl.BlockSpec((1, tm, W), lambda b, t: (b, t, qb)),
                  pl.BlockSpec((1, tm, W), lambda b, t: (b, t, kb)),
                  pl.BlockSpec((1, tm, W), lambda b, t: (b, t, vb)),
                  pl.BlockSpec((tm, 128), lambda b, t: (t, 0)),
                  pl.BlockSpec((tm, 128), lambda b, t: (t, 0)),
                  pl.BlockSpec((1, W), lambda b, t: (0, 0)),
                  pl.BlockSpec((1, W), lambda b, t: (0, 0)),
                  pl.BlockSpec((W, W), lambda b, t: (0, 0))],
        out_specs=[pl.BlockSpec((1, W, tm), lambda b, t: (b, 0, t)),
                   pl.BlockSpec((1, tm, 2 * W), lambda b, t: (b, t, 0)),
                   pl.BlockSpec((1, DA_HEADS, 1, DA_V_DIM + VT_PAD, tm), lambda b, t: (b, 0, t, 0, 0))],
        out_shape=[jax.ShapeDtypeStruct((B, W, Tt), BF16),
                   jax.ShapeDtypeStruct((B, Tt, 2 * W), BF16),
                   jax.ShapeDtypeStruct((B, DA_HEADS, Tt // tm, DA_V_DIM + VT_PAD, tm), BF16)],
        compiler_params=_cparams(("parallel", "parallel")),
        name="qkv_prep",
    )(P, P, P, cos_t, sin_t, gq_t, gk_t, smat)


def _flash_kernel(qt_ref, k_ref, vt_ref, lam_ref, subg_ref, o_ref, acc_s,
                  *, tq, sub, n_sub, nk, chunk0, lam_init, unroll):
    qt = qt_ref[0]
    acc_s[...] = jnp.zeros(acc_s.shape, F32)
    tk = sub * n_sub

    def body(j, ms):
        c0 = chunk0 + j * n_sub
        r0 = pl.multiple_of(c0 * sub, sub)
        kk = k_ref[0, pl.ds(r0, tk), :]
        k2 = jnp.concatenate([kk[:, :128], kk[:, 128:]], axis=0)
        st = jnp.dot(k2, qt, preferred_element_type=F32)
        vt = jnp.concatenate([vt_ref[0, 0, c0 + i] for i in range(n_sub)], axis=1)
        out = []
        for c in range(2):
            s_c = st[c * tk:(c + 1) * tk]
            m_old = ms[c]
            m_new = jnp.maximum(m_old, jnp.max(s_c, axis=0, keepdims=True))
            alpha = jnp.exp2(m_old - m_new)
            p = jnp.exp2(s_c - m_new).astype(BF16)
            acc_s[c] = alpha * acc_s[c] + jnp.dot(vt, p, preferred_element_type=F32)
            out.append(m_new)
        return tuple(out)

    m_init = jnp.full((1, tq), -jnp.inf, F32)
    lax.fori_loop(0, nk, body, (m_init, m_init), unroll=unroll)

    a0 = acc_s[0]
    a1 = acc_s[1]
    o0 = a0[:DA_V_DIM] / a0[DA_V_DIM:DA_V_DIM + 1]
    o1 = a1[:DA_V_DIM] / a1[DA_V_DIM:DA_V_DIM + 1]
    lv = lam_ref[...]
    s01 = jnp.sum(lv[0:1] * lv[1:2], axis=-1, keepdims=True)
    s23 = jnp.sum(lv[2:3] * lv[3:4], axis=-1, keepdims=True)
    lam = jnp.exp(s01) - jnp.exp(s23) + lam_init
    d = o0 - lam * o1
    ms_ = jnp.mean(d * d, axis=0, keepdims=True)
    o_ref[0] = (d * lax.rsqrt(ms_ + EPS)).T * subg_ref[...] * (1.0 - lam_init)


def _flash(QT, Ks, VT, lam_vec, sub_g, lam_init, q_row0, n_q_rows, k_row0, n_k_rows, tq, n_sub):
    B, _, Tt = QT.shape
    sub = VT.shape[-1]
    tk = sub * n_sub
    nq, nk = n_q_rows // tq, n_k_rows // tk
    q0 = q_row0 // tq
    vrows = VT.shape[3]
    return pl.pallas_call(
        functools.partial(_flash_kernel, tq=tq, sub=sub, n_sub=n_sub, nk=nk, chunk0=k_row0 // sub,
                          lam_init=lam_init, unroll=True),
        grid=(B, DA_HEADS, nq),
        in_specs=[pl.BlockSpec((1, 128, tq), lambda b, h, i: (b, h, q0 + i)),
                  pl.BlockSpec((1, Tt, 256), lambda b, h, i: (b, 0, h)),
                  pl.BlockSpec((1, 1, VT.shape[2], vrows, sub), lambda b, h, i: (b, h, 0, 0, 0)),
                  pl.BlockSpec((4, DA_QK_DIM), lambda b, h, i: (0, 0)),
                  pl.BlockSpec((1, DA_V_DIM), lambda b, h, i: (0, 0))],
        out_specs=pl.BlockSpec((1, tq, 128), lambda b, h, i: (b, i, h)),
        out_shape=jax.ShapeDtypeStruct((B, n_q_rows, BRANCH_WIDTH), F32),
        scratch_shapes=[pltpu.VMEM((2, vrows, tq), F32)],
        compiler_params=_cparams(("parallel", "parallel", "arbitrary")),
        name="diff_attn",
    )(QT, Ks, VT, lam_vec, sub_g.reshape(1, DA_V_DIM))


def _scan_tile(j, n_ctx, nt, reverse):
    n_lat = nt - n_ctx
    if not reverse:
        return jnp.where(j < n_ctx, n_lat + j, j - n_ctx)
    return jnp.where(j < n_ctx, nt - 1 - j, n_lat - 1 - (j - n_ctx))


def _stream_edges(t, n_ctx, nt):
    n_lat = nt - n_ctx
    return jnp.logical_or(t == 0, t == n_lat), jnp.logical_or(t == n_lat - 1, t == nt - 1)


def _dwconv4(x, prev8, next8, w_ref, is_first, is_last):
    tm = x.shape[0]
    row = lax.broadcasted_iota(jnp.int32, x.shape, 0)
    pf = jnp.where(is_first, 0.0, 1.0)
    nf = jnp.where(is_last, 0.0, 1.0)
    prow = prev8[HALO_ROWS - 1:HALO_ROWS, :] * pf
    n0 = next8[0:1, :] * nf
    n1 = next8[1:2, :] * nf
    xm1 = jnp.where(row == 0, prow, pltpu.roll(x, 1, 0))
    xp1 = jnp.where(row == tm - 1, n0, pltpu.roll(x, tm - 1, 0))
    xp2 = jnp.where(row == tm - 2, n0, jnp.where(row == tm - 1, n1, pltpu.roll(x, tm - 2, 0)))
    return w_ref[0:1, :] * xm1 + w_ref[1:2, :] * x + w_ref[2:3, :] * xp1 + w_ref[3:4, :] * xp2


def _halo_specs(tm, width, col_blk, n_ctx, nt, reverse):
    r8 = tm // HALO_ROWS
    last8 = nt * r8 - 1

    def cur(b, j):
        return (b, _scan_tile(j, n_ctx, nt, reverse), col_blk)

    def prev(b, j):
        return (b, jnp.maximum(_scan_tile(j, n_ctx, nt, reverse) * r8 - 1, 0), col_blk)

    def nxt(b, j):
        return (b, jnp.minimum((_scan_tile(j, n_ctx, nt, reverse) + 1) * r8, last8), col_blk)

    return [pl.BlockSpec((1, tm, width), cur),
            pl.BlockSpec((1, HALO_ROWS, width), prev),
            pl.BlockSpec((1, HALO_ROWS, width), nxt)]


def _linear_scan_tile(a, b, reverse):
    tm = a.shape[0]
    row = lax.broadcasted_iota(jnp.int32, a.shape, 0)
    s = 1
    while s < tm:
        if not reverse:
            keep = row >= s
            ap = jnp.where(keep, pltpu.roll(a, s, 0), 1.0)
            bp = jnp.where(keep, pltpu.roll(b, s, 0), 0.0)
        else:
            keep = row < tm - s
            ap = jnp.where(keep, pltpu.roll(a, tm - s, 0), 1.0)
            bp = jnp.where(keep, pltpu.roll(b, tm - s, 0), 0.0)
        b = a * bp + b
        a = a * ap
        s *= 2
    return a, b


def _lru_kernel(*refs, n_ctx, nt, reverse, final):
    if final:
        (x_ref, xp_ref, xn_ref, y_ref, hf_ref, cw_ref, cb_ref, wg_ref, gb_ref, lam_ref, o_ref, carry) = refs
    else:
        (x_ref, xp_ref, xn_ref, cw_ref, cb_ref, wg_ref, gb_ref, lam_ref, o_ref, carry) = refs
    j = pl.program_id(1)
    t = _scan_tile(j, n_ctx, nt, reverse)
    is_first, is_last = _stream_edges(t, n_ctx, nt)

    @pl.when(j == 0)
    def _():
        carry[...] = jnp.zeros(carry.shape, F32)

    x = x_ref[0]
    tm, W = x.shape
    xc = _dwconv4(x, xp_ref[0], xn_ref[0], cw_ref, is_first, is_last) + cb_ref[...]
    g = jnp.dot(xc.astype(BF16), wg_ref[...], preferred_element_type=F32) + gb_ref[...]
    r = jax.nn.sigmoid(g[:, :W])
    i = jax.nn.sigmoid(g[:, W:])
    log_a = (-LRU_C) * r * _softplus(-lam_ref[...])
    a = jnp.exp(log_a)
    th = jnp.tanh(log_a)
    bb = jnp.sqrt(-2.0 * th / (1.0 - th)) * (i * xc)
    a_c, b_c = _linear_scan_tile(a, bb, reverse)
    h = b_c + a_c * carry[0:1, :]
    carry[0:1, :] = h[0:1, :] if reverse else h[tm - 1:tm, :]
    if final:
        o_ref[0] = jax.nn.gelu(y_ref[0]) * (hf_ref[0] + h)
    else:
        o_ref[0] = h


def _lru_pass(P, cols, conv_w, conv_b, wg, gb, lam, n_ctx, reverse, hf=None):
    B, Tt, _ = P.shape
    tm = ROW_TILE
    W = BRANCH_WIDTH
    nt = Tt // tm
    final = hf is not None

    def cur(b, j):
        return (b, _scan_tile(j, n_ctx, nt, reverse), 0)

    in_specs = _halo_specs(tm, W, cols["lx"] // W, n_ctx, nt, reverse)
    args = [P, P, P]
    if final:
        yb = cols["ly"] // W
        in_specs += [pl.BlockSpec((1, tm, W), lambda b, j: (b, _scan_tile(j, n_ctx, nt, reverse), yb)),
                     pl.BlockSpec((1, tm, W), cur)]
        args += [P, hf]
    in_specs += [pl.BlockSpec((4, W), lambda b, j: (0, 0)),
                 pl.BlockSpec((1, W), lambda b, j: (0, 0)),
                 pl.BlockSpec((W, 2 * W), lambda b, j: (0, 0)),
                 pl.BlockSpec((1, 2 * W), lambda b, j: (0, 0)),
                 pl.BlockSpec((1, W), lambda b, j: (0, 0))]
    args += [conv_w, conv_b.reshape(1, W), wg, gb.reshape(1, 2 * W), lam.reshape(1, W)]
    return pl.pallas_call(
        functools.partial(_lru_kernel, n_ctx=n_ctx, nt=nt, reverse=reverse, final=final),
        grid=(B, nt),
        in_specs=in_specs,
        out_specs=pl.BlockSpec((1, tm, W), cur),
        out_shape=jax.ShapeDtypeStruct((B, Tt, W), F32),
        scratch_shapes=[pltpu.VMEM((HALO_ROWS, W), F32)],
        compiler_params=_cparams(("parallel", "arbitrary")),
        name="rglru_bwd" if reverse else "rglru_fwd",
    )(*args)


def _lru_gate_weights(gate_w_d):
    mats = []
    for g in range(2):
        mats.append(jax.scipy.linalg.block_diag(*[gate_w_d[g, n] for n in range(LRU_BLOCKS)]))
    return jnp.concatenate(mats, axis=1).astype(BF16)


def _split_bf16(x):
    hi = x.astype(BF16)
    lo = (x - hi.astype(F32)).astype(BF16)
    return hi, lo


def _dot3(a_hi, a_lo, b_hi, b_lo):
    return (jnp.dot(a_hi, b_hi, preferred_element_type=F32)
            + (jnp.dot(a_hi, b_lo, preferred_element_type=F32) + jnp.dot(a_lo, b_hi, preferred_element_type=F32)))


def _gdn_kernel(*refs, n_ctx, nt, reverse, final, d):
    if final:
        (x_ref, xp_ref, xn_ref, sm_ref, z_ref, of_ref, cw_ref, rate_ref, dtb_ref, ng_ref, o_ref, S_s, qkv_s) = refs
    else:
        (x_ref, xp_ref, xn_ref, sm_ref, cw_ref, rate_ref, dtb_ref, o_ref, S_s, qkv_s) = refs
    j = pl.program_id(1)
    t = _scan_tile(j, n_ctx, nt, reverse)
    is_first, is_last = _stream_edges(t, n_ctx, nt)

    @pl.when(j == 0)
    def _():
        S_s[...] = jnp.zeros(S_s.shape, F32)

    x = x_ref[0]
    tm = x.shape[0]
    C = GDN_CHUNK
    nC = tm // C
    HK = GDN_HEADS * GDN_DK
    qkv = _dwconv4(x, xp_ref[0], xn_ref[0], cw_ref, is_first, is_last)
    qkv_s[...] = qkv * jax.nn.sigmoid(qkv)

    sm = sm_ref[0]
    beta_all = jax.nn.sigmoid(sm)
    g_all = -jnp.exp(rate_ref[...]) * _softplus(sm + dtb_ref[...])

    ii = lax.broadcasted_iota(jnp.int32, (tm, tm), 0)
    jj = lax.broadcasted_iota(jnp.int32, (tm, tm), 1)
    same = (ii // C) == (jj // C)
    if not reverse:
        incl = jnp.logical_and(same, ii >= jj)
        strict = jnp.logical_and(same, ii > jj)
    else:
        incl = jnp.logical_and(same, ii <= jj)
        strict = jnp.logical_and(same, ii < jj)
    incl_b = incl.astype(F32).astype(BF16)
    eye = (ii == jj).astype(F32)
    heads = range(GDN_HEADS)

    g1 = g_all.astype(BF16)
    g2 = (g_all - g1.astype(F32)).astype(BF16)
    g3 = (g_all - g1.astype(F32) - g2.astype(F32)).astype(BF16)
    gc_all = (jnp.dot(incl_b, g1, preferred_element_type=F32)
              + (jnp.dot(incl_b, g2, preferred_element_type=F32) + jnp.dot(incl_b, g3, preferred_element_type=F32)))
    last = 0 if reverse else C - 1
    gtot_all = jnp.concatenate(
        [jnp.broadcast_to(gc_all[c * C + last:c * C + last + 1, :], (C, 128)) for c in range(nC)], axis=0)

    rhs, q_dec, k_dec, g_end, m_k, attn = [], [], [], [], [], []
    for h in heads:
        q = qkv_s[:, h * GDN_DK:(h + 1) * GDN_DK]
        k = qkv_s[:, HK + h * GDN_DK:HK + (h + 1) * GDN_DK]
        v = qkv_s[:, 2 * HK + h * GDN_DV:2 * HK + (h + 1) * GDN_DV]
        qn = q * lax.rsqrt(jnp.sum(q * q, axis=-1, keepdims=True) + EPS) * (GDN_DK ** -0.5)
        kn = k * lax.rsqrt(jnp.sum(k * k, axis=-1, keepdims=True) + EPS)
        cb = d * GDN_HEADS + h
        cg = N_DIR * GDN_HEADS + d * GDN_HEADS + h
        beta_b = jnp.broadcast_to(beta_all[:, cb:cb + 1], (tm, 128))
        gc_b = jnp.broadcast_to(gc_all[:, cg:cg + 1], (tm, 128))
        gtot_b = jnp.broadcast_to(gtot_all[:, cg:cg + 1], (tm, 128))
        eg = jnp.exp(gc_b)
        kb = kn * beta_b
        rhs.append(jnp.concatenate([v * beta_b, kb * eg], axis=1).astype(BF16))
        q_dec.append(qn * eg)
        k_dec.append(kn * jnp.exp(gtot_b - gc_b))
        g_end.append(jnp.exp(gtot_b))
        gc_cols = jnp.concatenate([gc_b] * (tm // 128), axis=1)
        gc_rows = gc_b.T[0:1, :]
        decay = jnp.exp(jnp.where(incl, gc_cols - gc_rows, NEG_BIG))
        kq = lax.dot_general(jnp.concatenate([kb, qn], axis=0).astype(BF16), kn.astype(BF16),
                             (((1,), (1,)), ((), ())), preferred_element_type=F32)
        m_k.append(jnp.where(strict, -(kq[:tm] * decay), 0.0))
        attn.append((kq[tm:] * decay).astype(BF16))

    t_inv = [eye + m_k[h] for h in heads]
    p_k = []
    for h in heads:
        m_hi, m_lo = _split_bf16(m_k[h])
        p_k.append(_dot3(m_hi, m_lo, m_hi, m_lo))
    for _ in range(4):
        for h in heads:
            p_hi, p_lo = _split_bf16(p_k[h])
            t_hi, t_lo = _split_bf16(t_inv[h])
            r2 = _dot3(jnp.concatenate([t_hi, p_hi], axis=0), jnp.concatenate([t_lo, p_lo], axis=0), p_hi, p_lo)
            t_inv[h] = t_inv[h] + r2[:tm]
            p_k[h] = r2[tm:]
    u, w = [], []
    for h in heads:
        p_hi, p_lo = _split_bf16(p_k[h])
        t_hi, t_lo = _split_bf16(t_inv[h])
        t_fin = t_inv[h] + _dot3(t_hi, t_lo, p_hi, p_lo)
        sol = jnp.dot(t_fin.astype(BF16), rhs[h], preferred_element_type=F32)
        u.append(sol[:, :GDN_DV])
        w.append(sol[:, GDN_DV:])

    S = [S_s[h] for h in heads]
    for cc in range(nC):
        c = nC - 1 - cc if reverse else cc
        r0 = c * C
        for h in heads:
            wq = jnp.concatenate([w[h][r0:r0 + C], q_dec[h][r0:r0 + C]], axis=0).astype(BF16)
            ws = jnp.dot(wq, S[h].astype(BF16), preferred_element_type=F32)
            v_new = (u[h][r0:r0 + C] - ws[:C]).astype(BF16)
            o_c = ws[C:] + jnp.dot(attn[h][r0:r0 + C, r0:r0 + C], v_new, preferred_element_type=F32)
            S[h] = S[h] * _col(g_end[h][r0:r0 + 1, :], GDN_DK) + lax.dot_general(
                k_dec[h][r0:r0 + C].astype(BF16), v_new, (((0,), (0,)), ((), ())),
                preferred_element_type=F32)
            if final:
                o_t = of_ref[0, r0:r0 + C, h * GDN_DV:(h + 1) * GDN_DV] + o_c
                ms = jnp.mean(o_t * o_t, axis=-1, keepdims=True)
                z = z_ref[0, r0:r0 + C, h * GDN_DV:(h + 1) * GDN_DV]
                o_ref[0, r0:r0 + C, h * GDN_DV:(h + 1) * GDN_DV] = (
                    o_t * lax.rsqrt(ms + EPS) * ng_ref[...] * (z * jax.nn.sigmoid(z)))
            else:
                o_ref[0, r0:r0 + C, h * GDN_DV:(h + 1) * GDN_DV] = o_c
    for h in heads:
        S_s[h] = S[h]


def _col(row_vec, n):
    return jnp.broadcast_to(row_vec, (n, row_vec.shape[1]))


def _gdn_pass(P, Psm, cols, conv_w, rate_row, dtb_row, n_ctx, reverse, d, of=None, norm_g=None):
    B, Tt, _ = P.shape
    tm = ROW_TILE
    W = BRANCH_WIDTH
    QW = GDN_HEADS * (2 * GDN_DK + GDN_DV)
    nt = Tt // tm
    final = of is not None

    def cur(b, j):
        return (b, _scan_tile(j, n_ctx, nt, reverse), 0)

    in_specs = _halo_specs(tm, QW, cols["gqkv"] // QW, n_ctx, nt, reverse)
    in_specs += [pl.BlockSpec((1, tm, 128), cur)]
    args = [P, P, P, Psm]
    if final:
        zb = cols["gz"] // W
        in_specs += [pl.BlockSpec((1, tm, W), lambda b, j: (b, _scan_tile(j, n_ctx, nt, reverse), zb)),
                     pl.BlockSpec((1, tm, W), cur)]
        args += [P, of]
    in_specs += [pl.BlockSpec((4, QW), lambda b, j: (0, 0)),
                 pl.BlockSpec((1, 128), lambda b, j: (0, 0)),
                 pl.BlockSpec((1, 128), lambda b, j: (0, 0))]
    args += [conv_w, rate_row, dtb_row]
    if final:
        in_specs += [pl.BlockSpec((1, GDN_DV), lambda b, j: (0, 0))]
        args += [norm_g.reshape(1, GDN_DV)]
    return pl.pallas_call(
        functools.partial(_gdn_kernel, n_ctx=n_ctx, nt=nt, reverse=reverse, final=final, d=d),
        grid=(B, nt),
        in_specs=in_specs,
        out_specs=pl.BlockSpec((1, tm, W), cur),
        out_shape=jax.ShapeDtypeStruct((B, Tt, W), F32),
        scratch_shapes=[pltpu.VMEM((GDN_HEADS, GDN_DK, GDN_DV), F32), pltpu.VMEM((tm, QW), F32)],
        compiler_params=_cparams(("parallel", "arbitrary")),
        name="gdn_bwd" if reverse else "gdn_fwd",
    )(*args)


def _merge_kernel(x_ref, da_ref, lru_ref, gdn_ref, g0_ref, g1_ref, g2_ref, mod_ref, wb_ref, wo_ref, o_ref):
    outs = (da_ref, lru_ref, gdn_ref)
    gates = (g0_ref, g1_ref, g2_ref)
    merged = None
    for i in range(N_BRANCH):
        y = jnp.dot(outs[i][0].astype(BF16), wb_ref[i], preferred_element_type=F32)
        term = jax.nn.sigmoid(gates[i][0]) * y
        merged = term if merged is None else merged + term
    proj = jnp.dot(merged.astype(BF16), wo_ref[...], preferred_element_type=F32)
    o_ref[0] = x_ref[0] + mod_ref[0, 2:3, :] * proj


def _merge(xa, da, lru, gdn, P, mod3, wb, wo, n_lat_tiles, ctx_row):
    B, Tt, D = xa.shape
    tm = ROW_TILE
    W = BRANCH_WIDTH
    row = lambda b, t: (b, t, 0)
    return pl.pallas_call(
        _merge_kernel,
        grid=(B, Tt // tm),
        in_specs=[pl.BlockSpec((1, tm, D), row),
                  pl.BlockSpec((1, tm, W), row), pl.BlockSpec((1, tm, W), row), pl.BlockSpec((1, tm, W), row),
                  pl.BlockSpec((1, tm, D), lambda b, t: (b, t, 0)),
                  pl.BlockSpec((1, tm, D), lambda b, t: (b, t, 1)),
                  pl.BlockSpec((1, tm, D), lambda b, t: (b, t, 2)),
                  pl.BlockSpec((1, N_MOD, D), lambda b, t: (jnp.where(t >= n_lat_tiles, ctx_row, b), 0, 0)),
                  pl.BlockSpec((N_BRANCH, W, D), lambda b, t: (0, 0, 0)),
                  pl.BlockSpec((D, D), lambda b, t: (0, 0))],
        out_specs=pl.BlockSpec((1, tm, D), row),
        out_shape=jax.ShapeDtypeStruct((B, Tt, D), F32),
        compiler_params=_cparams(("parallel", "parallel")),
        name="merge",
    )(xa, da, lru, gdn, P, P, P, mod3, wb, wo)


def _mlp_kernel(x_ref, mod_ref, g_ref, w1_ref, w2_ref, o_ref, *, ff_chunk):
    x = x_ref[0]
    ms = jnp.mean(x * x, axis=-1, keepdims=True)
    y = x * lax.rsqrt(ms + EPS) * g_ref[...]
    h = (y * (1.0 + mod_ref[0, 4:5, :]) + mod_ref[0, 3:4, :]).astype(BF16)
    n_ff = w1_ref.shape[1]
    acc = None
    for c0 in range(0, n_ff, ff_chunk):
        a = jnp.dot(h, w1_ref[:, c0:c0 + ff_chunk], preferred_element_type=F32)
        a = jnp.square(jnp.maximum(a, 0.0)).astype(BF16)
        part = jnp.dot(a, w2_ref[c0:c0 + ff_chunk, :], preferred_element_type=F32)
        acc = part if acc is None else acc + part
    o_ref[0] = x + mod_ref[0, 5:6, :] * acc


def _mlp(xa, mod3, g, w1, w2, n_lat_tiles, ctx_row, n_rows):
    B, Tt, D = xa.shape
    tm = ROW_TILE
    F = w1.shape[1]
    return pl.pallas_call(
        functools.partial(_mlp_kernel, ff_chunk=min(F, 1024)),
        grid=(B, n_rows // tm),
        in_specs=[pl.BlockSpec((1, tm, D), lambda b, t: (b, t, 0)),
                  pl.BlockSpec((1, N_MOD, D), lambda b, t: (jnp.where(t >= n_lat_tiles, ctx_row, b), 0, 0)),
                  pl.BlockSpec((1, D), lambda b, t: (0, 0)),
                  pl.BlockSpec((D, F), lambda b, t: (0, 0)),
                  pl.BlockSpec((F, D), lambda b, t: (0, 0))],
        out_specs=pl.BlockSpec((1, tm, D), lambda b, t: (b, t, 0)),
        out_shape=jax.ShapeDtypeStruct((B, n_rows, D), F32),
        compiler_params=_cparams(("parallel", "parallel")),
        name="mlp",
    )(xa, mod3, g.reshape(1, D), w1, w2)


def _proj_layout(D):
    W = BRANCH_WIDTH
    QW = GDN_HEADS * (2 * GDN_DK + GDN_DV)
    src = {}
    off = 0
    for name, width in (("q", W), ("k", W), ("v", W), ("lx", W), ("ly", W), ("gqkv", QW), ("gz", W),
                        ("gb", N_DIR * GDN_HEADS), ("ga", N_DIR * GDN_HEADS), ("gate", N_BRANCH * D)):
        src[name] = (off, width)
        off += width
    order = ("gate", "gqkv", "q", "k", "v", "lx", "ly", "gz")
    cols = {}
    idx = []
    o = 0
    for name in order:
        s, w = src[name]
        cols[name] = o
        idx.append(jnp.arange(s, s + w))
        o += w
    small_idx = jnp.concatenate([jnp.arange(src["gb"][0], src["gb"][0] + src["gb"][1]),
                                 jnp.arange(src["ga"][0], src["ga"][0] + src["ga"][1])])
    return cols, jnp.concatenate(idx), small_idx


def _rope_tables(n_ctx_rows, n_lat_rows):
    n_freq = DA_QK_DIM // 4
    inv = ROPE_BASE ** (-jnp.arange(n_freq, dtype=F32) / n_freq)
    tpos = jnp.arange(n_lat_rows, dtype=jnp.int32)
    ang_r = (tpos // GRID_W).astype(F32)[:, None] * inv
    ang_c = (tpos % GRID_W).astype(F32)[:, None] * inv
    ang = jnp.concatenate([ang_r, ang_r, ang_c, ang_c], axis=-1)
    ang = jnp.concatenate([ang, ang], axis=-1)
    sign = jnp.where((jnp.arange(128) % 32) < 16, -1.0, 1.0).astype(F32)
    cos = jnp.concatenate([jnp.cos(ang), jnp.ones((n_ctx_rows, 128), F32)], axis=0)
    sin = jnp.concatenate([jnp.sin(ang) * sign, jnp.zeros((n_ctx_rows, 128), F32)], axis=0)
    return cos, sin


def kernel(x, c, ctx, c_ctx, ada_w, ada_b, norm1_g, norm2_g, w_in, da_q_norm_g, da_k_norm_g, da_lambda, da_sub_norm_g, lru_conv_w, lru_conv_b, lru_gate_w, lru_gate_b, lru_lambda, gdn_conv_w, gdn_A_log, gdn_dt_bias, gdn_norm_g, w_branch, w_out, mlp_w1, mlp_w2):
    B, T, D = x.shape
    Tc = ctx.shape[1]
    depth = ada_w.shape[0]
    tm = ROW_TILE
    assert Tc % tm == 0 and T % tm == 0 and B + 1 <= 8
    n_ctx = Tc // tm
    n_lat = T // tm
    Tt = Tc + T
    ctx_row = B

    cols, main_idx, small_idx = _proj_layout(D)
    cos_t, sin_t = _rope_tables(Tc, T)
    xa = jnp.concatenate([x, ctx], axis=1)
    c_all = jnp.concatenate([c, c_ctx[None, :], jnp.zeros((8 - B - 1, D), F32)], axis=0)

    tq = _pick(T, (512, 256))
    n_sub = _pick(Tt // tm, (3, 2, 1))

    out = None
    for layer in range(depth):
        last = layer == depth - 1
        lam_init = 0.8 - 0.6 * math.exp(-0.3 * layer)
        w_main = w_in[layer][:, main_idx].astype(BF16)
        w_small = jnp.pad(w_in[layer][:, small_idx], ((0, 0), (0, 128 - small_idx.shape[0]))).astype(BF16)

        mod3 = _adaln(c_all, ada_w[layer], ada_b[layer]).reshape(8, N_MOD, D)
        h1 = _norm_mod(xa, mod3, norm1_g[layer], 0, n_lat, ctx_row).reshape(B * Tt, D)
        P = _matmul(h1, w_main, "in_proj").reshape(B, Tt, -1)
        Psm = _matmul(h1, w_small, "in_proj_small").reshape(B, Tt, 128)

        QT, Ks, VT = _qkv_prep(P, cols, cos_t, sin_t, da_q_norm_g[layer], da_k_norm_g[layer])
        da_l = _flash(QT, Ks, VT, da_lambda[layer], da_sub_norm_g[layer], lam_init, 0, T, 0, Tt, tq, n_sub)
        if last:
            da = jnp.pad(da_l, ((0, 0), (0, Tc), (0, 0)))
        else:
            da_c = _flash(QT, Ks, VT, da_lambda[layer], da_sub_norm_g[layer], lam_init, T, Tc, T, Tc, tm, 1)
            da = jnp.concatenate([da_l, da_c], axis=1)

        hf = _lru_pass(P, cols, lru_conv_w[layer], lru_conv_b[layer], _lru_gate_weights(lru_gate_w[layer, 0]),
                       lru_gate_b[layer, 0].reshape(-1), lru_lambda[layer, 0], n_ctx, False)
        lru = _lru_pass(P, cols, lru_conv_w[layer], lru_conv_b[layer], _lru_gate_weights(lru_gate_w[layer, 1]),
                        lru_gate_b[layer, 1].reshape(-1), lru_lambda[layer, 1], n_ctx, True, hf=hf)

        rate_row = jnp.zeros((1, 128), F32).at[0, N_DIR * GDN_HEADS:2 * N_DIR * GDN_HEADS].set(
            gdn_A_log[layer].astype(F32).reshape(-1))
        dtb_row = jnp.zeros((1, 128), F32).at[0, N_DIR * GDN_HEADS:2 * N_DIR * GDN_HEADS].set(
            gdn_dt_bias[layer].astype(F32).reshape(-1))
        of = _gdn_pass(P, Psm, cols, gdn_conv_w[layer], rate_row, dtb_row, n_ctx, False, 0)
        gdn = _gdn_pass(P, Psm, cols, gdn_conv_w[layer], rate_row, dtb_row, n_ctx, True, 1, of=of,
                        norm_g=gdn_norm_g[layer])

        xa = _merge(xa, da, lru, gdn, P, mod3, w_branch[layer].astype(BF16), w_out[layer].astype(BF16),
                    n_lat, ctx_row)
        if last:
            out = _mlp(xa, mod3, norm2_g[layer], mlp_w1[layer].astype(BF16), mlp_w2[layer].astype(BF16),
                       n_lat, ctx_row, T)
        else:
            xa = _mlp(xa, mod3, norm2_g[layer], mlp_w1[layer].astype(BF16), mlp_w2[layer].astype(BF16),
                      n_lat, ctx_row, Tt)
    return out
```

```python
import functools
import math

import jax
import jax.numpy as jnp
from jax import lax
from jax.experimental import pallas as pl
from jax.experimental.pallas import tpu as pltpu

F32 = jnp.float32
BF16 = jnp.bfloat16
HIGHEST = lax.Precision.HIGHEST

N_MOD = 6
BRANCH_WIDTH = 512
N_BRANCH = 3
DA_HEADS = 4
DA_QK_DIM = 64
DA_V_DIM = 128
ROPE_BASE = 10000.0
LRU_BLOCKS = 8
LRU_C = 8.0
GDN_HEADS = 4
GDN_DK = 128
GDN_DV = 128
GDN_CHUNK = 64
N_DIR = 2
GRID_W = 64
EPS = 1e-6
NEG_BIG = -1e30

ROW_TILE = 256
HALO_ROWS = 8
VT_PAD = 16
GDN_HEAD_GROUPS = ((0, 1), (2, 3))
GDN_STAGE_SKEW = 2
VMEM_LIMIT = 48 * 1024 * 1024

COL_GATE = 0


def _cparams(sem):
    return pltpu.CompilerParams(dimension_semantics=sem, vmem_limit_bytes=VMEM_LIMIT)


def _softplus(z):
    return jnp.maximum(z, 0.0) + jnp.log1p(jnp.exp(-jnp.abs(z)))


def _adaln_kernel(c_ref, w_ref, b_ref, o_ref):
    c = c_ref[...]
    a = c * jax.nn.sigmoid(c)
    o_ref[...] = jnp.dot(a, w_ref[...], preferred_element_type=F32, precision=HIGHEST) + b_ref[...]


def _adaln(c_all, w, b):
    R, D = c_all.shape
    N = w.shape[1]
    tn = 1536 if N % 1536 == 0 else N
    return pl.pallas_call(
        _adaln_kernel,
        grid=(N // tn,),
        in_specs=[pl.BlockSpec((R, D), lambda j: (0, 0)),
                  pl.BlockSpec((D, tn), lambda j: (0, j)),
                  pl.BlockSpec((1, tn), lambda j: (0, j))],
        out_specs=pl.BlockSpec((R, tn), lambda j: (0, j)),
        out_shape=jax.ShapeDtypeStruct((R, N), F32),
        compiler_params=_cparams(("arbitrary",)),
        name="adaln",
    )(c_all, w, b.reshape(1, N))


def _norm_mod_kernel(x_ref, mod_ref, g_ref, o_ref, *, which):
    x = x_ref[0]
    ms = jnp.mean(x * x, axis=-1, keepdims=True)
    y = x * lax.rsqrt(ms + EPS) * g_ref[...]
    sh = mod_ref[0, 3 * which:3 * which + 1, :]
    sc = mod_ref[0, 3 * which + 1:3 * which + 2, :]
    o_ref[0] = (y * (1.0 + sc) + sh).astype(BF16)


def _norm_mod(xa, mod3, g, which, n_lat_tiles, ctx_row):
    B, Tt, D = xa.shape
    tm = ROW_TILE
    return pl.pallas_call(
        functools.partial(_norm_mod_kernel, which=which),
        grid=(B, Tt // tm),
        in_specs=[pl.BlockSpec((1, tm, D), lambda b, t: (b, t, 0)),
                  pl.BlockSpec((1, N_MOD, D), lambda b, t: (jnp.where(t >= n_lat_tiles, ctx_row, b), 0, 0)),
                  pl.BlockSpec((1, D), lambda b, t: (0, 0))],
        out_specs=pl.BlockSpec((1, tm, D), lambda b, t: (b, t, 0)),
        out_shape=jax.ShapeDtypeStruct((B, Tt, D), BF16),
        compiler_params=_cparams(("parallel", "parallel")),
        name="norm_mod",
    )(xa, mod3, g.reshape(1, D))


def _matmul_kernel(h_ref, w_ref, o_ref):
    o_ref[...] = jnp.dot(h_ref[...], w_ref[...], preferred_element_type=F32)


def _pick(n, cands):
    for c in cands:
        if n % c == 0:
            return c
    return n


def _matmul(h2, w, name):
    R, D = h2.shape
    N = w.shape[1]
    tm = _pick(R, (1024, 512, 256))
    tn = _pick(N, (1536, 1280, 1024, 512, 128))
    return pl.pallas_call(
        _matmul_kernel,
        grid=(N // tn, R // tm),
        in_specs=[pl.BlockSpec((tm, D), lambda j, i: (i, 0)),
                  pl.BlockSpec((D, tn), lambda j, i: (0, j))],
        out_specs=pl.BlockSpec((tm, tn), lambda j, i: (i, j)),
        out_shape=jax.ShapeDtypeStruct((R, N), F32),
        compiler_params=_cparams(("parallel", "parallel")),
        name=name,
    )(h2, w)


def _seg_mean(x2, s_ref):
    hi = x2.astype(BF16)
    lo = (x2 - hi.astype(F32)).astype(BF16)
    s = s_ref[...]
    return jnp.dot(hi, s, preferred_element_type=F32) + jnp.dot(lo, s, preferred_element_type=F32)


def _rope(y, cos, sin_signed, first_half):
    rot = jnp.where(first_half, pltpu.roll(y, 128 - 16, 1), pltpu.roll(y, 16, 1))
    return y * cos + rot * sin_signed


def _qkv_prep_kernel(q_ref, k_ref, v_ref, cos_ref, sin_ref, gq_ref, gk_ref, s_ref, qo_ref, ko_ref, vo_ref,
                     *, q_scale):
    cos = cos_ref[...]
    sin = sin_ref[...]
    tm = cos.shape[0]
    lane = lax.broadcasted_iota(jnp.int32, (tm, 128), 1)
    first_half = (lane % 32) < 16
    comp0 = lane < DA_QK_DIM

    q = q_ref[0]
    qn = q * lax.rsqrt(_seg_mean(q * q, s_ref) + EPS) * gq_ref[...]
    k = k_ref[0]
    kn = k * lax.rsqrt(_seg_mean(k * k, s_ref) + EPS) * gk_ref[...]
    v = v_ref[0]
    ones_row = jnp.where(lax.broadcasted_iota(jnp.int32, (VT_PAD, tm), 0) == 0, 1.0, 0.0).astype(BF16)
    for h in range(DA_HEADS):
        qh = _rope(qn[:, h * 128:(h + 1) * 128], cos, sin, first_half) * q_scale
        qo_ref[0, h * 128:(h + 1) * 128, :] = qh.T.astype(BF16)
        kh = _rope(kn[:, h * 128:(h + 1) * 128], cos, sin, first_half)
        ko_ref[0, :, h * 256:h * 256 + 128] = jnp.where(comp0, kh, 0.0).astype(BF16)
        ko_ref[0, :, h * 256 + 128:(h + 1) * 256] = jnp.where(comp0, 0.0, kh).astype(BF16)
        vo_ref[0, h, 0, 0:DA_V_DIM, :] = v[:, h * DA_V_DIM:(h + 1) * DA_V_DIM].T.astype(BF16)
        vo_ref[0, h, 0, DA_V_DIM:DA_V_DIM + VT_PAD, :] = ones_row


def _qkv_prep(P, cols, cos_t, sin_t, gq, gk):
    B, Tt, _ = P.shape
    tm = ROW_TILE
    W = BRANCH_WIDTH
    seg = (jnp.arange(W)[:, None] // DA_QK_DIM == jnp.arange(W)[None, :] // DA_QK_DIM)
    smat = (seg.astype(F32) / DA_QK_DIM).astype(BF16)
    gq_t = jnp.tile(gq, W // DA_QK_DIM).reshape(1, W)
    gk_t = jnp.tile(gk, W // DA_QK_DIM).reshape(1, W)
    qb, kb, vb = cols["q"] // W, cols["k"] // W, cols["v"] // W
    q_scale = (DA_QK_DIM ** -0.5) * math.log2(math.e)
    return pl.pallas_call(
        functools.partial(_qkv_prep_kernel, q_scale=q_scale),
        grid=(B, Tt // tm),
        in_specs=[pl.BlockSpec((1, tm, W), lambda b, t: (b, t, qb)),
                  pl.BlockSpec((1, tm, W), lambda b, t: (b, t, kb)),
                  pl.BlockSpec((1, tm, W), lambda b, t: (b, t, vb)),
                  pl.BlockSpec((tm, 128), lambda b, t: (t, 0)),
                  pl.BlockSpec((tm, 128), lambda b, t: (t, 0)),
                  pl.BlockSpec((1, W), lambda b, t: (0, 0)),
                  pl.BlockSpec((1, W), lambda b, t: (0, 0)),
                  pl.BlockSpec((W, W), lambda b, t: (0, 0))],
        out_specs=[pl.BlockSpec((1, W, tm), lambda b, t: (b, 0, t)),
                   pl.BlockSpec((1, tm, 2 * W), lambda b, t: (b, t, 0)),
                   pl.BlockSpec((1, DA_HEADS, 1, DA_V_DIM + VT_PAD, tm), lambda b, t: (b, 0, t, 0, 0))],
        out_shape=[jax.ShapeDtypeStruct((B, W, Tt), BF16),
                   jax.ShapeDtypeStruct((B, Tt, 2 * W), BF16),
                   jax.ShapeDtypeStruct((B, DA_HEADS, Tt // tm, DA_V_DIM + VT_PAD, tm), BF16)],
        compiler_params=_cparams(("parallel", "parallel")),
        name="qkv_prep",
    )(P, P, P, cos_t, sin_t, gq_t, gk_t, smat)


def _flash_kernel(qt_ref, k_ref, vt_ref, lam_ref, subg_ref, o_ref, acc_s,
                  *, tq, sub, n_sub, nk, chunk0, lam_init, unroll):
    qt = qt_ref[0]
    acc_s[...] = jnp.zeros(acc_s.shape, F32)
    tk = sub * n_sub

    def scores(j):
        r0 = (chunk0 + j * n_sub) * sub
        kk = k_ref[0, r0:r0 + tk, :]
        k2 = jnp.concatenate([kk[:, :128], kk[:, 128:]], axis=0)
        return jnp.dot(k2, qt, preferred_element_type=F32)

    def accumulate(pend):
        j, ps, alphas = pend
        c0 = chunk0 + j * n_sub
        vt = jnp.concatenate([vt_ref[0, 0, c0 + i] for i in range(n_sub)], axis=1)
        for c in range(2):
            acc_s[c] = alphas[c] * acc_s[c] + jnp.dot(vt, ps[c], preferred_element_type=F32)

    ms = [jnp.full((1, tq), -jnp.inf, F32)] * 2
    st = scores(0)
    pend = None
    for j in range(nk):
        st_next = scores(j + 1) if j + 1 < nk else None
        if pend is not None:
            accumulate(pend)
        ps, alphas = [], []
        for c in range(2):
            s_c = st[c * tk:(c + 1) * tk]
            m_new = jnp.maximum(ms[c], jnp.max(s_c, axis=0, keepdims=True))
            alphas.append(jnp.exp2(ms[c] - m_new))
            ps.append(jnp.exp2(s_c - m_new).astype(BF16))
            ms[c] = m_new
        pend = (j, ps, alphas)
        st = st_next
    accumulate(pend)

    a0 = acc_s[0]
    a1 = acc_s[1]
    o0 = a0[:DA_V_DIM] / a0[DA_V_DIM:DA_V_DIM + 1]
    o1 = a1[:DA_V_DIM] / a1[DA_V_DIM:DA_V_DIM + 1]
    lv = lam_ref[...]
    s01 = jnp.sum(lv[0:1] * lv[1:2], axis=-1, keepdims=True)
    s23 = jnp.sum(lv[2:3] * lv[3:4], axis=-1, keepdims=True)
    lam = jnp.exp(s01) - jnp.exp(s23) + lam_init
    d = o0 - lam * o1
    ms_ = jnp.mean(d * d, axis=0, keepdims=True)
    o_ref[0] = (d * lax.rsqrt(ms_ + EPS)).T * subg_ref[...] * (1.0 - lam_init)


def _flash(QT, Ks, VT, lam_vec, sub_g, lam_init, q_row0, n_q_rows, k_row0, n_k_rows, tq, n_sub):
    B, _, Tt = QT.shape
    sub = VT.shape[-1]
    tk = sub * n_sub
    nq, nk = n_q_rows // tq, n_k_rows // tk
    q0 = q_row0 // tq
    vrows = VT.shape[3]
    return pl.pallas_call(
        functools.partial(_flash_kernel, tq=tq, sub=sub, n_sub=n_sub, nk=nk, chunk0=k_row0 // sub,
                          lam_init=lam_init, unroll=True),
        grid=(B, DA_HEADS, nq),
        in_specs=[pl.BlockSpec((1, 128, tq), lambda b, h, i: (b, h, q0 + i)),
                  pl.BlockSpec((1, Tt, 256), lambda b, h, i: (b, 0, h)),
                  pl.BlockSpec((1, 1, VT.shape[2], vrows, sub), lambda b, h, i: (b, h, 0, 0, 0)),
                  pl.BlockSpec((4, DA_QK_DIM), lambda b, h, i: (0, 0)),
                  pl.BlockSpec((1, DA_V_DIM), lambda b, h, i: (0, 0))],
        out_specs=pl.BlockSpec((1, tq, 128), lambda b, h, i: (b, i, h)),
        out_shape=jax.ShapeDtypeStruct((B, n_q_rows, BRANCH_WIDTH), F32),
        scratch_shapes=[pltpu.VMEM((2, vrows, tq), F32)],
        compiler_params=_cparams(("parallel", "parallel", "arbitrary")),
        name="diff_attn",
    )(QT, Ks, VT, lam_vec, sub_g.reshape(1, DA_V_DIM))


def _scan_tile(j, n_ctx, nt, reverse):
    n_lat = nt - n_ctx
    if not reverse:
        return jnp.where(j < n_ctx, n_lat + j, j - n_ctx)
    return jnp.where(j < n_ctx, nt - 1 - j, n_lat - 1 - (j - n_ctx))


def _stream_edges(t, n_ctx, nt):
    n_lat = nt - n_ctx
    return jnp.logical_or(t == 0, t == n_lat), jnp.logical_or(t == n_lat - 1, t == nt - 1)


def _dwconv4(x, prev8, next8, w_ref, is_first, is_last):
    tm = x.shape[0]
    row = lax.broadcasted_iota(jnp.int32, x.shape, 0)
    pf = jnp.where(is_first, 0.0, 1.0)
    nf = jnp.where(is_last, 0.0, 1.0)
    prow = prev8[HALO_ROWS - 1:HALO_ROWS, :] * pf
    n0 = next8[0:1, :] * nf
    n1 = next8[1:2, :] * nf
    xm1 = jnp.where(row == 0, prow, pltpu.roll(x, 1, 0))
    xp1 = jnp.where(row == tm - 1, n0, pltpu.roll(x, tm - 1, 0))
    xp2 = jnp.where(row == tm - 2, n0, jnp.where(row == tm - 1, n1, pltpu.roll(x, tm - 2, 0)))
    return w_ref[0:1, :] * xm1 + w_ref[1:2, :] * x + w_ref[2:3, :] * xp1 + w_ref[3:4, :] * xp2


def _halo_specs(tm, width, col_blk, n_ctx, nt, reverse):
    r8 = tm // HALO_ROWS
    last8 = nt * r8 - 1

    def cur(b, j):
        return (b, _scan_tile(j, n_ctx, nt, reverse), col_blk)

    def prev(b, j):
        return (b, jnp.maximum(_scan_tile(j, n_ctx, nt, reverse) * r8 - 1, 0), col_blk)

    def nxt(b, j):
        return (b, jnp.minimum((_scan_tile(j, n_ctx, nt, reverse) + 1) * r8, last8), col_blk)

    return [pl.BlockSpec((1, tm, width), cur),
            pl.BlockSpec((1, HALO_ROWS, width), prev),
            pl.BlockSpec((1, HALO_ROWS, width), nxt)]


def _linear_scan_tile(a, b, reverse):
    tm = a.shape[0]
    row = lax.broadcasted_iota(jnp.int32, a.shape, 0)
    s = 1
    while s < tm:
        if not reverse:
            keep = row >= s
            ap = jnp.where(keep, pltpu.roll(a, s, 0), 1.0)
            bp = jnp.where(keep, pltpu.roll(b, s, 0), 0.0)
        else:
            keep = row < tm - s
            ap = jnp.where(keep, pltpu.roll(a, tm - s, 0), 1.0)
            bp = jnp.where(keep, pltpu.roll(b, tm - s, 0), 0.0)
        b = a * bp + b
        a = a * ap
        s *= 2
    return a, b


def _lru_kernel(*refs, n_ctx, nt, reverse, final):
    if final:
        (x_ref, xp_ref, xn_ref, y_ref, hf_ref, cw_ref, cb_ref, wg_ref, gb_ref, lam_ref, o_ref, carry) = refs
    else:
        (x_ref, xp_ref, xn_ref, cw_ref, cb_ref, wg_ref, gb_ref, lam_ref, o_ref, carry) = refs
    j = pl.program_id(1)
    t = _scan_tile(j, n_ctx, nt, reverse)
    is_first, is_last = _stream_edges(t, n_ctx, nt)

    @pl.when(j == 0)
    def _():
        carry[...] = jnp.zeros(carry.shape, F32)

    x = x_ref[0]
    tm, W = x.shape
    xc = _dwconv4(x, xp_ref[0], xn_ref[0], cw_ref, is_first, is_last) + cb_ref[...]
    g = jnp.dot(xc.astype(BF16), wg_ref[...], preferred_element_type=F32) + gb_ref[...]
    r = jax.nn.sigmoid(g[:, :W])
    i = jax.nn.sigmoid(g[:, W:])
    log_a = (-LRU_C) * r * _softplus(-lam_ref[...])
    a = jnp.exp(log_a)
    th = jnp.tanh(log_a)
    bb = jnp.sqrt(-2.0 * th / (1.0 - th)) * (i * xc)
    a_c, b_c = _linear_scan_tile(a, bb, reverse)
    h = b_c + a_c * carry[0:1, :]
    carry[0:1, :] = h[0:1, :] if reverse else h[tm - 1:tm, :]
    if final:
        o_ref[0] = jax.nn.gelu(y_ref[0]) * (hf_ref[0] + h)
    else:
        o_ref[0] = h


def _lru_pass(P, cols, conv_w, conv_b, wg, gb, lam, n_ctx, reverse, hf=None):
    B, Tt, _ = P.shape
    tm = ROW_TILE
    W = BRANCH_WIDTH
    nt = Tt // tm
    final = hf is not None

    def cur(b, j):
        return (b, _scan_tile(j, n_ctx, nt, reverse), 0)

    in_specs = _halo_specs(tm, W, cols["lx"] // W, n_ctx, nt, reverse)
    args = [P, P, P]
    if final:
        yb = cols["ly"] // W
        in_specs += [pl.BlockSpec((1, tm, W), lambda b, j: (b, _scan_tile(j, n_ctx, nt, reverse), yb)),
                     pl.BlockSpec((1, tm, W), cur)]
        args += [P, hf]
    in_specs += [pl.BlockSpec((4, W), lambda b, j: (0, 0)),
                 pl.BlockSpec((1, W), lambda b, j: (0, 0)),
                 pl.BlockSpec((W, 2 * W), lambda b, j: (0, 0)),
                 pl.BlockSpec((1, 2 * W), lambda b, j: (0, 0)),
                 pl.BlockSpec((1, W), lambda b, j: (0, 0))]
    args += [conv_w, conv_b.reshape(1, W), wg, gb.reshape(1, 2 * W), lam.reshape(1, W)]
    return pl.pallas_call(
        functools.partial(_lru_kernel, n_ctx=n_ctx, nt=nt, reverse=reverse, final=final),
        grid=(B, nt),
        in_specs=in_specs,
        out_specs=pl.BlockSpec((1, tm, W), cur),
        out_shape=jax.ShapeDtypeStruct((B, Tt, W), F32),
        scratch_shapes=[pltpu.VMEM((HALO_ROWS, W), F32)],
        compiler_params=_cparams(("parallel", "arbitrary")),
        name="rglru_bwd" if reverse else "rglru_fwd",
    )(*args)


def _lru_gate_weights(gate_w_d):
    mats = []
    for g in range(2):
        mats.append(jax.scipy.linalg.block_diag(*[gate_w_d[g, n] for n in range(LRU_BLOCKS)]))
    return jnp.concatenate(mats, axis=1).astype(BF16)


def _split_bf16(x):
    hi = x.astype(BF16)
    lo = (x - hi.astype(F32)).astype(BF16)
    return hi, lo


def _dot3(a_hi, a_lo, b_hi, b_lo):
    return (jnp.dot(a_hi, b_hi, preferred_element_type=F32)
            + (jnp.dot(a_hi, b_lo, preferred_element_type=F32) + jnp.dot(a_lo, b_hi, preferred_element_type=F32)))


def _gdn_kernel(*refs, n_ctx, nt, reverse, final, d):
    if final:
        (x_ref, xp_ref, xn_ref, sm_ref, z_ref, of_ref, cw_ref, rate_ref, dtb_ref, ng_ref, o_ref, S_s, qkv_s) = refs
    else:
        (x_ref, xp_ref, xn_ref, sm_ref, cw_ref, rate_ref, dtb_ref, o_ref, S_s, qkv_s) = refs
    j = pl.program_id(1)
    t = _scan_tile(j, n_ctx, nt, reverse)
    is_first, is_last = _stream_edges(t, n_ctx, nt)

    @pl.when(j == 0)
    def _():
        S_s[...] = jnp.zeros(S_s.shape, F32)

    x = x_ref[0]
    tm = x.shape[0]
    C = GDN_CHUNK
    nC = tm // C
    HK = GDN_HEADS * GDN_DK
    qkv = _dwconv4(x, xp_ref[0], xn_ref[0], cw_ref, is_first, is_last)
    qkv_s[...] = qkv * jax.nn.sigmoid(qkv)

    sm = sm_ref[0]
    beta_all = jax.nn.sigmoid(sm)
    g_all = -jnp.exp(rate_ref[...]) * _softplus(sm + dtb_ref[...])

    ii = lax.broadcasted_iota(jnp.int32, (tm, tm), 0)
    jj = lax.broadcasted_iota(jnp.int32, (tm, tm), 1)
    same = (ii // C) == (jj // C)
    if not reverse:
        incl = jnp.logical_and(same, ii >= jj)
        strict = jnp.logical_and(same, ii > jj)
    else:
        incl = jnp.logical_and(same, ii <= jj)
        strict = jnp.logical_and(same, ii < jj)
    incl_b = incl.astype(F32).astype(BF16)
    eye = (ii == jj).astype(F32)

    g1 = g_all.astype(BF16)
    g2 = (g_all - g1.astype(F32)).astype(BF16)
    g3 = (g_all - g1.astype(F32) - g2.astype(F32)).astype(BF16)
    gc_all = (jnp.dot(incl_b, g1, preferred_element_type=F32)
              + (jnp.dot(incl_b, g2, preferred_element_type=F32) + jnp.dot(incl_b, g3, preferred_element_type=F32)))
    last = 0 if reverse else C - 1
    gtot_all = jnp.concatenate(
        [jnp.broadcast_to(gc_all[c * C + last:c * C + last + 1, :], (C, 128)) for c in range(nC)], axis=0)

    groups = [_gdn_group_stages(hs, d=d, tm=tm, reverse=reverse, final=final, qkv_s=qkv_s, S_s=S_s, o_ref=o_ref,
                                of_ref=of_ref if final else None, z_ref=z_ref if final else None,
                                ng_ref=ng_ref if final else None, beta_all=beta_all, gc_all=gc_all,
                                gtot_all=gtot_all, incl=incl, strict=strict, eye=eye)
              for hs in GDN_HEAD_GROUPS]
    _run_skewed(groups, GDN_STAGE_SKEW)


def _run_skewed(gens, skew):
    done = [False] * len(gens)
    rnd = 0
    while not all(done):
        for k, g in enumerate(gens):
            if done[k] or rnd < k * skew:
                continue
            try:
                next(g)
            except StopIteration:
                done[k] = True
        rnd += 1


def _gdn_group_stages(heads, *, d, tm, reverse, final, qkv_s, S_s, o_ref, of_ref, z_ref, ng_ref, beta_all, gc_all,
                      gtot_all, incl, strict, eye):
    C = GDN_CHUNK
    nC = tm // C
    HK = GDN_HEADS * GDN_DK
    rhs, q_dec, k_dec, g_end, m_k, attn = {}, {}, {}, {}, {}, {}
    for h in heads:
        q = qkv_s[:, h * GDN_DK:(h + 1) * GDN_DK]
        k = qkv_s[:, HK + h * GDN_DK:HK + (h + 1) * GDN_DK]
        v = qkv_s[:, 2 * HK + h * GDN_DV:2 * HK + (h + 1) * GDN_DV]
        qn = q * lax.rsqrt(jnp.sum(q * q, axis=-1, keepdims=True) + EPS) * (GDN_DK ** -0.5)
        kn = k * lax.rsqrt(jnp.sum(k * k, axis=-1, keepdims=True) + EPS)
        cb = d * GDN_HEADS + h
        cg = N_DIR * GDN_HEADS + d * GDN_HEADS + h
        beta_b = jnp.broadcast_to(beta_all[:, cb:cb + 1], (tm, 128))
        gc_b = jnp.broadcast_to(gc_all[:, cg:cg + 1], (tm, 128))
        gtot_b = jnp.broadcast_to(gtot_all[:, cg:cg + 1], (tm, 128))
        eg = jnp.exp(gc_b)
        kb = kn * beta_b
        rhs[h] = jnp.concatenate([v * beta_b, kb * eg], axis=1).astype(BF16)
        q_dec[h] = qn * eg
        k_dec[h] = kn * jnp.exp(gtot_b - gc_b)
        g_end[h] = jnp.exp(gtot_b)
        gc_cols = jnp.concatenate([gc_b] * (tm // 128), axis=1)
        gc_rows = gc_b.T[0:1, :]
        decay = jnp.exp(jnp.where(incl, gc_cols - gc_rows, NEG_BIG))
        kq = lax.dot_general(jnp.concatenate([kb, qn], axis=0).astype(BF16), kn.astype(BF16),
                             (((1,), (1,)), ((), ())), preferred_element_type=F32)
        m_k[h] = jnp.where(strict, -(kq[:tm] * decay), 0.0)
        attn[h] = (kq[tm:] * decay).astype(BF16)
    yield

    t_inv = {h: eye + m_k[h] for h in heads}
    p_k = {}
    for h in heads:
        m_hi, m_lo = _split_bf16(m_k[h])
        p_k[h] = _dot3(m_hi, m_lo, m_hi, m_lo)
    yield
    for _ in range(4):
        for h in heads:
            p_hi, p_lo = _split_bf16(p_k[h])
            t_hi, t_lo = _split_bf16(t_inv[h])
            r2 = _dot3(jnp.concatenate([t_hi, p_hi], axis=0), jnp.concatenate([t_lo, p_lo], axis=0), p_hi, p_lo)
            t_inv[h] = t_inv[h] + r2[:tm]
            p_k[h] = r2[tm:]
        yield
    u, w = {}, {}
    for h in heads:
        p_hi, p_lo = _split_bf16(p_k[h])
        t_hi, t_lo = _split_bf16(t_inv[h])
        t_fin = t_inv[h] + _dot3(t_hi, t_lo, p_hi, p_lo)
        sol = jnp.dot(t_fin.astype(BF16), rhs[h], preferred_element_type=F32)
        u[h] = sol[:, :GDN_DV]
        w[h] = sol[:, GDN_DV:]
    yield

    S = {h: S_s[h] for h in heads}
    for cc in range(nC):
        c = nC - 1 - cc if reverse else cc
        r0 = c * C
        for h in heads:
            wq = jnp.concatenate([w[h][r0:r0 + C], q_dec[h][r0:r0 + C]], axis=0).astype(BF16)
            ws = jnp.dot(wq, S[h].astype(BF16), preferred_element_type=F32)
            v_new = (u[h][r0:r0 + C] - ws[:C]).astype(BF16)
            o_c = ws[C:] + jnp.dot(attn[h][r0:r0 + C, r0:r0 + C], v_new, preferred_element_type=F32)
            S[h] = S[h] * _col(g_end[h][r0:r0 + 1, :], GDN_DK) + lax.dot_general(
                k_dec[h][r0:r0 + C].astype(BF16), v_new, (((0,), (0,)), ((), ())),
                preferred_element_type=F32)
            if final:
                o_t = of_ref[0, r0:r0 + C, h * GDN_DV:(h + 1) * GDN_DV] + o_c
                ms = jnp.mean(o_t * o_t, axis=-1, keepdims=True)
                z = z_ref[0, r0:r0 + C, h * GDN_DV:(h + 1) * GDN_DV]
                o_ref[0, r0:r0 + C, h * GDN_DV:(h + 1) * GDN_DV] = (
                    o_t * lax.rsqrt(ms + EPS) * ng_ref[...] * (z * jax.nn.sigmoid(z)))
            else:
                o_ref[0, r0:r0 + C, h * GDN_DV:(h + 1) * GDN_DV] = o_c
        yield
    for h in heads:
        S_s[h] = S[h]


def _col(row_vec, n):
    return jnp.broadcast_to(row_vec, (n, row_vec.shape[1]))


def _gdn_pass(P, Psm, cols, conv_w, rate_row, dtb_row, n_ctx, reverse, d, of=None, norm_g=None):
    B, Tt, _ = P.shape
    tm = ROW_TILE
    W = BRANCH_WIDTH
    QW = GDN_HEADS * (2 * GDN_DK + GDN_DV)
    nt = Tt // tm
    final = of is not None

    def cur(b, j):
        return (b, _scan_tile(j, n_ctx, nt, reverse), 0)

    in_specs = _halo_specs(tm, QW, cols["gqkv"] // QW, n_ctx, nt, reverse)
    in_specs += [pl.BlockSpec((1, tm, 128), cur)]
    args = [P, P, P, Psm]
    if final:
        zb = cols["gz"] // W
        in_specs += [pl.BlockSpec((1, tm, W), lambda b, j: (b, _scan_tile(j, n_ctx, nt, reverse), zb)),
                     pl.BlockSpec((1, tm, W), cur)]
        args += [P, of]
    in_specs += [pl.BlockSpec((4, QW), lambda b, j: (0, 0)),
                 pl.BlockSpec((1, 128), lambda b, j: (0, 0)),
                 pl.BlockSpec((1, 128), lambda b, j: (0, 0))]
    args += [conv_w, rate_row, dtb_row]
    if final:
        in_specs += [pl.BlockSpec((1, GDN_DV), lambda b, j: (0, 0))]
        args += [norm_g.reshape(1, GDN_DV)]
    return pl.pallas_call(
        functools.partial(_gdn_kernel, n_ctx=n_ctx, nt=nt, reverse=reverse, final=final, d=d),
        grid=(B, nt),
        in_specs=in_specs,
        out_specs=pl.BlockSpec((1, tm, W), cur),
        out_shape=jax.ShapeDtypeStruct((B, Tt, W), F32),
        scratch_shapes=[pltpu.VMEM((GDN_HEADS, GDN_DK, GDN_DV), F32), pltpu.VMEM((tm, QW), F32)],
        compiler_params=_cparams(("parallel", "arbitrary")),
        name="gdn_bwd" if reverse else "gdn_fwd",
    )(*args)


def _merge_kernel(x_ref, da_ref, lru_ref, gdn_ref, g0_ref, g1_ref, g2_ref, mod_ref, wb_ref, wo_ref, o_ref):
    outs = (da_ref, lru_ref, gdn_ref)
    gates = (g0_ref, g1_ref, g2_ref)
    merged = None
    for i in range(N_BRANCH):
        y = jnp.dot(outs[i][0].astype(BF16), wb_ref[i], preferred_element_type=F32)
        term = jax.nn.sigmoid(gates[i][0]) * y
        merged = term if merged is None else merged + term
    proj = jnp.dot(merged.astype(BF16), wo_ref[...], preferred_element_type=F32)
    o_ref[0] = x_ref[0] + mod_ref[0, 2:3, :] * proj


def _merge(xa, da, lru, gdn, P, mod3, wb, wo, n_lat_tiles, ctx_row):
    B, Tt, D = xa.shape
    tm = ROW_TILE
    W = BRANCH_WIDTH
    row = lambda b, t: (b, t, 0)
    return pl.pallas_call(
        _merge_kernel,
        grid=(B, Tt // tm),
        in_specs=[pl.BlockSpec((1, tm, D), row),
                  pl.BlockSpec((1, tm, W), row), pl.BlockSpec((1, tm, W), row), pl.BlockSpec((1, tm, W), row),
                  pl.BlockSpec((1, tm, D), lambda b, t: (b, t, 0)),
                  pl.BlockSpec((1, tm, D), lambda b, t: (b, t, 1)),
                  pl.BlockSpec((1, tm, D), lambda b, t: (b, t, 2)),
                  pl.BlockSpec((1, N_MOD, D), lambda b, t: (jnp.where(t >= n_lat_tiles, ctx_row, b), 0, 0)),
                  pl.BlockSpec((N_BRANCH, W, D), lambda b, t: (0, 0, 0)),
                  pl.BlockSpec((D, D), lambda b, t: (0, 0))],
        out_specs=pl.BlockSpec((1, tm, D), row),
        out_shape=jax.ShapeDtypeStruct((B, Tt, D), F32),
        compiler_params=_cparams(("parallel", "parallel")),
        name="merge",
    )(xa, da, lru, gdn, P, P, P, mod3, wb, wo)


def _mlp_kernel(x_ref, mod_ref, g_ref, w1_ref, w2_ref, o_ref, *, ff_chunk):
    x = x_ref[0]
    ms = jnp.mean(x * x, axis=-1, keepdims=True)
    y = x * lax.rsqrt(ms + EPS) * g_ref[...]
    h = (y * (1.0 + mod_ref[0, 4:5, :]) + mod_ref[0, 3:4, :]).astype(BF16)
    n_ff = w1_ref.shape[1]
    acc = None
    for c0 in range(0, n_ff, ff_chunk):
        a = jnp.dot(h, w1_ref[:, c0:c0 + ff_chunk], preferred_element_type=F32)
        a = jnp.square(jnp.maximum(a, 0.0)).astype(BF16)
        part = jnp.dot(a, w2_ref[c0:c0 + ff_chunk, :], preferred_element_type=F32)
        acc = part if acc is None else acc + part
    o_ref[0] = x + mod_ref[0, 5:6, :] * acc


def _mlp(xa, mod3, g, w1, w2, n_lat_tiles, ctx_row, n_rows):
    B, Tt, D = xa.shape
    tm = ROW_TILE
    F = w1.shape[1]
    return pl.pallas_call(
        functools.partial(_mlp_kernel, ff_chunk=min(F, 1024)),
        grid=(B, n_rows // tm),
        in_specs=[pl.BlockSpec((1, tm, D), lambda b, t: (b, t, 0)),
                  pl.BlockSpec((1, N_MOD, D), lambda b, t: (jnp.where(t >= n_lat_tiles, ctx_row, b), 0, 0)),
                  pl.BlockSpec((1, D), lambda b, t: (0, 0)),
                  pl.BlockSpec((D, F), lambda b, t: (0, 0)),
                  pl.BlockSpec((F, D), lambda b, t: (0, 0))],
        out_specs=pl.BlockSpec((1, tm, D), lambda b, t: (b, t, 0)),
        out_shape=jax.ShapeDtypeStruct((B, n_rows, D), F32),
        compiler_params=_cparams(("parallel", "parallel")),
        name="mlp",
    )(xa, mod3, g.reshape(1, D), w1, w2)


def _proj_layout(D):
    W = BRANCH_WIDTH
    QW = GDN_HEADS * (2 * GDN_DK + GDN_DV)
    src = {}
    off = 0
    for name, width in (("q", W), ("k", W), ("v", W), ("lx", W), ("ly", W), ("gqkv", QW), ("gz", W),
                        ("gb", N_DIR * GDN_HEADS), ("ga", N_DIR * GDN_HEADS), ("gate", N_BRANCH * D)):
        src[name] = (off, width)
        off += width
    order = ("gate", "gqkv", "q", "k", "v", "lx", "ly", "gz")
    cols = {}
    idx = []
    o = 0
    for name in order:
        s, w = src[name]
        cols[name] = o
        idx.append(jnp.arange(s, s + w))
        o += w
    small_idx = jnp.concatenate([jnp.arange(src["gb"][0], src["gb"][0] + src["gb"][1]),
                                 jnp.arange(src["ga"][0], src["ga"][0] + src["ga"][1])])
    return cols, jnp.concatenate(idx), small_idx


def _rope_tables(n_ctx_rows, n_lat_rows):
    n_freq = DA_QK_DIM // 4
    inv = ROPE_BASE ** (-jnp.arange(n_freq, dtype=F32) / n_freq)
    tpos = jnp.arange(n_lat_rows, dtype=jnp.int32)
    ang_r = (tpos // GRID_W).astype(F32)[:, None] * inv
    ang_c = (tpos % GRID_W).astype(F32)[:, None] * inv
    ang = jnp.concatenate([ang_r, ang_r, ang_c, ang_c], axis=-1)
    ang = jnp.concatenate([ang, ang], axis=-1)
    sign = jnp.where((jnp.arange(128) % 32) < 16, -1.0, 1.0).astype(F32)
    cos = jnp.concatenate([jnp.cos(ang), jnp.ones((n_ctx_rows, 128), F32)], axis=0)
    sin = jnp.concatenate([jnp.sin(ang) * sign, jnp.zeros((n_ctx_rows, 128), F32)], axis=0)
    return cos, sin


def kernel(x, c, ctx, c_ctx, ada_w, ada_b, norm1_g, norm2_g, w_in, da_q_norm_g, da_k_norm_g, da_lambda, da_sub_norm_g, lru_conv_w, lru_conv_b, lru_gate_w, lru_gate_b, lru_lambda, gdn_conv_w, gdn_A_log, gdn_dt_bias, gdn_norm_g, w_branch, w_out, mlp_w1, mlp_w2):
    B, T, D = x.shape
    Tc = ctx.shape[1]
    depth = ada_w.shape[0]
    tm = ROW_TILE
    assert Tc % tm == 0 and T % tm == 0 and B + 1 <= 8
    n_ctx = Tc // tm
    n_lat = T // tm
    Tt = Tc + T
    ctx_row = B

    cols, main_idx, small_idx = _proj_layout(D)
    cos_t, sin_t = _rope_tables(Tc, T)
    xa = jnp.concatenate([x, ctx], axis=1)
    c_all = jnp.concatenate([c, c_ctx[None, :], jnp.zeros((8 - B - 1, D), F32)], axis=0)

    tq = _pick(T, (512, 256))
    n_sub = _pick(Tt // tm, (3, 2, 1))

    out = None
    for layer in range(depth):
        last = layer == depth - 1
        lam_init = 0.8 - 0.6 * math.exp(-0.3 * layer)
        w_main = w_in[layer][:, main_idx].astype(BF16)
        w_small = jnp.pad(w_in[layer][:, small_idx], ((0, 0), (0, 128 - small_idx.shape[0]))).astype(BF16)

        mod3 = _adaln(c_all, ada_w[layer], ada_b[layer]).reshape(8, N_MOD, D)
        h1 = _norm_mod(xa, mod3, norm1_g[layer], 0, n_lat, ctx_row).reshape(B * Tt, D)
        P = _matmul(h1, w_main, "in_proj").reshape(B, Tt, -1)
        Psm = _matmul(h1, w_small, "in_proj_small").reshape(B, Tt, 128)

        QT, Ks, VT = _qkv_prep(P, cols, cos_t, sin_t, da_q_norm_g[layer], da_k_norm_g[layer])
        da_l = _flash(QT, Ks, VT, da_lambda[layer], da_sub_norm_g[layer], lam_init, 0, T, 0, Tt, tq, n_sub)
        if last:
            da = jnp.pad(da_l, ((0, 0), (0, Tc), (0, 0)))
        else:
            da_c = _flash(QT, Ks, VT, da_lambda[layer], da_sub_norm_g[layer], lam_init, T, Tc, T, Tc, tm, 1)
            da = jnp.concatenate([da_l, da_c], axis=1)

        hf = _lru_pass(P, cols, lru_conv_w[layer], lru_conv_b[layer], _lru_gate_weights(lru_gate_w[layer, 0]),
                       lru_gate_b[layer, 0].reshape(-1), lru_lambda[layer, 0], n_ctx, False)
        lru = _lru_pass(P, cols, lru_conv_w[layer], lru_conv_b[layer], _lru_gate_weights(lru_gate_w[layer, 1]),
                        lru_gate_b[layer, 1].reshape(-1), lru_lambda[layer, 1], n_ctx, True, hf=hf)

        rate_row = jnp.zeros((1, 128), F32).at[0, N_DIR * GDN_HEADS:2 * N_DIR * GDN_HEADS].set(
            gdn_A_log[layer].astype(F32).reshape(-1))
        dtb_row = jnp.zeros((1, 128), F32).at[0, N_DIR * GDN_HEADS:2 * N_DIR * GDN_HEADS].set(
            gdn_dt_bias[layer].astype(F32).reshape(-1))
        of = _gdn_pass(P, Psm, cols, gdn_conv_w[layer], rate_row, dtb_row, n_ctx, False, 0)
        gdn = _gdn_pass(P, Psm, cols, gdn_conv_w[layer], rate_row, dtb_row, n_ctx, True, 1, of=of,
                        norm_g=gdn_norm_g[layer])

        xa = _merge(xa, da, lru, gdn, P, mod3, w_branch[layer].astype(BF16), w_out[layer].astype(BF16),
                    n_lat, ctx_row)
        if last:
            out = _mlp(xa, mod3, norm2_g[layer], mlp_w1[layer].astype(BF16), mlp_w2[layer].astype(BF16),
                       n_lat, ctx_row, T)
        else:
            xa = _mlp(xa, mod3, norm2_g[layer], mlp_w1[layer].astype(BF16), mlp_w2[layer].astype(BF16),
                      n_lat, ctx_row, Tt)
    return out
```

```python
import functools
import math

import jax
import jax.numpy as jnp
from jax import lax
from jax.experimental import pallas as pl
from jax.experimental.pallas import tpu as pltpu

F32 = jnp.float32
BF16 = jnp.bfloat16
HIGHEST = lax.Precision.HIGHEST

N_MOD = 6
BRANCH_WIDTH = 512
N_BRANCH = 3
DA_HEADS = 4
DA_QK_DIM = 64
DA_V_DIM = 128
ROPE_BASE = 10000.0
LRU_BLOCKS = 8
LRU_C = 8.0
GDN_HEADS = 4
GDN_DK = 128
GDN_DV = 128
GDN_CHUNK = 64
N_DIR = 2
GRID_W = 64
EPS = 1e-6
NEG_BIG = -1e30

ROW_TILE = 256
HALO_ROWS = 8
VT_PAD = 16
GDN_HEAD_GROUPS = ((0, 1), (2, 3))
GDN_STAGE_SKEW = 2
VMEM_LIMIT = 48 * 1024 * 1024

COL_GATE = 0


def _cparams(sem):
    return pltpu.CompilerParams(dimension_semantics=sem, vmem_limit_bytes=VMEM_LIMIT)


def _softplus(z):
    return jnp.maximum(z, 0.0) + jnp.log1p(jnp.exp(-jnp.abs(z)))


def _adaln_kernel(c_ref, w_ref, b_ref, o_ref):
    c = c_ref[...]
    a = c * jax.nn.sigmoid(c)
    o_ref[...] = jnp.dot(a, w_ref[...], preferred_element_type=F32, precision=HIGHEST) + b_ref[...]


def _adaln(c_all, w, b):
    R, D = c_all.shape
    N = w.shape[1]
    tn = 1536 if N % 1536 == 0 else N
    return pl.pallas_call(
        _adaln_kernel,
        grid=(N // tn,),
        in_specs=[pl.BlockSpec((R, D), lambda j: (0, 0)),
                  pl.BlockSpec((D, tn), lambda j: (0, j)),
                  pl.BlockSpec((1, tn), lambda j: (0, j))],
        out_specs=pl.BlockSpec((R, tn), lambda j: (0, j)),
        out_shape=jax.ShapeDtypeStruct((R, N), F32),
        compiler_params=_cparams(("arbitrary",)),
        name="adaln",
    )(c_all, w, b.reshape(1, N))


def _norm_mod_kernel(x_ref, mod_ref, g_ref, o_ref, *, which):
    x = x_ref[0]
    ms = jnp.mean(x * x, axis=-1, keepdims=True)
    y = x * lax.rsqrt(ms + EPS) * g_ref[...]
    sh = mod_ref[0, 3 * which:3 * which + 1, :]
    sc = mod_ref[0, 3 * which + 1:3 * which + 2, :]
    o_ref[0] = (y * (1.0 + sc) + sh).astype(BF16)


def _norm_mod(xa, mod3, g, which, n_lat_tiles, ctx_row):
    B, Tt, D = xa.shape
    tm = ROW_TILE
    return pl.pallas_call(
        functools.partial(_norm_mod_kernel, which=which),
        grid=(B, Tt // tm),
        in_specs=[pl.BlockSpec((1, tm, D), lambda b, t: (b, t, 0)),
                  pl.BlockSpec((1, N_MOD, D), lambda b, t: (jnp.where(t >= n_lat_tiles, ctx_row, b), 0, 0)),
                  pl.BlockSpec((1, D), lambda b, t: (0, 0))],
        out_specs=pl.BlockSpec((1, tm, D), lambda b, t: (b, t, 0)),
        out_shape=jax.ShapeDtypeStruct((B, Tt, D), BF16),
        compiler_params=_cparams(("parallel", "parallel")),
        name="norm_mod",
    )(xa, mod3, g.reshape(1, D))


def _matmul_kernel(h_ref, w_ref, o_ref):
    o_ref[...] = jnp.dot(h_ref[...], w_ref[...], preferred_element_type=F32)


def _pick(n, cands):
    for c in cands:
        if n % c == 0:
            return c
    return n


def _matmul(h2, w, name):
    R, D = h2.shape
    N = w.shape[1]
    tm = _pick(R, (1024, 512, 256))
    tn = _pick(N, (1536, 1280, 1024, 512, 128))
    return pl.pallas_call(
        _matmul_kernel,
        grid=(N // tn, R // tm),
        in_specs=[pl.BlockSpec((tm, D), lambda j, i: (i, 0)),
                  pl.BlockSpec((D, tn), lambda j, i: (0, j))],
        out_specs=pl.BlockSpec((tm, tn), lambda j, i: (i, j)),
        out_shape=jax.ShapeDtypeStruct((R, N), F32),
        compiler_params=_cparams(("parallel", "parallel")),
        name=name,
    )(h2, w)


def _seg_mean(x2, s_ref):
    hi = x2.astype(BF16)
    lo = (x2 - hi.astype(F32)).astype(BF16)
    s = s_ref[...]
    return jnp.dot(hi, s, preferred_element_type=F32) + jnp.dot(lo, s, preferred_element_type=F32)


def _rope(y, cos, sin_signed, first_half):
    rot = jnp.where(first_half, pltpu.roll(y, 128 - 16, 1), pltpu.roll(y, 16, 1))
    return y * cos + rot * sin_signed


def _qkv_prep_kernel(q_ref, k_ref, v_ref, cos_ref, sin_ref, gq_ref, gk_ref, s_ref, qo_ref, ko_ref, vo_ref,
                     *, q_scale):
    cos = cos_ref[...]
    sin = sin_ref[...]
    tm = cos.shape[0]
    lane = lax.broadcasted_iota(jnp.int32, (tm, 128), 1)
    first_half = (lane % 32) < 16
    comp0 = lane < DA_QK_DIM

    q = q_ref[0]
    qn = q * lax.rsqrt(_seg_mean(q * q, s_ref) + EPS) * gq_ref[...]
    k = k_ref[0]
    kn = k * lax.rsqrt(_seg_mean(k * k, s_ref) + EPS) * gk_ref[...]
    v = v_ref[0]
    ones_row = jnp.where(lax.broadcasted_iota(jnp.int32, (VT_PAD, tm), 0) == 0, 1.0, 0.0).astype(BF16)
    for h in range(DA_HEADS):
        qh = _rope(qn[:, h * 128:(h + 1) * 128], cos, sin, first_half) * q_scale
        qo_ref[0, h * 128:(h + 1) * 128, :] = qh.T.astype(BF16)
        kh = _rope(kn[:, h * 128:(h + 1) * 128], cos, sin, first_half)
        ko_ref[0, :, h * 256:h * 256 + 128] = jnp.where(comp0, kh, 0.0).astype(BF16)
        ko_ref[0, :, h * 256 + 128:(h + 1) * 256] = jnp.where(comp0, 0.0, kh).astype(BF16)
        vo_ref[0, h, 0, 0:DA_V_DIM, :] = v[:, h * DA_V_DIM:(h + 1) * DA_V_DIM].T.astype(BF16)
        vo_ref[0, h, 0, DA_V_DIM:DA_V_DIM + VT_PAD, :] = ones_row


def _qkv_prep(P, cols, cos_t, sin_t, gq, gk):
    B, Tt, _ = P.shape
    tm = ROW_TILE
    W = BRANCH_WIDTH
    seg = (jnp.arange(W)[:, None] // DA_QK_DIM == jnp.arange(W)[None, :] // DA_QK_DIM)
    smat = (seg.astype(F32) / DA_QK_DIM).astype(BF16)
    gq_t = jnp.tile(gq, W // DA_QK_DIM).reshape(1, W)
    gk_t = jnp.tile(gk, W // DA_QK_DIM).reshape(1, W)
    qb, kb, vb = cols["q"] // W, cols["k"] // W, cols["v"] // W
    q_scale = (DA_QK_DIM ** -0.5) * math.log2(math.e)
    return pl.pallas_call(
        functools.partial(_qkv_prep_kernel, q_scale=q_scale),
        grid=(B, Tt // tm),
        in_specs=[pl.BlockSpec((1, tm, W), lambda b, t: (b, t, qb)),
                  pl.BlockSpec((1, tm, W), lambda b, t: (b, t, kb)),
                  pl.BlockSpec((1, tm, W), lambda b, t: (b, t, vb)),
                  pl.BlockSpec((tm, 128), lambda b, t: (t, 0)),
                  pl.BlockSpec((tm, 128), lambda b, t: (t, 0)),
                  pl.BlockSpec((1, W), lambda b, t: (0, 0)),
                  pl.BlockSpec((1, W), lambda b, t: (0, 0)),
                  pl.BlockSpec((W, W), lambda b, t: (0, 0))],
        out_specs=[pl.BlockSpec((1, W, tm), lambda b, t: (b, 0, t)),
                   pl.BlockSpec((1, tm, 2 * W), lambda b, t: (b, t, 0)),
                   pl.BlockSpec((1, DA_HEADS, 1, DA_V_DIM + VT_PAD, tm), lambda b, t: (b, 0, t, 0, 0))],
        out_shape=[jax.ShapeDtypeStruct((B, W, Tt), BF16),
                   jax.ShapeDtypeStruct((B, Tt, 2 * W), BF16),
                   jax.ShapeDtypeStruct((B, DA_HEADS, Tt // tm, DA_V_DIM + VT_PAD, tm), BF16)],
        compiler_params=_cparams(("parallel", "parallel")),
        name="qkv_prep",
    )(P, P, P, cos_t, sin_t, gq_t, gk_t, smat)


def _flash_kernel(qt_ref, k_ref, vt_ref, lam_ref, subg_ref, o_ref, acc_s,
                  *, tq, sub, n_sub, nk, chunk0, lam_init, unroll):
    qt = qt_ref[0]
    acc_s[...] = jnp.zeros(acc_s.shape, F32)
    tk = sub * n_sub

    def scores(j):
        r0 = (chunk0 + j * n_sub) * sub
        kk = k_ref[0, r0:r0 + tk, :]
        k2 = jnp.concatenate([kk[:, :128], kk[:, 128:]], axis=0)
        return jnp.dot(k2, qt, preferred_element_type=F32)

    def accumulate(pend):
        j, ps, alphas = pend
        c0 = chunk0 + j * n_sub
        vt = jnp.concatenate([vt_ref[0, 0, c0 + i] for i in range(n_sub)], axis=1)
        for c in range(2):
            acc_s[c] = alphas[c] * acc_s[c] + jnp.dot(vt, ps[c], preferred_element_type=F32)

    ms = [jnp.full((1, tq), -jnp.inf, F32)] * 2
    st = scores(0)
    pend = None
    for j in range(nk):
        st_next = scores(j + 1) if j + 1 < nk else None
        if pend is not None:
            accumulate(pend)
        ps, alphas = [], []
        for c in range(2):
            s_c = st[c * tk:(c + 1) * tk]
            m_new = jnp.maximum(ms[c], jnp.max(s_c, axis=0, keepdims=True))
            alphas.append(jnp.exp2(ms[c] - m_new))
            ps.append(jnp.exp2(s_c - m_new).astype(BF16))
            ms[c] = m_new
        pend = (j, ps, alphas)
        st = st_next
    accumulate(pend)

    a0 = acc_s[0]
    a1 = acc_s[1]
    o0 = a0[:DA_V_DIM] / a0[DA_V_DIM:DA_V_DIM + 1]
    o1 = a1[:DA_V_DIM] / a1[DA_V_DIM:DA_V_DIM + 1]
    lv = lam_ref[...]
    s01 = jnp.sum(lv[0:1] * lv[1:2], axis=-1, keepdims=True)
    s23 = jnp.sum(lv[2:3] * lv[3:4], axis=-1, keepdims=True)
    lam = jnp.exp(s01) - jnp.exp(s23) + lam_init
    d = o0 - lam * o1
    ms_ = jnp.mean(d * d, axis=0, keepdims=True)
    o_ref[0] = (d * lax.rsqrt(ms_ + EPS)).T * subg_ref[...] * (1.0 - lam_init)


def _flash(QT, Ks, VT, lam_vec, sub_g, lam_init, q_row0, n_q_rows, k_row0, n_k_rows, tq, n_sub):
    B, _, Tt = QT.shape
    sub = VT.shape[-1]
    tk = sub * n_sub
    nq, nk = n_q_rows // tq, n_k_rows // tk
    q0 = q_row0 // tq
    vrows = VT.shape[3]
    return pl.pallas_call(
        functools.partial(_flash_kernel, tq=tq, sub=sub, n_sub=n_sub, nk=nk, chunk0=k_row0 // sub,
                          lam_init=lam_init, unroll=True),
        grid=(B, DA_HEADS, nq),
        in_specs=[pl.BlockSpec((1, 128, tq), lambda b, h, i: (b, h, q0 + i)),
                  pl.BlockSpec((1, Tt, 256), lambda b, h, i: (b, 0, h)),
                  pl.BlockSpec((1, 1, VT.shape[2], vrows, sub), lambda b, h, i: (b, h, 0, 0, 0)),
                  pl.BlockSpec((4, DA_QK_DIM), lambda b, h, i: (0, 0)),
                  pl.BlockSpec((1, DA_V_DIM), lambda b, h, i: (0, 0))],
        out_specs=pl.BlockSpec((1, tq, 128), lambda b, h, i: (b, i, h)),
        out_shape=jax.ShapeDtypeStruct((B, n_q_rows, BRANCH_WIDTH), F32),
        scratch_shapes=[pltpu.VMEM((2, vrows, tq), F32)],
        compiler_params=_cparams(("parallel", "parallel", "arbitrary")),
        name="diff_attn",
    )(QT, Ks, VT, lam_vec, sub_g.reshape(1, DA_V_DIM))


def _scan_tile(j, n_ctx, nt, reverse):
    n_lat = nt - n_ctx
    if not reverse:
        return jnp.where(j < n_ctx, n_lat + j, j - n_ctx)
    return jnp.where(j < n_ctx, nt - 1 - j, n_lat - 1 - (j - n_ctx))


def _stream_edges(t, n_ctx, nt):
    n_lat = nt - n_ctx
    return jnp.logical_or(t == 0, t == n_lat), jnp.logical_or(t == n_lat - 1, t == nt - 1)


def _dwconv4(x, prev8, next8, w_ref, is_first, is_last):
    tm = x.shape[0]
    row = lax.broadcasted_iota(jnp.int32, x.shape, 0)
    pf = jnp.where(is_first, 0.0, 1.0)
    nf = jnp.where(is_last, 0.0, 1.0)
    prow = prev8[HALO_ROWS - 1:HALO_ROWS, :] * pf
    n0 = next8[0:1, :] * nf
    n1 = next8[1:2, :] * nf
    xm1 = jnp.where(row == 0, prow, pltpu.roll(x, 1, 0))
    xp1 = jnp.where(row == tm - 1, n0, pltpu.roll(x, tm - 1, 0))
    xp2 = jnp.where(row == tm - 2, n0, jnp.where(row == tm - 1, n1, pltpu.roll(x, tm - 2, 0)))
    return w_ref[0:1, :] * xm1 + w_ref[1:2, :] * x + w_ref[2:3, :] * xp1 + w_ref[3:4, :] * xp2


def _halo_specs(tm, width, col_blk, n_ctx, nt, reverse):
    r8 = tm // HALO_ROWS
    last8 = nt * r8 - 1

    def cur(b, j):
        return (b, _scan_tile(j, n_ctx, nt, reverse), col_blk)

    def prev(b, j):
        return (b, jnp.maximum(_scan_tile(j, n_ctx, nt, reverse) * r8 - 1, 0), col_blk)

    def nxt(b, j):
        return (b, jnp.minimum((_scan_tile(j, n_ctx, nt, reverse) + 1) * r8, last8), col_blk)

    return [pl.BlockSpec((1, tm, width), cur),
            pl.BlockSpec((1, HALO_ROWS, width), prev),
            pl.BlockSpec((1, HALO_ROWS, width), nxt)]


def _linear_scan_tile(a, b, h0, reverse):
    tm = a.shape[0]
    G = HALO_ROWS
    sub = lax.broadcasted_iota(jnp.int32, a.shape, 0) % G
    s = 1
    while s < G:
        if not reverse:
            keep = sub >= s
            ap = jnp.where(keep, pltpu.roll(a, s, 0), 1.0)
            bp = jnp.where(keep, pltpu.roll(b, s, 0), 0.0)
        else:
            keep = sub < G - s
            ap = jnp.where(keep, pltpu.roll(a, tm - s, 0), 1.0)
            bp = jnp.where(keep, pltpu.roll(b, tm - s, 0), 0.0)
        b = a * bp + b
        a = a * ap
        s *= 2
    n_groups = tm // G
    outs = [None] * n_groups
    carry = h0
    for g in (range(n_groups - 1, -1, -1) if reverse else range(n_groups)):
        hg = b[g * G:(g + 1) * G] + a[g * G:(g + 1) * G] * carry
        outs[g] = hg
        carry = hg[0:1] if reverse else hg[G - 1:G]
    return jnp.concatenate(outs, axis=0), carry


def _lru_kernel(*refs, n_ctx, nt, reverse, final):
    if final:
        (x_ref, xp_ref, xn_ref, y_ref, hf_ref, cw_ref, cb_ref, wg_ref, gb_ref, lam_ref, o_ref, carry) = refs
    else:
        (x_ref, xp_ref, xn_ref, cw_ref, cb_ref, wg_ref, gb_ref, lam_ref, o_ref, carry) = refs
    j = pl.program_id(1)
    t = _scan_tile(j, n_ctx, nt, reverse)
    is_first, is_last = _stream_edges(t, n_ctx, nt)

    @pl.when(j == 0)
    def _():
        carry[...] = jnp.zeros(carry.shape, F32)

    x = x_ref[0]
    tm, W = x.shape
    xc = _dwconv4(x, xp_ref[0], xn_ref[0], cw_ref, is_first, is_last) + cb_ref[...]
    g = jnp.dot(xc.astype(BF16), wg_ref[...], preferred_element_type=F32) + gb_ref[...]
    r = jax.nn.sigmoid(g[:, :W])
    i = jax.nn.sigmoid(g[:, W:])
    log_a = (-LRU_C) * r * _softplus(-lam_ref[...])
    a = jnp.exp(log_a)
    th = jnp.tanh(log_a)
    bb = jnp.sqrt(-2.0 * th / (1.0 - th)) * (i * xc)
    h, h_last = _linear_scan_tile(a, bb, carry[0:1, :], reverse)
    carry[0:1, :] = h_last
    if final:
        o_ref[0] = jax.nn.gelu(y_ref[0]) * (hf_ref[0] + h)
    else:
        o_ref[0] = h


def _lru_pass(P, cols, conv_w, conv_b, wg, gb, lam, n_ctx, reverse, hf=None):
    B, Tt, _ = P.shape
    tm = ROW_TILE
    W = BRANCH_WIDTH
    nt = Tt // tm
    final = hf is not None

    def cur(b, j):
        return (b, _scan_tile(j, n_ctx, nt, reverse), 0)

    in_specs = _halo_specs(tm, W, cols["lx"] // W, n_ctx, nt, reverse)
    args = [P, P, P]
    if final:
        yb = cols["ly"] // W
        in_specs += [pl.BlockSpec((1, tm, W), lambda b, j: (b, _scan_tile(j, n_ctx, nt, reverse), yb)),
                     pl.BlockSpec((1, tm, W), cur)]
        args += [P, hf]
    in_specs += [pl.BlockSpec((4, W), lambda b, j: (0, 0)),
                 pl.BlockSpec((1, W), lambda b, j: (0, 0)),
                 pl.BlockSpec((W, 2 * W), lambda b, j: (0, 0)),
                 pl.BlockSpec((1, 2 * W), lambda b, j: (0, 0)),
                 pl.BlockSpec((1, W), lambda b, j: (0, 0))]
    args += [conv_w, conv_b.reshape(1, W), wg, gb.reshape(1, 2 * W), lam.reshape(1, W)]
    return pl.pallas_call(
        functools.partial(_lru_kernel, n_ctx=n_ctx, nt=nt, reverse=reverse, final=final),
        grid=(B, nt),
        in_specs=in_specs,
        out_specs=pl.BlockSpec((1, tm, W), cur),
        out_shape=jax.ShapeDtypeStruct((B, Tt, W), F32),
        scratch_shapes=[pltpu.VMEM((HALO_ROWS, W), F32)],
        compiler_params=_cparams(("parallel", "arbitrary")),
        name="rglru_bwd" if reverse else "rglru_fwd",
    )(*args)


def _lru_gate_weights(gate_w_d):
    mats = []
    for g in range(2):
        mats.append(jax.scipy.linalg.block_diag(*[gate_w_d[g, n] for n in range(LRU_BLOCKS)]))
    return jnp.concatenate(mats, axis=1).astype(BF16)


def _split_bf16(x):
    hi = x.astype(BF16)
    lo = (x - hi.astype(F32)).astype(BF16)
    return hi, lo


def _dot3(a_hi, a_lo, b_hi, b_lo):
    return (jnp.dot(a_hi, b_hi, preferred_element_type=F32)
            + (jnp.dot(a_hi, b_lo, preferred_element_type=F32) + jnp.dot(a_lo, b_hi, preferred_element_type=F32)))


def _gdn_kernel(*refs, n_ctx, nt, reverse, final, d):
    if final:
        (x_ref, xp_ref, xn_ref, sm_ref, z_ref, of_ref, cw_ref, rate_ref, dtb_ref, ng_ref, o_ref, S_s, qkv_s) = refs
    else:
        (x_ref, xp_ref, xn_ref, sm_ref, cw_ref, rate_ref, dtb_ref, o_ref, S_s, qkv_s) = refs
    j = pl.program_id(1)
    t = _scan_tile(j, n_ctx, nt, reverse)
    is_first, is_last = _stream_edges(t, n_ctx, nt)

    @pl.when(j == 0)
    def _():
        S_s[...] = jnp.zeros(S_s.shape, F32)

    x = x_ref[0]
    tm = x.shape[0]
    C = GDN_CHUNK
    nC = tm // C
    HK = GDN_HEADS * GDN_DK
    qkv = _dwconv4(x, xp_ref[0], xn_ref[0], cw_ref, is_first, is_last)
    qkv_s[...] = qkv * jax.nn.sigmoid(qkv)

    sm = sm_ref[0]
    beta_all = jax.nn.sigmoid(sm)
    g_all = -jnp.exp(rate_ref[...]) * _softplus(sm + dtb_ref[...])

    ii = lax.broadcasted_iota(jnp.int32, (tm, tm), 0)
    jj = lax.broadcasted_iota(jnp.int32, (tm, tm), 1)
    same = (ii // C) == (jj // C)
    if not reverse:
        incl = jnp.logical_and(same, ii >= jj)
        strict = jnp.logical_and(same, ii > jj)
    else:
        incl = jnp.logical_and(same, ii <= jj)
        strict = jnp.logical_and(same, ii < jj)
    incl_b = incl.astype(F32).astype(BF16)
    eye = (ii == jj).astype(F32)

    g1 = g_all.astype(BF16)
    g2 = (g_all - g1.astype(F32)).astype(BF16)
    g3 = (g_all - g1.astype(F32) - g2.astype(F32)).astype(BF16)
    gc_all = (jnp.dot(incl_b, g1, preferred_element_type=F32)
              + (jnp.dot(incl_b, g2, preferred_element_type=F32) + jnp.dot(incl_b, g3, preferred_element_type=F32)))
    last = 0 if reverse else C - 1
    gtot_all = jnp.concatenate(
        [jnp.broadcast_to(gc_all[c * C + last:c * C + last + 1, :], (C, 128)) for c in range(nC)], axis=0)

    groups = [_gdn_group_stages(hs, d=d, tm=tm, reverse=reverse, final=final, qkv_s=qkv_s, S_s=S_s, o_ref=o_ref,
                                of_ref=of_ref if final else None, z_ref=z_ref if final else None,
                                ng_ref=ng_ref if final else None, beta_all=beta_all, gc_all=gc_all,
                                gtot_all=gtot_all, incl=incl, strict=strict, eye=eye)
              for hs in GDN_HEAD_GROUPS]
    _run_skewed(groups, GDN_STAGE_SKEW)


def _run_skewed(gens, skew):
    done = [False] * len(gens)
    rnd = 0
    while not all(done):
        for k, g in enumerate(gens):
            if done[k] or rnd < k * skew:
                continue
            try:
                next(g)
            except StopIteration:
                done[k] = True
        rnd += 1


def _gdn_group_stages(heads, *, d, tm, reverse, final, qkv_s, S_s, o_ref, of_ref, z_ref, ng_ref, beta_all, gc_all,
                      gtot_all, incl, strict, eye):
    C = GDN_CHUNK
    nC = tm // C
    HK = GDN_HEADS * GDN_DK
    rhs, q_dec, k_dec, g_end, m_k, attn = {}, {}, {}, {}, {}, {}
    for h in heads:
        q = qkv_s[:, h * GDN_DK:(h + 1) * GDN_DK]
        k = qkv_s[:, HK + h * GDN_DK:HK + (h + 1) * GDN_DK]
        v = qkv_s[:, 2 * HK + h * GDN_DV:2 * HK + (h + 1) * GDN_DV]
        qn = q * lax.rsqrt(jnp.sum(q * q, axis=-1, keepdims=True) + EPS) * (GDN_DK ** -0.5)
        kn = k * lax.rsqrt(jnp.sum(k * k, axis=-1, keepdims=True) + EPS)
        cb = d * GDN_HEADS + h
        cg = N_DIR * GDN_HEADS + d * GDN_HEADS + h
        beta_b = jnp.broadcast_to(beta_all[:, cb:cb + 1], (tm, 128))
        gc_b = jnp.broadcast_to(gc_all[:, cg:cg + 1], (tm, 128))
        gtot_b = jnp.broadcast_to(gtot_all[:, cg:cg + 1], (tm, 128))
        eg = jnp.exp(gc_b)
        kb = kn * beta_b
        rhs[h] = jnp.concatenate([v * beta_b, kb * eg], axis=1).astype(BF16)
        q_dec[h] = qn * eg
        k_dec[h] = kn * jnp.exp(gtot_b - gc_b)
        g_end[h] = jnp.exp(gtot_b)
        gc_cols = jnp.concatenate([gc_b] * (tm // 128), axis=1)
        gc_rows = gc_b.T[0:1, :]
        decay = jnp.exp(jnp.where(incl, gc_cols - gc_rows, NEG_BIG))
        kq = lax.dot_general(jnp.concatenate([kb, qn], axis=0).astype(BF16), kn.astype(BF16),
                             (((1,), (1,)), ((), ())), preferred_element_type=F32)
        m_k[h] = jnp.where(strict, -(kq[:tm] * decay), 0.0)
        attn[h] = (kq[tm:] * decay).astype(BF16)
    yield

    t_inv = {h: eye + m_k[h] for h in heads}
    p_k = {}
    for h in heads:
        m_hi, m_lo = _split_bf16(m_k[h])
        p_k[h] = _dot3(m_hi, m_lo, m_hi, m_lo)
    yield
    for _ in range(4):
        for h in heads:
            p_hi, p_lo = _split_bf16(p_k[h])
            t_hi, t_lo = _split_bf16(t_inv[h])
            r2 = _dot3(jnp.concatenate([t_hi, p_hi], axis=0), jnp.concatenate([t_lo, p_lo], axis=0), p_hi, p_lo)
            t_inv[h] = t_inv[h] + r2[:tm]
            p_k[h] = r2[tm:]
        yield
    u, w = {}, {}
    for h in heads:
        p_hi, p_lo = _split_bf16(p_k[h])
        t_hi, t_lo = _split_bf16(t_inv[h])
        t_fin = t_inv[h] + _dot3(t_hi, t_lo, p_hi, p_lo)
        sol = jnp.dot(t_fin.astype(BF16), rhs[h], preferred_element_type=F32)
        u[h] = sol[:, :GDN_DV]
        w[h] = sol[:, GDN_DV:]
    yield

    q_eff, o_free, kw = {}, {}, {}
    for h in heads:
        wu = jnp.concatenate([w[h], u[h]], axis=1).astype(BF16)
        aw = jnp.dot(attn[h], wu, preferred_element_type=F32)
        q_eff[h] = (q_dec[h] - aw[:, :GDN_DV]).astype(BF16)
        o_free[h] = aw[:, GDN_DV:]
        kd = k_dec[h].astype(BF16)
        kw[h] = [lax.dot_general(kd[c * C:(c + 1) * C], wu[c * C:(c + 1) * C], (((0,), (0,)), ((), ())),
                                 preferred_element_type=F32) for c in range(nC)]
    yield

    S = {h: S_s[h] for h in heads}
    for cc in range(nC):
        c = nC - 1 - cc if reverse else cc
        r0 = c * C
        for h in heads:
            s_b = S[h].astype(BF16)
            o_c = o_free[h][r0:r0 + C] + jnp.dot(q_eff[h][r0:r0 + C], s_b, preferred_element_type=F32)
            S[h] = (S[h] * _col(g_end[h][r0:r0 + 1, :], GDN_DK) + kw[h][c][:, GDN_DV:]
                    - jnp.dot(kw[h][c][:, :GDN_DV].astype(BF16), s_b, preferred_element_type=F32))
            if final:
                o_t = of_ref[0, r0:r0 + C, h * GDN_DV:(h + 1) * GDN_DV] + o_c
                ms = jnp.mean(o_t * o_t, axis=-1, keepdims=True)
                z = z_ref[0, r0:r0 + C, h * GDN_DV:(h + 1) * GDN_DV]
                o_ref[0, r0:r0 + C, h * GDN_DV:(h + 1) * GDN_DV] = (
                    o_t * lax.rsqrt(ms + EPS) * ng_ref[...] * (z * jax.nn.sigmoid(z)))
            else:
                o_ref[0, r0:r0 + C, h * GDN_DV:(h + 1) * GDN_DV] = o_c
        yield
    for h in heads:
        S_s[h] = S[h]


def _col(row_vec, n):
    return jnp.broadcast_to(row_vec, (n, row_vec.shape[1]))


def _gdn_pass(P, Psm, cols, conv_w, rate_row, dtb_row, n_ctx, reverse, d, of=None, norm_g=None):
    B, Tt, _ = P.shape
    tm = ROW_TILE
    W = BRANCH_WIDTH
    QW = GDN_HEADS * (2 * GDN_DK + GDN_DV)
    nt = Tt // tm
    final = of is not None

    def cur(b, j):
        return (b, _scan_tile(j, n_ctx, nt, reverse), 0)

    in_specs = _halo_specs(tm, QW, cols["gqkv"] // QW, n_ctx, nt, reverse)
    in_specs += [pl.BlockSpec((1, tm, 128), cur)]
    args = [P, P, P, Psm]
    if final:
        zb = cols["gz"] // W
        in_specs += [pl.BlockSpec((1, tm, W), lambda b, j: (b, _scan_tile(j, n_ctx, nt, reverse), zb)),
                     pl.BlockSpec((1, tm, W), cur)]
        args += [P, of]
    in_specs += [pl.BlockSpec((4, QW), lambda b, j: (0, 0)),
                 pl.BlockSpec((1, 128), lambda b, j: (0, 0)),
                 pl.BlockSpec((1, 128), lambda b, j: (0, 0))]
    args += [conv_w, rate_row, dtb_row]
    if final:
        in_specs += [pl.BlockSpec((1, GDN_DV), lambda b, j: (0, 0))]
        args += [norm_g.reshape(1, GDN_DV)]
    return pl.pallas_call(
        functools.partial(_gdn_kernel, n_ctx=n_ctx, nt=nt, reverse=reverse, final=final, d=d),
        grid=(B, nt),
        in_specs=in_specs,
        out_specs=pl.BlockSpec((1, tm, W), cur),
        out_shape=jax.ShapeDtypeStruct((B, Tt, W), F32),
        scratch_shapes=[pltpu.VMEM((GDN_HEADS, GDN_DK, GDN_DV), F32), pltpu.VMEM((tm, QW), F32)],
        compiler_params=_cparams(("parallel", "arbitrary")),
        name="gdn_bwd" if reverse else "gdn_fwd",
    )(*args)


def _merge_kernel(x_ref, da_ref, lru_ref, gdn_ref, g0_ref, g1_ref, g2_ref, mod_ref, wb_ref, wo_ref, o_ref):
    outs = (da_ref, lru_ref, gdn_ref)
    gates = (g0_ref, g1_ref, g2_ref)
    merged = None
    for i in range(N_BRANCH):
        y = jnp.dot(outs[i][0].astype(BF16), wb_ref[i], preferred_element_type=F32)
        term = jax.nn.sigmoid(gates[i][0]) * y
        merged = term if merged is None else merged + term
    proj = jnp.dot(merged.astype(BF16), wo_ref[...], preferred_element_type=F32)
    o_ref[0] = x_ref[0] + mod_ref[0, 2:3, :] * proj


def _merge(xa, da, lru, gdn, P, mod3, wb, wo, n_lat_tiles, ctx_row):
    B, Tt, D = xa.shape
    tm = ROW_TILE
    W = BRANCH_WIDTH
    row = lambda b, t: (b, t, 0)
    return pl.pallas_call(
        _merge_kernel,
        grid=(B, Tt // tm),
        in_specs=[pl.BlockSpec((1, tm, D), row),
                  pl.BlockSpec((1, tm, W), row), pl.BlockSpec((1, tm, W), row), pl.BlockSpec((1, tm, W), row),
                  pl.BlockSpec((1, tm, D), lambda b, t: (b, t, 0)),
                  pl.BlockSpec((1, tm, D), lambda b, t: (b, t, 1)),
                  pl.BlockSpec((1, tm, D), lambda b, t: (b, t, 2)),
                  pl.BlockSpec((1, N_MOD, D), lambda b, t: (jnp.where(t >= n_lat_tiles, ctx_row, b), 0, 0)),
                  pl.BlockSpec((N_BRANCH, W, D), lambda b, t: (0, 0, 0)),
                  pl.BlockSpec((D, D), lambda b, t: (0, 0))],
        out_specs=pl.BlockSpec((1, tm, D), row),
        out_shape=jax.ShapeDtypeStruct((B, Tt, D), F32),
        compiler_params=_cparams(("parallel", "parallel")),
        name="merge",
    )(xa, da, lru, gdn, P, P, P, mod3, wb, wo)


def _mlp_kernel(x_ref, mod_ref, g_ref, w1_ref, w2_ref, o_ref, *, ff_chunk):
    x = x_ref[0]
    ms = jnp.mean(x * x, axis=-1, keepdims=True)
    y = x * lax.rsqrt(ms + EPS) * g_ref[...]
    h = (y * (1.0 + mod_ref[0, 4:5, :]) + mod_ref[0, 3:4, :]).astype(BF16)
    n_ff = w1_ref.shape[1]
    acc = None
    for c0 in range(0, n_ff, ff_chunk):
        a = jnp.dot(h, w1_ref[:, c0:c0 + ff_chunk], preferred_element_type=F32)
        a = jnp.square(jnp.maximum(a, 0.0)).astype(BF16)
        part = jnp.dot(a, w2_ref[c0:c0 + ff_chunk, :], preferred_element_type=F32)
        acc = part if acc is None else acc + part
    o_ref[0] = x + mod_ref[0, 5:6, :] * acc


def _mlp(xa, mod3, g, w1, w2, n_lat_tiles, ctx_row, n_rows):
    B, Tt, D = xa.shape
    tm = ROW_TILE
    F = w1.shape[1]
    return pl.pallas_call(
        functools.partial(_mlp_kernel, ff_chunk=min(F, 1024)),
        grid=(B, n_rows // tm),
        in_specs=[pl.BlockSpec((1, tm, D), lambda b, t: (b, t, 0)),
                  pl.BlockSpec((1, N_MOD, D), lambda b, t: (jnp.where(t >= n_lat_tiles, ctx_row, b), 0, 0)),
                  pl.BlockSpec((1, D), lambda b, t: (0, 0)),
                  pl.BlockSpec((D, F), lambda b, t: (0, 0)),
                  pl.BlockSpec((F, D), lambda b, t: (0, 0))],
        out_specs=pl.BlockSpec((1, tm, D), lambda b, t: (b, t, 0)),
        out_shape=jax.ShapeDtypeStruct((B, n_rows, D), F32),
        compiler_params=_cparams(("parallel", "parallel")),
        name="mlp",
    )(xa, mod3, g.reshape(1, D), w1, w2)


def _proj_layout(D):
    W = BRANCH_WIDTH
    QW = GDN_HEADS * (2 * GDN_DK + GDN_DV)
    src = {}
    off = 0
    for name, width in (("q", W), ("k", W), ("v", W), ("lx", W), ("ly", W), ("gqkv", QW), ("gz", W),
                        ("gb", N_DIR * GDN_HEADS), ("ga", N_DIR * GDN_HEADS), ("gate", N_BRANCH * D)):
        src[name] = (off, width)
        off += width
    order = ("gate", "gqkv", "q", "k", "v", "lx", "ly", "gz")
    cols = {}
    idx = []
    o = 0
    for name in order:
        s, w = src[name]
        cols[name] = o
        idx.append(jnp.arange(s, s + w))
        o += w
    small_idx = jnp.concatenate([jnp.arange(src["gb"][0], src["gb"][0] + src["gb"][1]),
                                 jnp.arange(src["ga"][0], src["ga"][0] + src["ga"][1])])
    return cols, jnp.concatenate(idx), small_idx


def _rope_tables(n_ctx_rows, n_lat_rows):
    n_freq = DA_QK_DIM // 4
    inv = ROPE_BASE ** (-jnp.arange(n_freq, dtype=F32) / n_freq)
    tpos = jnp.arange(n_lat_rows, dtype=jnp.int32)
    ang_r = (tpos // GRID_W).astype(F32)[:, None] * inv
    ang_c = (tpos % GRID_W).astype(F32)[:, None] * inv
    ang = jnp.concatenate([ang_r, ang_r, ang_c, ang_c], axis=-1)
    ang = jnp.concatenate([ang, ang], axis=-1)
    sign = jnp.where((jnp.arange(128) % 32) < 16, -1.0, 1.0).astype(F32)
    cos = jnp.concatenate([jnp.cos(ang), jnp.ones((n_ctx_rows, 128), F32)], axis=0)
    sin = jnp.concatenate([jnp.sin(ang) * sign, jnp.zeros((n_ctx_rows, 128), F32)], axis=0)
    return cos, sin


def kernel(x, c, ctx, c_ctx, ada_w, ada_b, norm1_g, norm2_g, w_in, da_q_norm_g, da_k_norm_g, da_lambda, da_sub_norm_g, lru_conv_w, lru_conv_b, lru_gate_w, lru_gate_b, lru_lambda, gdn_conv_w, gdn_A_log, gdn_dt_bias, gdn_norm_g, w_branch, w_out, mlp_w1, mlp_w2):
    B, T, D = x.shape
    Tc = ctx.shape[1]
    depth = ada_w.shape[0]
    tm = ROW_TILE
    assert Tc % tm == 0 and T % tm == 0 and B + 1 <= 8
    n_ctx = Tc // tm
    n_lat = T // tm
    Tt = Tc + T
    ctx_row = B

    cols, main_idx, small_idx = _proj_layout(D)
    cos_t, sin_t = _rope_tables(Tc, T)
    xa = jnp.concatenate([x, ctx], axis=1)
    c_all = jnp.concatenate([c, c_ctx[None, :], jnp.zeros((8 - B - 1, D), F32)], axis=0)

    tq = _pick(T, (512, 256))
    n_sub = _pick(Tt // tm, (3, 2, 1))

    out = None
    for layer in range(depth):
        last = layer == depth - 1
        lam_init = 0.8 - 0.6 * math.exp(-0.3 * layer)
        w_main = w_in[layer][:, main_idx].astype(BF16)
        w_small = jnp.pad(w_in[layer][:, small_idx], ((0, 0), (0, 128 - small_idx.shape[0]))).astype(BF16)

        mod3 = _adaln(c_all, ada_w[layer], ada_b[layer]).reshape(8, N_MOD, D)
        h1 = _norm_mod(xa, mod3, norm1_g[layer], 0, n_lat, ctx_row).reshape(B * Tt, D)
        P = _matmul(h1, w_main, "in_proj").reshape(B, Tt, -1)
        Psm = _matmul(h1, w_small, "in_proj_small").reshape(B, Tt, 128)

        QT, Ks, VT = _qkv_prep(P, cols, cos_t, sin_t, da_q_norm_g[layer], da_k_norm_g[layer])
        da_l = _flash(QT, Ks, VT, da_lambda[layer], da_sub_norm_g[layer], lam_init, 0, T, 0, Tt, tq, n_sub)
        if last:
            da = jnp.pad(da_l, ((0, 0), (0, Tc), (0, 0)))
        else:
            da_c = _flash(QT, Ks, VT, da_lambda[layer], da_sub_norm_g[layer], lam_init, T, Tc, T, Tc, tm, 1)
            da = jnp.concatenate([da_l, da_c], axis=1)

        hf = _lru_pass(P, cols, lru_conv_w[layer], lru_conv_b[layer], _lru_gate_weights(lru_gate_w[layer, 0]),
                       lru_gate_b[layer, 0].reshape(-1), lru_lambda[layer, 0], n_ctx, False)
        lru = _lru_pass(P, cols, lru_conv_w[layer], lru_conv_b[layer], _lru_gate_weights(lru_gate_w[layer, 1]),
                        lru_gate_b[layer, 1].reshape(-1), lru_lambda[layer, 1], n_ctx, True, hf=hf)

        rate_row = jnp.zeros((1, 128), F32).at[0, N_DIR * GDN_HEADS:2 * N_DIR * GDN_HEADS].set(
            gdn_A_log[layer].astype(F32).reshape(-1))
        dtb_row = jnp.zeros((1, 128), F32).at[0, N_DIR * GDN_HEADS:2 * N_DIR * GDN_HEADS].set(
            gdn_dt_bias[layer].astype(F32).reshape(-1))
        of = _gdn_pass(P, Psm, cols, gdn_conv_w[layer], rate_row, dtb_row, n_ctx, False, 0)
        gdn = _gdn_pass(P, Psm, cols, gdn_conv_w[layer], rate_row, dtb_row, n_ctx, True, 1, of=of,
                        norm_g=gdn_norm_g[layer])

        xa = _merge(xa, da, lru, gdn, P, mod3, w_branch[layer].astype(BF16), w_out[layer].astype(BF16),
                    n_lat, ctx_row)
        if last:
            out = _mlp(xa, mod3, norm2_g[layer], mlp_w1[layer].astype(BF16), mlp_w2[layer].astype(BF16),
                       n_lat, ctx_row, T)
        else:
            xa = _mlp(xa, mod3, norm2_g[layer], mlp_w1[layer].astype(BF16), mlp_w2[layer].astype(BF16),
                      n_lat, ctx_row, Tt)
    return out
```

```python
import functools
import math

import jax
import jax.numpy as jnp
from jax import lax
from jax.experimental import pallas as pl
from jax.experimental.pallas import tpu as pltpu

F32 = jnp.float32
BF16 = jnp.bfloat16
HIGHEST = lax.Precision.HIGHEST

N_MOD = 6
BRANCH_WIDTH = 512
N_BRANCH = 3
DA_HEADS = 4
DA_QK_DIM = 64
DA_V_DIM = 128
ROPE_BASE = 10000.0
LRU_BLOCKS = 8
LRU_C = 8.0
GDN_HEADS = 4
GDN_DK = 128
GDN_DV = 128
GDN_CHUNK = 64
N_DIR = 2
GRID_W = 64
EPS = 1e-6
NEG_BIG = -1e30

ROW_TILE = 256
HALO_ROWS = 8
VT_PAD = 16
GDN_HEAD_GROUPS = ((0, 1), (2, 3))
GDN_STAGE_SKEW = 2
VMEM_LIMIT = 48 * 1024 * 1024


def _cparams(sem):
    return pltpu.CompilerParams(dimension_semantics=sem, vmem_limit_bytes=VMEM_LIMIT)


def _softplus(z):
    return jnp.maximum(z, 0.0) + jnp.log1p(jnp.exp(-jnp.abs(z)))


def _adaln_kernel(c_ref, w_ref, b_ref, o_ref):
    c = c_ref[...]
    a = c * jax.nn.sigmoid(c)
    o_ref[...] = jnp.dot(a, w_ref[...], preferred_element_type=F32, precision=HIGHEST) + b_ref[...]


def _adaln(c_all, w, b):
    R, D = c_all.shape
    N = w.shape[1]
    tn = 1536 if N % 1536 == 0 else N
    return pl.pallas_call(
        _adaln_kernel,
        grid=(N // tn,),
        in_specs=[pl.BlockSpec((R, D), lambda j: (0, 0)),
                  pl.BlockSpec((D, tn), lambda j: (0, j)),
                  pl.BlockSpec((1, tn), lambda j: (0, j))],
        out_specs=pl.BlockSpec((R, tn), lambda j: (0, j)),
        out_shape=jax.ShapeDtypeStruct((R, N), F32),
        compiler_params=_cparams(("arbitrary",)),
        name="adaln",
    )(c_all, w, b.reshape(1, N))


def _norm_mod_kernel(x_ref, mod_ref, g_ref, o_ref, *, which):
    x = x_ref[0]
    ms = jnp.mean(x * x, axis=-1, keepdims=True)
    y = x * lax.rsqrt(ms + EPS) * g_ref[...]
    sh = mod_ref[0, 3 * which:3 * which + 1, :]
    sc = mod_ref[0, 3 * which + 1:3 * which + 2, :]
    o_ref[0] = (y * (1.0 + sc) + sh).astype(BF16)


def _norm_mod(xa, mod3, g, which, n_lat_tiles, ctx_row):
    B, Tt, D = xa.shape
    tm = ROW_TILE
    return pl.pallas_call(
        functools.partial(_norm_mod_kernel, which=which),
        grid=(B, Tt // tm),
        in_specs=[pl.BlockSpec((1, tm, D), lambda b, t: (b, t, 0)),
                  pl.BlockSpec((1, N_MOD, D), lambda b, t: (jnp.where(t >= n_lat_tiles, ctx_row, b), 0, 0)),
                  pl.BlockSpec((1, D), lambda b, t: (0, 0))],
        out_specs=pl.BlockSpec((1, tm, D), lambda b, t: (b, t, 0)),
        out_shape=jax.ShapeDtypeStruct((B, Tt, D), BF16),
        compiler_params=_cparams(("parallel", "parallel")),
        name="norm_mod",
    )(xa, mod3, g.reshape(1, D))


def _matmul_kernel(h_ref, w_ref, o_ref):
    o_ref[...] = jnp.dot(h_ref[...], w_ref[...], preferred_element_type=F32)


def _pick(n, cands):
    for c in cands:
        if n % c == 0:
            return c
    return n


def _matmul(h2, w, name):
    R, D = h2.shape
    N = w.shape[1]
    tm = _pick(R, (1024, 512, 256))
    tn = _pick(N, (1536, 1280, 1024, 512, 128))
    return pl.pallas_call(
        _matmul_kernel,
        grid=(N // tn, R // tm),
        in_specs=[pl.BlockSpec((tm, D), lambda j, i: (i, 0)),
                  pl.BlockSpec((D, tn), lambda j, i: (0, j))],
        out_specs=pl.BlockSpec((tm, tn), lambda j, i: (i, j)),
        out_shape=jax.ShapeDtypeStruct((R, N), F32),
        compiler_params=_cparams(("parallel", "parallel")),
        name=name,
    )(h2, w)


def _seg_mean(x2, s_ref):
    hi = x2.astype(BF16)
    lo = (x2 - hi.astype(F32)).astype(BF16)
    s = s_ref[...]
    return jnp.dot(hi, s, preferred_element_type=F32) + jnp.dot(lo, s, preferred_element_type=F32)


def _rope(y, cos, sin_signed, first_half):
    rot = jnp.where(first_half, pltpu.roll(y, 128 - 16, 1), pltpu.roll(y, 16, 1))
    return y * cos + rot * sin_signed


def _qkv_prep_kernel(q_ref, k_ref, v_ref, cos_ref, sin_ref, gq_ref, gk_ref, s_ref, qo_ref, ko_ref, vo_ref,
                     *, q_scale):
    cos = cos_ref[...]
    sin = sin_ref[...]
    tm = cos.shape[0]
    lane = lax.broadcasted_iota(jnp.int32, (tm, 128), 1)
    first_half = (lane % 32) < 16
    comp0 = lane < DA_QK_DIM

    q = q_ref[0]
    qn = q * lax.rsqrt(_seg_mean(q * q, s_ref) + EPS) * gq_ref[...]
    k = k_ref[0]
    kn = k * lax.rsqrt(_seg_mean(k * k, s_ref) + EPS) * gk_ref[...]
    v = v_ref[0]
    ones_row = jnp.where(lax.broadcasted_iota(jnp.int32, (VT_PAD, tm), 0) == 0, 1.0, 0.0).astype(BF16)
    for h in range(DA_HEADS):
        qh = _rope(qn[:, h * 128:(h + 1) * 128], cos, sin, first_half) * q_scale
        qo_ref[0, h * 128:(h + 1) * 128, :] = qh.T.astype(BF16)
        kh = _rope(kn[:, h * 128:(h + 1) * 128], cos, sin, first_half)
        ko_ref[0, :, h * 256:h * 256 + 128] = jnp.where(comp0, kh, 0.0).astype(BF16)
        ko_ref[0, :, h * 256 + 128:(h + 1) * 256] = jnp.where(comp0, 0.0, kh).astype(BF16)
        vo_ref[0, h, 0, 0:DA_V_DIM, :] = v[:, h * DA_V_DIM:(h + 1) * DA_V_DIM].T.astype(BF16)
        vo_ref[0, h, 0, DA_V_DIM:DA_V_DIM + VT_PAD, :] = ones_row


def _qkv_prep(P, cols, cos_t, sin_t, gq, gk):
    B, Tt, _ = P.shape
    tm = ROW_TILE
    W = BRANCH_WIDTH
    seg = (jnp.arange(W)[:, None] // DA_QK_DIM == jnp.arange(W)[None, :] // DA_QK_DIM)
    smat = (seg.astype(F32) / DA_QK_DIM).astype(BF16)
    gq_t = jnp.tile(gq, W // DA_QK_DIM).reshape(1, W)
    gk_t = jnp.tile(gk, W // DA_QK_DIM).reshape(1, W)
    qb, kb, vb = cols["q"] // W, cols["k"] // W, cols["v"] // W
    q_scale = (DA_QK_DIM ** -0.5) * math.log2(math.e)
    return pl.pallas_call(
        functools.partial(_qkv_prep_kernel, q_scale=q_scale),
        grid=(B, Tt // tm),
        in_specs=[pl.BlockSpec((1, tm, W), lambda b, t: (b, t, qb)),
                  pl.BlockSpec((1, tm, W), lambda b, t: (b, t, kb)),
                  pl.BlockSpec((1, tm, W), lambda b, t: (b, t, vb)),
                  pl.BlockSpec((tm, 128), lambda b, t: (t, 0)),
                  pl.BlockSpec((tm, 128), lambda b, t: (t, 0)),
                  pl.BlockSpec((1, W), lambda b, t: (0, 0)),
                  pl.BlockSpec((1, W), lambda b, t: (0, 0)),
                  pl.BlockSpec((W, W), lambda b, t: (0, 0))],
        out_specs=[pl.BlockSpec((1, W, tm), lambda b, t: (b, 0, t)),
                   pl.BlockSpec((1, tm, 2 * W), lambda b, t: (b, t, 0)),
                   pl.BlockSpec((1, DA_HEADS, 1, DA_V_DIM + VT_PAD, tm), lambda b, t: (b, 0, t, 0, 0))],
        out_shape=[jax.ShapeDtypeStruct((B, W, Tt), BF16),
                   jax.ShapeDtypeStruct((B, Tt, 2 * W), BF16),
                   jax.ShapeDtypeStruct((B, DA_HEADS, Tt // tm, DA_V_DIM + VT_PAD, tm), BF16)],
        compiler_params=_cparams(("parallel", "parallel")),
        name="qkv_prep",
    )(P, P, P, cos_t, sin_t, gq_t, gk_t, smat)


def _flash_kernel(qt_ref, k_ref, vt_ref, lam_ref, subg_ref, o_ref, acc_s,
                  *, tq, sub, n_sub, nk, chunk0, lam_init):
    qt = qt_ref[0]
    acc_s[...] = jnp.zeros(acc_s.shape, F32)
    tk = sub * n_sub

    def scores(j):
        r0 = (chunk0 + j * n_sub) * sub
        kk = k_ref[0, r0:r0 + tk, :]
        k2 = jnp.concatenate([kk[:, :128], kk[:, 128:]], axis=0)
        return jnp.dot(k2, qt, preferred_element_type=F32)

    def accumulate(pend):
        j, ps, alphas = pend
        c0 = chunk0 + j * n_sub
        vt = jnp.concatenate([vt_ref[0, 0, c0 + i] for i in range(n_sub)], axis=1)
        for c in range(2):
            acc_s[c] = alphas[c] * acc_s[c] + jnp.dot(vt, ps[c], preferred_element_type=F32)

    ms = [jnp.full((1, tq), -jnp.inf, F32)] * 2
    st = scores(0)
    pend = None
    for j in range(nk):
        st_next = scores(j + 1) if j + 1 < nk else None
        if pend is not None:
            accumulate(pend)
        ps, alphas = [], []
        for c in range(2):
            s_c = st[c * tk:(c + 1) * tk]
            m_new = jnp.maximum(ms[c], jnp.max(s_c, axis=0, keepdims=True))
            alphas.append(jnp.exp2(ms[c] - m_new))
            ps.append(jnp.exp2(s_c - m_new).astype(BF16))
            ms[c] = m_new
        pend = (j, ps, alphas)
        st = st_next
    accumulate(pend)

    a0 = acc_s[0]
    a1 = acc_s[1]
    o0 = a0[:DA_V_DIM] / a0[DA_V_DIM:DA_V_DIM + 1]
    o1 = a1[:DA_V_DIM] / a1[DA_V_DIM:DA_V_DIM + 1]
    lv = lam_ref[...]
    s01 = jnp.sum(lv[0:1] * lv[1:2], axis=-1, keepdims=True)
    s23 = jnp.sum(lv[2:3] * lv[3:4], axis=-1, keepdims=True)
    lam = jnp.exp(s01) - jnp.exp(s23) + lam_init
    d = o0 - lam * o1
    ms_ = jnp.mean(d * d, axis=0, keepdims=True)
    o_ref[0] = (d * lax.rsqrt(ms_ + EPS)).T * subg_ref[...] * (1.0 - lam_init)


def _flash(QT, Ks, VT, lam_vec, sub_g, lam_init, q_row0, n_q_rows, k_row0, n_k_rows, tq, n_sub):
    B, _, Tt = QT.shape
    sub = VT.shape[-1]
    tk = sub * n_sub
    nq, nk = n_q_rows // tq, n_k_rows // tk
    q0 = q_row0 // tq
    vrows = VT.shape[3]
    return pl.pallas_call(
        functools.partial(_flash_kernel, tq=tq, sub=sub, n_sub=n_sub, nk=nk, chunk0=k_row0 // sub,
                          lam_init=lam_init),
        grid=(B, DA_HEADS, nq),
        in_specs=[pl.BlockSpec((1, 128, tq), lambda b, h, i: (b, h, q0 + i)),
                  pl.BlockSpec((1, Tt, 256), lambda b, h, i: (b, 0, h)),
                  pl.BlockSpec((1, 1, VT.shape[2], vrows, sub), lambda b, h, i: (b, h, 0, 0, 0)),
                  pl.BlockSpec((4, DA_QK_DIM), lambda b, h, i: (0, 0)),
                  pl.BlockSpec((1, DA_V_DIM), lambda b, h, i: (0, 0))],
        out_specs=pl.BlockSpec((1, tq, 128), lambda b, h, i: (b, i, h)),
        out_shape=jax.ShapeDtypeStruct((B, n_q_rows, BRANCH_WIDTH), F32),
        scratch_shapes=[pltpu.VMEM((2, vrows, tq), F32)],
        compiler_params=_cparams(("parallel", "parallel", "arbitrary")),
        name="diff_attn",
    )(QT, Ks, VT, lam_vec, sub_g.reshape(1, DA_V_DIM))


def _scan_tile(j, n_ctx, nt, reverse):
    n_lat = nt - n_ctx
    if not reverse:
        return jnp.where(j < n_ctx, n_lat + j, j - n_ctx)
    return jnp.where(j < n_ctx, nt - 1 - j, n_lat - 1 - (j - n_ctx))


def _stream_edges(t, n_ctx, nt):
    n_lat = nt - n_ctx
    return jnp.logical_or(t == 0, t == n_lat), jnp.logical_or(t == n_lat - 1, t == nt - 1)


def _dwconv4(x, prev8, next8, w_ref, is_first, is_last):
    tm = x.shape[0]
    row = lax.broadcasted_iota(jnp.int32, x.shape, 0)
    pf = jnp.where(is_first, 0.0, 1.0)
    nf = jnp.where(is_last, 0.0, 1.0)
    prow = prev8[HALO_ROWS - 1:HALO_ROWS, :] * pf
    n0 = next8[0:1, :] * nf
    n1 = next8[1:2, :] * nf
    xm1 = jnp.where(row == 0, prow, pltpu.roll(x, 1, 0))
    xp1 = jnp.where(row == tm - 1, n0, pltpu.roll(x, tm - 1, 0))
    xp2 = jnp.where(row == tm - 2, n0, jnp.where(row == tm - 1, n1, pltpu.roll(x, tm - 2, 0)))
    return w_ref[0:1, :] * xm1 + w_ref[1:2, :] * x + w_ref[2:3, :] * xp1 + w_ref[3:4, :] * xp2


def _halo_specs(tm, width, col_blk, n_ctx, nt, reverse):
    r8 = tm // HALO_ROWS
    last8 = nt * r8 - 1

    def cur(b, j):
        return (b, _scan_tile(j, n_ctx, nt, reverse), col_blk)

    def prev(b, j):
        return (b, jnp.maximum(_scan_tile(j, n_ctx, nt, reverse) * r8 - 1, 0), col_blk)

    def nxt(b, j):
        return (b, jnp.minimum((_scan_tile(j, n_ctx, nt, reverse) + 1) * r8, last8), col_blk)

    return [pl.BlockSpec((1, tm, width), cur),
            pl.BlockSpec((1, HALO_ROWS, width), prev),
            pl.BlockSpec((1, HALO_ROWS, width), nxt)]


def _linear_scan_tile(a, b, h0, reverse):
    tm = a.shape[0]
    G = HALO_ROWS
    sub = lax.broadcasted_iota(jnp.int32, a.shape, 0) % G
    s = 1
    while s < G:
        if not reverse:
            keep = sub >= s
            ap = jnp.where(keep, pltpu.roll(a, s, 0), 1.0)
            bp = jnp.where(keep, pltpu.roll(b, s, 0), 0.0)
        else:
            keep = sub < G - s
            ap = jnp.where(keep, pltpu.roll(a, tm - s, 0), 1.0)
            bp = jnp.where(keep, pltpu.roll(b, tm - s, 0), 0.0)
        b = a * bp + b
        a = a * ap
        s *= 2
    n_groups = tm // G
    outs = [None] * n_groups
    carry = h0
    for g in (range(n_groups - 1, -1, -1) if reverse else range(n_groups)):
        hg = b[g * G:(g + 1) * G] + a[g * G:(g + 1) * G] * carry
        outs[g] = hg
        carry = hg[0:1] if reverse else hg[G - 1:G]
    return jnp.concatenate(outs, axis=0), carry


def _lru_kernel(*refs, n_ctx, nt, reverse, final):
    if final:
        (x_ref, xp_ref, xn_ref, y_ref, hf_ref, cw_ref, cb_ref, wg_ref, gb_ref, lam_ref, o_ref, carry) = refs
    else:
        (x_ref, xp_ref, xn_ref, cw_ref, cb_ref, wg_ref, gb_ref, lam_ref, o_ref, carry) = refs
    j = pl.program_id(1)
    t = _scan_tile(j, n_ctx, nt, reverse)
    is_first, is_last = _stream_edges(t, n_ctx, nt)

    @pl.when(j == 0)
    def _():
        carry[...] = jnp.zeros(carry.shape, F32)

    x = x_ref[0]
    tm, W = x.shape
    xc = _dwconv4(x, xp_ref[0], xn_ref[0], cw_ref, is_first, is_last) + cb_ref[...]
    g = jnp.dot(xc.astype(BF16), wg_ref[...], preferred_element_type=F32) + gb_ref[...]
    r = jax.nn.sigmoid(g[:, :W])
    i = jax.nn.sigmoid(g[:, W:])
    log_a = (-LRU_C) * r * _softplus(-lam_ref[...])
    a = jnp.exp(log_a)
    th = jnp.tanh(log_a)
    bb = jnp.sqrt(-2.0 * th / (1.0 - th)) * (i * xc)
    h, h_last = _linear_scan_tile(a, bb, carry[0:1, :], reverse)
    carry[0:1, :] = h_last
    if final:
        o_ref[0] = jax.nn.gelu(y_ref[0]) * (hf_ref[0] + h)
    else:
        o_ref[0] = h


def _lru_pass(P, cols, conv_w, conv_b, wg, gb, lam, n_ctx, reverse, hf=None):
    B, Tt, _ = P.shape
    tm = ROW_TILE
    W = BRANCH_WIDTH
    nt = Tt // tm
    final = hf is not None

    def cur(b, j):
        return (b, _scan_tile(j, n_ctx, nt, reverse), 0)

    in_specs = _halo_specs(tm, W, cols["lx"] // W, n_ctx, nt, reverse)
    args = [P, P, P]
    if final:
        yb = cols["ly"] // W
        in_specs += [pl.BlockSpec((1, tm, W), lambda b, j: (b, _scan_tile(j, n_ctx, nt, reverse), yb)),
                     pl.BlockSpec((1, tm, W), cur)]
        args += [P, hf]
    in_specs += [pl.BlockSpec((4, W), lambda b, j: (0, 0)),
                 pl.BlockSpec((1, W), lambda b, j: (0, 0)),
                 pl.BlockSpec((W, 2 * W), lambda b, j: (0, 0)),
                 pl.BlockSpec((1, 2 * W), lambda b, j: (0, 0)),
                 pl.BlockSpec((1, W), lambda b, j: (0, 0))]
    args += [conv_w, conv_b.reshape(1, W), wg, gb.reshape(1, 2 * W), lam.reshape(1, W)]
    return pl.pallas_call(
        functools.partial(_lru_kernel, n_ctx=n_ctx, nt=nt, reverse=reverse, final=final),
        grid=(B, nt),
        in_specs=in_specs,
        out_specs=pl.BlockSpec((1, tm, W), cur),
        out_shape=jax.ShapeDtypeStruct((B, Tt, W), F32),
        scratch_shapes=[pltpu.VMEM((HALO_ROWS, W), F32)],
        compiler_params=_cparams(("parallel", "arbitrary")),
        name="rglru_bwd" if reverse else "rglru_fwd",
    )(*args)


def _lru_gate_weights(gate_w_d):
    mats = []
    for g in range(2):
        mats.append(jax.scipy.linalg.block_diag(*[gate_w_d[g, n] for n in range(LRU_BLOCKS)]))
    return jnp.concatenate(mats, axis=1).astype(BF16)


def _split_bf16(x):
    hi = x.astype(BF16)
    lo = (x - hi.astype(F32)).astype(BF16)
    return hi, lo


def _dot3(a_hi, a_lo, b_hi, b_lo):
    return (jnp.dot(a_hi, b_hi, preferred_element_type=F32)
            + (jnp.dot(a_hi, b_lo, preferred_element_type=F32) + jnp.dot(a_lo, b_hi, preferred_element_type=F32)))


def _gdn_kernel(*refs, n_ctx, nt, reverse, final, d):
    if final:
        (x_ref, xp_ref, xn_ref, sm_ref, z_ref, of_ref, cw_ref, rate_ref, dtb_ref, ng_ref, o_ref, S_s, qkv_s) = refs
    else:
        (x_ref, xp_ref, xn_ref, sm_ref, cw_ref, rate_ref, dtb_ref, o_ref, S_s, qkv_s) = refs
    j = pl.program_id(1)
    t = _scan_tile(j, n_ctx, nt, reverse)
    is_first, is_last = _stream_edges(t, n_ctx, nt)

    @pl.when(j == 0)
    def _():
        S_s[...] = jnp.zeros(S_s.shape, F32)

    x = x_ref[0]
    tm = x.shape[0]
    C = GDN_CHUNK
    nC = tm // C
    qkv = _dwconv4(x, xp_ref[0], xn_ref[0], cw_ref, is_first, is_last)
    qkv_s[...] = qkv * jax.nn.sigmoid(qkv)

    sm = sm_ref[0]
    beta_all = jax.nn.sigmoid(sm)
    g_all = -jnp.exp(rate_ref[...]) * _softplus(sm + dtb_ref[...])

    ii = lax.broadcasted_iota(jnp.int32, (tm, tm), 0)
    jj = lax.broadcasted_iota(jnp.int32, (tm, tm), 1)
    same = (ii // C) == (jj // C)
    if not reverse:
        incl = jnp.logical_and(same, ii >= jj)
        strict = jnp.logical_and(same, ii > jj)
    else:
        incl = jnp.logical_and(same, ii <= jj)
        strict = jnp.logical_and(same, ii < jj)
    incl_b = incl.astype(F32).astype(BF16)
    eye = (ii == jj).astype(F32)

    g1 = g_all.astype(BF16)
    g2 = (g_all - g1.astype(F32)).astype(BF16)
    g3 = (g_all - g1.astype(F32) - g2.astype(F32)).astype(BF16)
    gc_all = (jnp.dot(incl_b, g1, preferred_element_type=F32)
              + (jnp.dot(incl_b, g2, preferred_element_type=F32) + jnp.dot(incl_b, g3, preferred_element_type=F32)))
    last = 0 if reverse else C - 1
    gtot_all = jnp.concatenate(
        [jnp.broadcast_to(gc_all[c * C + last:c * C + last + 1, :], (C, 128)) for c in range(nC)], axis=0)

    groups = [_gdn_group_stages(hs, d=d, tm=tm, reverse=reverse, final=final, qkv_s=qkv_s, S_s=S_s, o_ref=o_ref,
                                of_ref=of_ref if final else None, z_ref=z_ref if final else None,
                                ng_ref=ng_ref if final else None, beta_all=beta_all, gc_all=gc_all,
                                gtot_all=gtot_all, incl=incl, strict=strict, eye=eye)
              for hs in GDN_HEAD_GROUPS]
    _run_skewed(groups, GDN_STAGE_SKEW)


def _run_skewed(gens, skew):
    done = [False] * len(gens)
    rnd = 0
    while not all(done):
        for k, g in enumerate(gens):
            if done[k] or rnd < k * skew:
                continue
            try:
                next(g)
            except StopIteration:
                done[k] = True
        rnd += 1


def _gdn_group_stages(heads, *, d, tm, reverse, final, qkv_s, S_s, o_ref, of_ref, z_ref, ng_ref, beta_all, gc_all,
                      gtot_all, incl, strict, eye):
    C = GDN_CHUNK
    nC = tm // C
    HK = GDN_HEADS * GDN_DK
    rhs, q_dec, k_dec, g_end, m_k, attn = {}, {}, {}, {}, {}, {}
    for h in heads:
        q = qkv_s[:, h * GDN_DK:(h + 1) * GDN_DK]
        k = qkv_s[:, HK + h * GDN_DK:HK + (h + 1) * GDN_DK]
        v = qkv_s[:, 2 * HK + h * GDN_DV:2 * HK + (h + 1) * GDN_DV]
        qn = q * lax.rsqrt(jnp.sum(q * q, axis=-1, keepdims=True) + EPS) * (GDN_DK ** -0.5)
        kn = k * lax.rsqrt(jnp.sum(k * k, axis=-1, keepdims=True) + EPS)
        cb = d * GDN_HEADS + h
        cg = N_DIR * GDN_HEADS + d * GDN_HEADS + h
        beta_b = jnp.broadcast_to(beta_all[:, cb:cb + 1], (tm, 128))
        gc_b = jnp.broadcast_to(gc_all[:, cg:cg + 1], (tm, 128))
        gtot_b = jnp.broadcast_to(gtot_all[:, cg:cg + 1], (tm, 128))
        eg = jnp.exp(gc_b)
        kb = kn * beta_b
        rhs[h] = jnp.concatenate([v * beta_b, kb * eg], axis=1).astype(BF16)
        q_dec[h] = qn * eg
        k_dec[h] = kn * jnp.exp(gtot_b - gc_b)
        g_end[h] = jnp.exp(gtot_b)
        gc_cols = jnp.concatenate([gc_b] * (tm // 128), axis=1)
        gc_rows = gc_b.T[0:1, :]
        decay = jnp.exp(jnp.where(incl, gc_cols - gc_rows, NEG_BIG))
        kq = lax.dot_general(jnp.concatenate([kb, qn], axis=0).astype(BF16), kn.astype(BF16),
                             (((1,), (1,)), ((), ())), preferred_element_type=F32)
        m_k[h] = jnp.where(strict, -(kq[:tm] * decay), 0.0)
        attn[h] = (kq[tm:] * decay).astype(BF16)
    yield

    t_inv = {h: eye + m_k[h] for h in heads}
    p_k = {}
    for h in heads:
        m_hi, m_lo = _split_bf16(m_k[h])
        p_k[h] = _dot3(m_hi, m_lo, m_hi, m_lo)
    yield
    for _ in range(4):
        for h in heads:
            p_hi, p_lo = _split_bf16(p_k[h])
            t_hi, t_lo = _split_bf16(t_inv[h])
            r2 = _dot3(jnp.concatenate([t_hi, p_hi], axis=0), jnp.concatenate([t_lo, p_lo], axis=0), p_hi, p_lo)
            t_inv[h] = t_inv[h] + r2[:tm]
            p_k[h] = r2[tm:]
        yield
    u, w = {}, {}
    for h in heads:
        p_hi, p_lo = _split_bf16(p_k[h])
        t_hi, t_lo = _split_bf16(t_inv[h])
        t_fin = t_inv[h] + _dot3(t_hi, t_lo, p_hi, p_lo)
        sol = jnp.dot(t_fin.astype(BF16), rhs[h], preferred_element_type=F32)
        u[h] = sol[:, :GDN_DV]
        w[h] = sol[:, GDN_DV:]
    yield

    q_eff, o_free, kw = {}, {}, {}
    for h in heads:
        wu = jnp.concatenate([w[h], u[h]], axis=1).astype(BF16)
        aw = jnp.dot(attn[h], wu, preferred_element_type=F32)
        q_eff[h] = (q_dec[h] - aw[:, :GDN_DV]).astype(BF16)
        o_free[h] = aw[:, GDN_DV:]
        kd = k_dec[h].astype(BF16)
        kw[h] = [lax.dot_general(kd[c * C:(c + 1) * C], wu[c * C:(c + 1) * C], (((0,), (0,)), ((), ())),
                                 preferred_element_type=F32) for c in range(nC)]
    yield

    S = {h: S_s[h] for h in heads}
    for cc in range(nC):
        c = nC - 1 - cc if reverse else cc
        r0 = c * C
        for h in heads:
            s_b = S[h].astype(BF16)
            o_c = o_free[h][r0:r0 + C] + jnp.dot(q_eff[h][r0:r0 + C], s_b, preferred_element_type=F32)
            S[h] = (S[h] * _col(g_end[h][r0:r0 + 1, :], GDN_DK) + kw[h][c][:, GDN_DV:]
                    - jnp.dot(kw[h][c][:, :GDN_DV].astype(BF16), s_b, preferred_element_type=F32))
            if final:
                o_t = of_ref[0, r0:r0 + C, h * GDN_DV:(h + 1) * GDN_DV] + o_c
                ms = jnp.mean(o_t * o_t, axis=-1, keepdims=True)
                z = z_ref[0, r0:r0 + C, h * GDN_DV:(h + 1) * GDN_DV]
                o_ref[0, r0:r0 + C, h * GDN_DV:(h + 1) * GDN_DV] = (
                    o_t * lax.rsqrt(ms + EPS) * ng_ref[...] * (z * jax.nn.sigmoid(z)))
            else:
                o_ref[0, r0:r0 + C, h * GDN_DV:(h + 1) * GDN_DV] = o_c
        yield
    for h in heads:
        S_s[h] = S[h]


def _col(row_vec, n):
    return jnp.broadcast_to(row_vec, (n, row_vec.shape[1]))


def _gdn_pass(P, Psm, cols, conv_w, rate_row, dtb_row, n_ctx, reverse, d, of=None, norm_g=None):
    B, Tt, _ = P.shape
    tm = ROW_TILE
    W = BRANCH_WIDTH
    QW = GDN_HEADS * (2 * GDN_DK + GDN_DV)
    nt = Tt // tm
    final = of is not None

    def cur(b, j):
        return (b, _scan_tile(j, n_ctx, nt, reverse), 0)

    in_specs = _halo_specs(tm, QW, cols["gqkv"] // QW, n_ctx, nt, reverse)
    in_specs += [pl.BlockSpec((1, tm, 128), cur)]
    args = [P, P, P, Psm]
    if final:
        zb = cols["gz"] // W
        in_specs += [pl.BlockSpec((1, tm, W), lambda b, j: (b, _scan_tile(j, n_ctx, nt, reverse), zb)),
                     pl.BlockSpec((1, tm, W), cur)]
        args += [P, of]
    in_specs += [pl.BlockSpec((4, QW), lambda b, j: (0, 0)),
                 pl.BlockSpec((1, 128), lambda b, j: (0, 0)),
                 pl.BlockSpec((1, 128), lambda b, j: (0, 0))]
    args += [conv_w, rate_row, dtb_row]
    if final:
        in_specs += [pl.BlockSpec((1, GDN_DV), lambda b, j: (0, 0))]
        args += [norm_g.reshape(1, GDN_DV)]
    return pl.pallas_call(
        functools.partial(_gdn_kernel, n_ctx=n_ctx, nt=nt, reverse=reverse, final=final, d=d),
        grid=(B, nt),
        in_specs=in_specs,
        out_specs=pl.BlockSpec((1, tm, W), cur),
        out_shape=jax.ShapeDtypeStruct((B, Tt, W), F32),
        scratch_shapes=[pltpu.VMEM((GDN_HEADS, GDN_DK, GDN_DV), F32), pltpu.VMEM((tm, QW), F32)],
        compiler_params=_cparams(("parallel", "arbitrary")),
        name="gdn_bwd" if reverse else "gdn_fwd",
    )(*args)


def _post_kernel(x_ref, da_ref, lru_ref, gdn_ref, g0_ref, g1_ref, g2_ref, mod_ref, n2_ref, wb_ref, wo_ref,
                 w1_ref, w2_ref, o_ref, *, ff_chunk):
    outs = (da_ref, lru_ref, gdn_ref)
    gates = (g0_ref, g1_ref, g2_ref)
    merged = None
    for i in range(N_BRANCH):
        y = jnp.dot(outs[i][0].astype(BF16), wb_ref[i], preferred_element_type=F32)
        term = jax.nn.sigmoid(gates[i][0]) * y
        merged = term if merged is None else merged + term
    proj = jnp.dot(merged.astype(BF16), wo_ref[...], preferred_element_type=F32)
    x = x_ref[0] + mod_ref[0, 2:3, :] * proj

    ms = jnp.mean(x * x, axis=-1, keepdims=True)
    y = x * lax.rsqrt(ms + EPS) * n2_ref[...]
    h = (y * (1.0 + mod_ref[0, 4:5, :]) + mod_ref[0, 3:4, :]).astype(BF16)
    n_ff = w1_ref.shape[1]
    acc = None
    for c0 in range(0, n_ff, ff_chunk):
        a = jnp.dot(h, w1_ref[:, c0:c0 + ff_chunk], preferred_element_type=F32)
        a = jnp.square(jnp.maximum(a, 0.0)).astype(BF16)
        part = jnp.dot(a, w2_ref[c0:c0 + ff_chunk, :], preferred_element_type=F32)
        acc = part if acc is None else acc + part
    o_ref[0] = x + mod_ref[0, 5:6, :] * acc


def _post(xa, da, lru, gdn, P, mod3, n2_g, wb, wo, w1, w2, n_lat_tiles, ctx_row, n_rows):
    B, Tt, D = xa.shape
    tm = ROW_TILE
    W = BRANCH_WIDTH
    F = w1.shape[1]
    row = lambda b, t: (b, t, 0)
    once = pl.Buffered(1)
    return pl.pallas_call(
        functools.partial(_post_kernel, ff_chunk=min(F, 1024)),
        grid=(B, n_rows // tm),
        in_specs=[pl.BlockSpec((1, tm, D), row),
                  pl.BlockSpec((1, tm, W), row), pl.BlockSpec((1, tm, W), row), pl.BlockSpec((1, tm, W), row),
                  pl.BlockSpec((1, tm, D), lambda b, t: (b, t, 0)),
                  pl.BlockSpec((1, tm, D), lambda b, t: (b, t, 1)),
                  pl.BlockSpec((1, tm, D), lambda b, t: (b, t, 2)),
                  pl.BlockSpec((1, N_MOD, D), lambda b, t: (jnp.where(t >= n_lat_tiles, ctx_row, b), 0, 0)),
                  pl.BlockSpec((1, D), lambda b, t: (0, 0)),
                  pl.BlockSpec((N_BRANCH, W, D), lambda b, t: (0, 0, 0), pipeline_mode=once),
                  pl.BlockSpec((D, D), lambda b, t: (0, 0), pipeline_mode=once),
                  pl.BlockSpec((D, F), lambda b, t: (0, 0), pipeline_mode=once),
                  pl.BlockSpec((F, D), lambda b, t: (0, 0), pipeline_mode=once)],
        out_specs=pl.BlockSpec((1, tm, D), row),
        out_shape=jax.ShapeDtypeStruct((B, n_rows, D), F32),
        compiler_params=_cparams(("parallel", "parallel")),
        name="merge_mlp",
    )(xa, da, lru, gdn, P, P, P, mod3, n2_g.reshape(1, D), wb, wo, w1, w2)


def _proj_layout(D):
    W = BRANCH_WIDTH
    QW = GDN_HEADS * (2 * GDN_DK + GDN_DV)
    src = {}
    off = 0
    for name, width in (("q", W), ("k", W), ("v", W), ("lx", W), ("ly", W), ("gqkv", QW), ("gz", W),
                        ("gb", N_DIR * GDN_HEADS), ("ga", N_DIR * GDN_HEADS), ("gate", N_BRANCH * D)):
        src[name] = (off, width)
        off += width
    order = ("gate", "gqkv", "q", "k", "v", "lx", "ly", "gz")
    cols = {}
    idx = []
    o = 0
    for name in order:
        s, w = src[name]
        cols[name] = o
        idx.append(jnp.arange(s, s + w))
        o += w
    small_idx = jnp.concatenate([jnp.arange(src["gb"][0], src["gb"][0] + src["gb"][1]),
                                 jnp.arange(src["ga"][0], src["ga"][0] + src["ga"][1])])
    return cols, jnp.concatenate(idx), small_idx


def _rope_tables(n_ctx_rows, n_lat_rows):
    n_freq = DA_QK_DIM // 4
    inv = ROPE_BASE ** (-jnp.arange(n_freq, dtype=F32) / n_freq)
    tpos = jnp.arange(n_lat_rows, dtype=jnp.int32)
    ang_r = (tpos // GRID_W).astype(F32)[:, None] * inv
    ang_c = (tpos % GRID_W).astype(F32)[:, None] * inv
    ang = jnp.concatenate([ang_r, ang_r, ang_c, ang_c], axis=-1)
    ang = jnp.concatenate([ang, ang], axis=-1)
    sign = jnp.where((jnp.arange(128) % 32) < 16, -1.0, 1.0).astype(F32)
    cos = jnp.concatenate([jnp.cos(ang), jnp.ones((n_ctx_rows, 128), F32)], axis=0)
    sin = jnp.concatenate([jnp.sin(ang) * sign, jnp.zeros((n_ctx_rows, 128), F32)], axis=0)
    return cos, sin


def kernel(x, c, ctx, c_ctx, ada_w, ada_b, norm1_g, norm2_g, w_in, da_q_norm_g, da_k_norm_g, da_lambda, da_sub_norm_g, lru_conv_w, lru_conv_b, lru_gate_w, lru_gate_b, lru_lambda, gdn_conv_w, gdn_A_log, gdn_dt_bias, gdn_norm_g, w_branch, w_out, mlp_w1, mlp_w2):
    B, T, D = x.shape
    Tc = ctx.shape[1]
    depth = ada_w.shape[0]
    tm = ROW_TILE
    assert Tc % tm == 0 and T % tm == 0 and B + 1 <= 8
    n_ctx = Tc // tm
    n_lat = T // tm
    Tt = Tc + T
    ctx_row = B

    cols, main_idx, small_idx = _proj_layout(D)
    cos_t, sin_t = _rope_tables(Tc, T)
    xa = jnp.concatenate([x, ctx], axis=1)
    c_all = jnp.concatenate([c, c_ctx[None, :], jnp.zeros((8 - B - 1, D), F32)], axis=0)

    tq = _pick(T, (1024, 512, 256))
    n_sub = _pick(Tt // tm, (3, 2, 1))

    out = None
    for layer in range(depth):
        last = layer == depth - 1
        lam_init = 0.8 - 0.6 * math.exp(-0.3 * layer)
        w_main = w_in[layer][:, main_idx].astype(BF16)
        w_small = jnp.pad(w_in[layer][:, small_idx], ((0, 0), (0, 128 - small_idx.shape[0]))).astype(BF16)

        mod3 = _adaln(c_all, ada_w[layer], ada_b[layer]).reshape(8, N_MOD, D)
        h1 = _norm_mod(xa, mod3, norm1_g[layer], 0, n_lat, ctx_row).reshape(B * Tt, D)
        P = _matmul(h1, w_main, "in_proj").reshape(B, Tt, -1)
        Psm = _matmul(h1, w_small, "in_proj_small").reshape(B, Tt, 128)

        QT, Ks, VT = _qkv_prep(P, cols, cos_t, sin_t, da_q_norm_g[layer], da_k_norm_g[layer])
        da_l = _flash(QT, Ks, VT, da_lambda[layer], da_sub_norm_g[layer], lam_init, 0, T, 0, Tt, tq, n_sub)
        if last:
            da = da_l
        else:
            da_c = _flash(QT, Ks, VT, da_lambda[layer], da_sub_norm_g[layer], lam_init, T, Tc, T, Tc, tm, 1)
            da = jnp.concatenate([da_l, da_c], axis=1)

        hf = _lru_pass(P, cols, lru_conv_w[layer], lru_conv_b[layer], _lru_gate_weights(lru_gate_w[layer, 0]),
                       lru_gate_b[layer, 0].reshape(-1), lru_lambda[layer, 0], n_ctx, False)
        lru = _lru_pass(P, cols, lru_conv_w[layer], lru_conv_b[layer], _lru_gate_weights(lru_gate_w[layer, 1]),
                        lru_gate_b[layer, 1].reshape(-1), lru_lambda[layer, 1], n_ctx, True, hf=hf)

        rate_row = jnp.zeros((1, 128), F32).at[0, N_DIR * GDN_HEADS:2 * N_DIR * GDN_HEADS].set(
            gdn_A_log[layer].astype(F32).reshape(-1))
        dtb_row = jnp.zeros((1, 128), F32).at[0, N_DIR * GDN_HEADS:2 * N_DIR * GDN_HEADS].set(
            gdn_dt_bias[layer].astype(F32).reshape(-1))
        of = _gdn_pass(P, Psm, cols, gdn_conv_w[layer], rate_row, dtb_row, n_ctx, False, 0)
        gdn = _gdn_pass(P, Psm, cols, gdn_conv_w[layer], rate_row, dtb_row, n_ctx, True, 1, of=of,
                        norm_g=gdn_norm_g[layer])

        res = _post(xa, da, lru, gdn, P, mod3, norm2_g[layer], w_branch[layer].astype(BF16),
                    w_out[layer].astype(BF16), mlp_w1[layer].astype(BF16), mlp_w2[layer].astype(BF16),
                    n_lat, ctx_row, T if last else Tt)
        if last:
            out = res
        else:
            xa = res
    return out
```

```python
import functools
import math

import jax
import jax.numpy as jnp
from jax import lax
from jax.experimental import pallas as pl
from jax.experimental.pallas import tpu as pltpu

F32 = jnp.float32
BF16 = jnp.bfloat16
HIGHEST = lax.Precision.HIGHEST

N_MOD = 6
BRANCH_WIDTH = 512
N_BRANCH = 3
DA_HEADS = 4
DA_QK_DIM = 64
DA_V_DIM = 128
ROPE_BASE = 10000.0
LRU_BLOCKS = 8
LRU_C = 8.0
GDN_HEADS = 4
GDN_DK = 128
GDN_DV = 128
GDN_CHUNK = 64
N_DIR = 2
GRID_W = 64
EPS = 1e-6
NEG_BIG = -1e30

ROW_TILE = 256
HALO_ROWS = 8
VT_PAD = 16
GDN_HEAD_GROUPS = ((0, 1), (2, 3))
GDN_STAGE_SKEW = 2
VMEM_LIMIT = 48 * 1024 * 1024


def _cparams(sem):
    return pltpu.CompilerParams(dimension_semantics=sem, vmem_limit_bytes=VMEM_LIMIT)


def _softplus(z):
    return jnp.maximum(z, 0.0) + jnp.log1p(jnp.exp(-jnp.abs(z)))


def _adaln_kernel(c_ref, w_ref, b_ref, o_ref):
    c = c_ref[...]
    a = c * jax.nn.sigmoid(c)
    o_ref[...] = jnp.dot(a, w_ref[...], preferred_element_type=F32, precision=HIGHEST) + b_ref[...]


def _adaln(c_all, w, b):
    R, D = c_all.shape
    N = w.shape[1]
    tn = 1536 if N % 1536 == 0 else N
    return pl.pallas_call(
        _adaln_kernel,
        grid=(N // tn,),
        in_specs=[pl.BlockSpec((R, D), lambda j: (0, 0)),
                  pl.BlockSpec((D, tn), lambda j: (0, j)),
                  pl.BlockSpec((1, tn), lambda j: (0, j))],
        out_specs=pl.BlockSpec((R, tn), lambda j: (0, j)),
        out_shape=jax.ShapeDtypeStruct((R, N), F32),
        compiler_params=_cparams(("arbitrary",)),
        name="adaln",
    )(c_all, w, b.reshape(1, N))


def _pick(n, cands):
    for c in cands:
        if n % c == 0:
            return c
    return n


def _seg_mean(x2, s_ref):
    hi = x2.astype(BF16)
    lo = (x2 - hi.astype(F32)).astype(BF16)
    s = s_ref[...]
    return jnp.dot(hi, s, preferred_element_type=F32) + jnp.dot(lo, s, preferred_element_type=F32)


def _rope(y, cos, sin_signed, first_half):
    rot = jnp.where(first_half, pltpu.roll(y, 128 - 16, 1), pltpu.roll(y, 16, 1))
    return y * cos + rot * sin_signed


def _front_kernel(x_ref, mod_ref, g_ref, wm_ref, wq_ref, ws_ref, cos_ref, sin_ref, gq_ref, gk_ref, s_ref,
                  p_ref, ps_ref, qo_ref, ko_ref, vo_ref, *, q_scale, col_chunk):
    x = x_ref[0]
    ms = jnp.mean(x * x, axis=-1, keepdims=True)
    y = x * lax.rsqrt(ms + EPS) * g_ref[...]
    hb = (y * (1.0 + mod_ref[0, 1:2, :]) + mod_ref[0, 0:1, :]).astype(BF16)
    for c0 in range(0, wm_ref.shape[1], col_chunk):
        p_ref[0, :, c0:c0 + col_chunk] = jnp.dot(hb, wm_ref[:, c0:c0 + col_chunk], preferred_element_type=F32)
    ps_ref[0] = jnp.dot(hb, ws_ref[...], preferred_element_type=F32)
    qkv = jnp.dot(hb, wq_ref[...], preferred_element_type=F32)
    W = BRANCH_WIDTH

    cos = cos_ref[...]
    sin = sin_ref[...]
    tm = cos.shape[0]
    lane = lax.broadcasted_iota(jnp.int32, (tm, 128), 1)
    first_half = (lane % 32) < 16
    comp0 = lane < DA_QK_DIM

    q = qkv[:, :W]
    qn = q * lax.rsqrt(_seg_mean(q * q, s_ref) + EPS) * gq_ref[...]
    k = qkv[:, W:2 * W]
    kn = k * lax.rsqrt(_seg_mean(k * k, s_ref) + EPS) * gk_ref[...]
    v = qkv[:, 2 * W:]
    ones_row = jnp.where(lax.broadcasted_iota(jnp.int32, (VT_PAD, tm), 0) == 0, 1.0, 0.0).astype(BF16)
    for h in range(DA_HEADS):
        qh = _rope(qn[:, h * 128:(h + 1) * 128], cos, sin, first_half) * q_scale
        qo_ref[0, h * 128:(h + 1) * 128, :] = qh.T.astype(BF16)
        kh = _rope(kn[:, h * 128:(h + 1) * 128], cos, sin, first_half)
        ko_ref[0, :, h * 256:h * 256 + 128] = jnp.where(comp0, kh, 0.0).astype(BF16)
        ko_ref[0, :, h * 256 + 128:(h + 1) * 256] = jnp.where(comp0, 0.0, kh).astype(BF16)
        vo_ref[0, h, 0, 0:DA_V_DIM, :] = v[:, h * DA_V_DIM:(h + 1) * DA_V_DIM].T.astype(BF16)
        vo_ref[0, h, 0, DA_V_DIM:DA_V_DIM + VT_PAD, :] = ones_row


def _front(xa, mod3, g, w_main, w_qkv, w_small, cos_t, sin_t, gq, gk, n_lat_tiles, ctx_row):
    B, Tt, D = xa.shape
    tm = ROW_TILE
    W = BRANCH_WIDTH
    NM = w_main.shape[1]
    seg = (jnp.arange(W)[:, None] // DA_QK_DIM == jnp.arange(W)[None, :] // DA_QK_DIM)
    smat = (seg.astype(F32) / DA_QK_DIM).astype(BF16)
    gq_t = jnp.tile(gq, W // DA_QK_DIM).reshape(1, W)
    gk_t = jnp.tile(gk, W // DA_QK_DIM).reshape(1, W)
    q_scale = (DA_QK_DIM ** -0.5) * math.log2(math.e)
    once = pl.Buffered(1)
    const = lambda b, t: (0, 0)
    return pl.pallas_call(
        functools.partial(_front_kernel, q_scale=q_scale, col_chunk=_pick(NM, (1536, 1024, 512))),
        grid=(B, Tt // tm),
        in_specs=[pl.BlockSpec((1, tm, D), lambda b, t: (b, t, 0)),
                  pl.BlockSpec((1, N_MOD, D), lambda b, t: (jnp.where(t >= n_lat_tiles, ctx_row, b), 0, 0)),
                  pl.BlockSpec((1, D), const),
                  pl.BlockSpec((D, NM), const, pipeline_mode=once),
                  pl.BlockSpec((D, 3 * W), const, pipeline_mode=once),
                  pl.BlockSpec((D, 128), const, pipeline_mode=once),
                  pl.BlockSpec((tm, 128), lambda b, t: (t, 0)),
                  pl.BlockSpec((tm, 128), lambda b, t: (t, 0)),
                  pl.BlockSpec((1, W), const),
                  pl.BlockSpec((1, W), const),
                  pl.BlockSpec((W, W), const)],
        out_specs=[pl.BlockSpec((1, tm, NM), lambda b, t: (b, t, 0)),
                   pl.BlockSpec((1, tm, 128), lambda b, t: (b, t, 0)),
                   pl.BlockSpec((1, W, tm), lambda b, t: (b, 0, t)),
                   pl.BlockSpec((1, tm, 2 * W), lambda b, t: (b, t, 0)),
                   pl.BlockSpec((1, DA_HEADS, 1, DA_V_DIM + VT_PAD, tm), lambda b, t: (b, 0, t, 0, 0))],
        out_shape=[jax.ShapeDtypeStruct((B, Tt, NM), F32),
                   jax.ShapeDtypeStruct((B, Tt, 128), F32),
                   jax.ShapeDtypeStruct((B, W, Tt), BF16),
                   jax.ShapeDtypeStruct((B, Tt, 2 * W), BF16),
                   jax.ShapeDtypeStruct((B, DA_HEADS, Tt // tm, DA_V_DIM + VT_PAD, tm), BF16)],
        compiler_params=_cparams(("parallel", "parallel")),
        name="front",
    )(xa, mod3, g.reshape(1, D), w_main, w_qkv, w_small, cos_t, sin_t, gq_t, gk_t, smat)


def _flash_kernel(qt_ref, k_ref, vt_ref, lam_ref, subg_ref, o_ref, acc_s,
                  *, tq, sub, n_sub, nk, chunk0, lam_init):
    qt = qt_ref[0]
    acc_s[...] = jnp.zeros(acc_s.shape, F32)
    tk = sub * n_sub

    def scores(j):
        r0 = (chunk0 + j * n_sub) * sub
        kk = k_ref[0, r0:r0 + tk, :]
        k2 = jnp.concatenate([kk[:, :128], kk[:, 128:]], axis=0)
        return jnp.dot(k2, qt, preferred_element_type=F32)

    def accumulate(pend):
        j, ps, alphas = pend
        c0 = chunk0 + j * n_sub
        vt = jnp.concatenate([vt_ref[0, 0, c0 + i] for i in range(n_sub)], axis=1)
        for c in range(2):
            acc_s[c] = alphas[c] * acc_s[c] + jnp.dot(vt, ps[c], preferred_element_type=F32)

    ms = [jnp.full((1, tq), -jnp.inf, F32)] * 2
    st = scores(0)
    pend = None
    for j in range(nk):
        st_next = scores(j + 1) if j + 1 < nk else None
        if pend is not None:
            accumulate(pend)
        ps, alphas = [], []
        for c in range(2):
            s_c = st[c * tk:(c + 1) * tk]
            m_new = jnp.maximum(ms[c], jnp.max(s_c, axis=0, keepdims=True))
            alphas.append(jnp.exp2(ms[c] - m_new))
            ps.append(jnp.exp2(s_c - m_new).astype(BF16))
            ms[c] = m_new
        pend = (j, ps, alphas)
        st = st_next
    accumulate(pend)

    a0 = acc_s[0]
    a1 = acc_s[1]
    o0 = a0[:DA_V_DIM] / a0[DA_V_DIM:DA_V_DIM + 1]
    o1 = a1[:DA_V_DIM] / a1[DA_V_DIM:DA_V_DIM + 1]
    lv = lam_ref[...]
    s01 = jnp.sum(lv[0:1] * lv[1:2], axis=-1, keepdims=True)
    s23 = jnp.sum(lv[2:3] * lv[3:4], axis=-1, keepdims=True)
    lam = jnp.exp(s01) - jnp.exp(s23) + lam_init
    d = o0 - lam * o1
    ms_ = jnp.mean(d * d, axis=0, keepdims=True)
    o_ref[0] = (d * lax.rsqrt(ms_ + EPS)).T * subg_ref[...] * (1.0 - lam_init)


def _flash(QT, Ks, VT, lam_vec, sub_g, lam_init, q_row0, n_q_rows, k_row0, n_k_rows, tq, n_sub):
    B, _, Tt = QT.shape
    sub = VT.shape[-1]
    tk = sub * n_sub
    nq, nk = n_q_rows // tq, n_k_rows // tk
    q0 = q_row0 // tq
    vrows = VT.shape[3]
    return pl.pallas_call(
        functools.partial(_flash_kernel, tq=tq, sub=sub, n_sub=n_sub, nk=nk, chunk0=k_row0 // sub,
                          lam_init=lam_init),
        grid=(B, DA_HEADS, nq),
        in_specs=[pl.BlockSpec((1, 128, tq), lambda b, h, i: (b, h, q0 + i)),
                  pl.BlockSpec((1, Tt, 256), lambda b, h, i: (b, 0, h)),
                  pl.BlockSpec((1, 1, VT.shape[2], vrows, sub), lambda b, h, i: (b, h, 0, 0, 0)),
                  pl.BlockSpec((4, DA_QK_DIM), lambda b, h, i: (0, 0)),
                  pl.BlockSpec((1, DA_V_DIM), lambda b, h, i: (0, 0))],
        out_specs=pl.BlockSpec((1, tq, 128), lambda b, h, i: (b, i, h)),
        out_shape=jax.ShapeDtypeStruct((B, n_q_rows, BRANCH_WIDTH), F32),
        scratch_shapes=[pltpu.VMEM((2, vrows, tq), F32)],
        compiler_params=_cparams(("parallel", "parallel", "arbitrary")),
        name="diff_attn",
    )(QT, Ks, VT, lam_vec, sub_g.reshape(1, DA_V_DIM))


def _scan_tile(j, n_ctx, nt, reverse):
    n_lat = nt - n_ctx
    if not reverse:
        return jnp.where(j < n_ctx, n_lat + j, j - n_ctx)
    return jnp.where(j < n_ctx, nt - 1 - j, n_lat - 1 - (j - n_ctx))


def _stream_edges(t, n_ctx, nt):
    n_lat = nt - n_ctx
    return jnp.logical_or(t == 0, t == n_lat), jnp.logical_or(t == n_lat - 1, t == nt - 1)


def _dwconv4(x, prev8, next8, w_ref, is_first, is_last):
    tm = x.shape[0]
    row = lax.broadcasted_iota(jnp.int32, x.shape, 0)
    pf = jnp.where(is_first, 0.0, 1.0)
    nf = jnp.where(is_last, 0.0, 1.0)
    prow = prev8[HALO_ROWS - 1:HALO_ROWS, :] * pf
    n0 = next8[0:1, :] * nf
    n1 = next8[1:2, :] * nf
    xm1 = jnp.where(row == 0, prow, pltpu.roll(x, 1, 0))
    xp1 = jnp.where(row == tm - 1, n0, pltpu.roll(x, tm - 1, 0))
    xp2 = jnp.where(row == tm - 2, n0, jnp.where(row == tm - 1, n1, pltpu.roll(x, tm - 2, 0)))
    return w_ref[0:1, :] * xm1 + w_ref[1:2, :] * x + w_ref[2:3, :] * xp1 + w_ref[3:4, :] * xp2


def _halo_specs(tm, width, col_blk, n_ctx, nt, reverse):
    r8 = tm // HALO_ROWS
    last8 = nt * r8 - 1

    def cur(b, j):
        return (b, _scan_tile(j, n_ctx, nt, reverse), col_blk)

    def prev(b, j):
        return (b, jnp.maximum(_scan_tile(j, n_ctx, nt, reverse) * r8 - 1, 0), col_blk)

    def nxt(b, j):
        return (b, jnp.minimum((_scan_tile(j, n_ctx, nt, reverse) + 1) * r8, last8), col_blk)

    return [pl.BlockSpec((1, tm, width), cur),
            pl.BlockSpec((1, HALO_ROWS, width), prev),
            pl.BlockSpec((1, HALO_ROWS, width), nxt)]


def _linear_scan_tile(a, b, h0, reverse):
    tm = a.shape[0]
    G = HALO_ROWS
    sub = lax.broadcasted_iota(jnp.int32, a.shape, 0) % G
    s = 1
    while s < G:
        if not reverse:
            keep = sub >= s
            ap = jnp.where(keep, pltpu.roll(a, s, 0), 1.0)
            bp = jnp.where(keep, pltpu.roll(b, s, 0), 0.0)
        else:
            keep = sub < G - s
            ap = jnp.where(keep, pltpu.roll(a, tm - s, 0), 1.0)
            bp = jnp.where(keep, pltpu.roll(b, tm - s, 0), 0.0)
        b = a * bp + b
        a = a * ap
        s *= 2
    n_groups = tm // G
    outs = [None] * n_groups
    carry = h0
    for g in (range(n_groups - 1, -1, -1) if reverse else range(n_groups)):
        hg = b[g * G:(g + 1) * G] + a[g * G:(g + 1) * G] * carry
        outs[g] = hg
        carry = hg[0:1] if reverse else hg[G - 1:G]
    return jnp.concatenate(outs, axis=0), carry


def _lru_kernel(*refs, n_ctx, nt, reverse, final):
    if final:
        (x_ref, xp_ref, xn_ref, y_ref, hf_ref, cw_ref, cb_ref, wg_ref, gb_ref, lam_ref, o_ref, carry) = refs
    else:
        (x_ref, xp_ref, xn_ref, cw_ref, cb_ref, wg_ref, gb_ref, lam_ref, o_ref, carry) = refs
    j = pl.program_id(1)
    t = _scan_tile(j, n_ctx, nt, reverse)
    is_first, is_last = _stream_edges(t, n_ctx, nt)

    @pl.when(j == 0)
    def _():
        carry[...] = jnp.zeros(carry.shape, F32)

    x = x_ref[0]
    tm, W = x.shape
    xc = _dwconv4(x, xp_ref[0], xn_ref[0], cw_ref, is_first, is_last) + cb_ref[...]
    g = jnp.dot(xc.astype(BF16), wg_ref[...], preferred_element_type=F32) + gb_ref[...]
    r = jax.nn.sigmoid(g[:, :W])
    i = jax.nn.sigmoid(g[:, W:])
    log_a = (-LRU_C) * r * _softplus(-lam_ref[...])
    a = jnp.exp(log_a)
    th = jnp.tanh(log_a)
    bb = jnp.sqrt(-2.0 * th / (1.0 - th)) * (i * xc)
    h, h_last = _linear_scan_tile(a, bb, carry[0:1, :], reverse)
    carry[0:1, :] = h_last
    if final:
        o_ref[0] = jax.nn.gelu(y_ref[0]) * (hf_ref[0] + h)
    else:
        o_ref[0] = h


def _lru_pass(P, cols, conv_w, conv_b, wg, gb, lam, n_ctx, reverse, hf=None):
    B, Tt, _ = P.shape
    tm = ROW_TILE
    W = BRANCH_WIDTH
    nt = Tt // tm
    final = hf is not None

    def cur(b, j):
        return (b, _scan_tile(j, n_ctx, nt, reverse), 0)

    in_specs = _halo_specs(tm, W, cols["lx"] // W, n_ctx, nt, reverse)
    args = [P, P, P]
    if final:
        yb = cols["ly"] // W
        in_specs += [pl.BlockSpec((1, tm, W), lambda b, j: (b, _scan_tile(j, n_ctx, nt, reverse), yb)),
                     pl.BlockSpec((1, tm, W), cur)]
        args += [P, hf]
    in_specs += [pl.BlockSpec((4, W), lambda b, j: (0, 0)),
                 pl.BlockSpec((1, W), lambda b, j: (0, 0)),
                 pl.BlockSpec((W, 2 * W), lambda b, j: (0, 0)),
                 pl.BlockSpec((1, 2 * W), lambda b, j: (0, 0)),
                 pl.BlockSpec((1, W), lambda b, j: (0, 0))]
    args += [conv_w, conv_b.reshape(1, W), wg, gb.reshape(1, 2 * W), lam.reshape(1, W)]
    return pl.pallas_call(
        functools.partial(_lru_kernel, n_ctx=n_ctx, nt=nt, reverse=reverse, final=final),
        grid=(B, nt),
        in_specs=in_specs,
        out_specs=pl.BlockSpec((1, tm, W), cur),
        out_shape=jax.ShapeDtypeStruct((B, Tt, W), F32),
        scratch_shapes=[pltpu.VMEM((HALO_ROWS, W), F32)],
        compiler_params=_cparams(("parallel", "arbitrary")),
        name="rglru_bwd" if reverse else "rglru_fwd",
    )(*args)


def _lru_gate_weights(gate_w_d):
    mats = []
    for g in range(2):
        mats.append(jax.scipy.linalg.block_diag(*[gate_w_d[g, n] for n in range(LRU_BLOCKS)]))
    return jnp.concatenate(mats, axis=1).astype(BF16)


def _split_bf16(x):
    hi = x.astype(BF16)
    lo = (x - hi.astype(F32)).astype(BF16)
    return hi, lo


def _dot3(a_hi, a_lo, b_hi, b_lo):
    return (jnp.dot(a_hi, b_hi, preferred_element_type=F32)
            + (jnp.dot(a_hi, b_lo, preferred_element_type=F32) + jnp.dot(a_lo, b_hi, preferred_element_type=F32)))


def _gdn_kernel(*refs, n_ctx, nt, reverse, final, d):
    if final:
        (x_ref, xp_ref, xn_ref, sm_ref, z_ref, of_ref, cw_ref, rate_ref, dtb_ref, ng_ref, o_ref, S_s, qkv_s) = refs
    else:
        (x_ref, xp_ref, xn_ref, sm_ref, cw_ref, rate_ref, dtb_ref, o_ref, S_s, qkv_s) = refs
    j = pl.program_id(1)
    t = _scan_tile(j, n_ctx, nt, reverse)
    is_first, is_last = _stream_edges(t, n_ctx, nt)

    @pl.when(j == 0)
    def _():
        S_s[...] = jnp.zeros(S_s.shape, F32)

    x = x_ref[0]
    tm = x.shape[0]
    C = GDN_CHUNK
    nC = tm // C
    qkv = _dwconv4(x, xp_ref[0], xn_ref[0], cw_ref, is_first, is_last)
    qkv_s[...] = qkv * jax.nn.sigmoid(qkv)

    sm = sm_ref[0]
    beta_all = jax.nn.sigmoid(sm)
    g_all = -jnp.exp(rate_ref[...]) * _softplus(sm + dtb_ref[...])

    ii = lax.broadcasted_iota(jnp.int32, (tm, tm), 0)
    jj = lax.broadcasted_iota(jnp.int32, (tm, tm), 1)
    same = (ii // C) == (jj // C)
    if not reverse:
        incl = jnp.logical_and(same, ii >= jj)
        strict = jnp.logical_and(same, ii > jj)
    else:
        incl = jnp.logical_and(same, ii <= jj)
        strict = jnp.logical_and(same, ii < jj)
    incl_b = incl.astype(F32).astype(BF16)
    eye = (ii == jj).astype(F32)

    g1 = g_all.astype(BF16)
    g2 = (g_all - g1.astype(F32)).astype(BF16)
    g3 = (g_all - g1.astype(F32) - g2.astype(F32)).astype(BF16)
    gc_all = (jnp.dot(incl_b, g1, preferred_element_type=F32)
              + (jnp.dot(incl_b, g2, preferred_element_type=F32) + jnp.dot(incl_b, g3, preferred_element_type=F32)))
    last = 0 if reverse else C - 1
    gtot_all = jnp.concatenate(
        [jnp.broadcast_to(gc_all[c * C + last:c * C + last + 1, :], (C, 128)) for c in range(nC)], axis=0)

    groups = [_gdn_group_stages(hs, d=d, tm=tm, reverse=reverse, final=final, qkv_s=qkv_s, S_s=S_s, o_ref=o_ref,
                                of_ref=of_ref if final else None, z_ref=z_ref if final else None,
                                ng_ref=ng_ref if final else None, beta_all=beta_all, gc_all=gc_all,
                                gtot_all=gtot_all, incl=incl, strict=strict, eye=eye)
              for hs in GDN_HEAD_GROUPS]
    _run_skewed(groups, GDN_STAGE_SKEW)


def _run_skewed(gens, skew):
    done = [False] * len(gens)
    rnd = 0
    while not all(done):
        for k, g in enumerate(gens):
            if done[k] or rnd < k * skew:
                continue
            try:
                next(g)
            except StopIteration:
                done[k] = True
        rnd += 1


def _gdn_group_stages(heads, *, d, tm, reverse, final, qkv_s, S_s, o_ref, of_ref, z_ref, ng_ref, beta_all, gc_all,
                      gtot_all, incl, strict, eye):
    C = GDN_CHUNK
    nC = tm // C
    HK = GDN_HEADS * GDN_DK
    rhs, q_dec, k_dec, g_end, m_k, attn = {}, {}, {}, {}, {}, {}
    for h in heads:
        q = qkv_s[:, h * GDN_DK:(h + 1) * GDN_DK]
        k = qkv_s[:, HK + h * GDN_DK:HK + (h + 1) * GDN_DK]
        v = qkv_s[:, 2 * HK + h * GDN_DV:2 * HK + (h + 1) * GDN_DV]
        qn = q * lax.rsqrt(jnp.sum(q * q, axis=-1, keepdims=True) + EPS) * (GDN_DK ** -0.5)
        kn = k * lax.rsqrt(jnp.sum(k * k, axis=-1, keepdims=True) + EPS)
        cb = d * GDN_HEADS + h
        cg = N_DIR * GDN_HEADS + d * GDN_HEADS + h
        beta_b = jnp.broadcast_to(beta_all[:, cb:cb + 1], (tm, 128))
        gc_b = jnp.broadcast_to(gc_all[:, cg:cg + 1], (tm, 128))
        gtot_b = jnp.broadcast_to(gtot_all[:, cg:cg + 1], (tm, 128))
        eg = jnp.exp(gc_b)
        kb = kn * beta_b
        rhs[h] = jnp.concatenate([v * beta_b, kb * eg], axis=1).astype(BF16)
        q_dec[h] = qn * eg
        k_dec[h] = kn * jnp.exp(gtot_b - gc_b)
        g_end[h] = jnp.exp(gtot_b)
        gc_cols = jnp.concatenate([gc_b] * (tm // 128), axis=1)
        gc_rows = gc_b.T[0:1, :]
        decay = jnp.exp(jnp.where(incl, gc_cols - gc_rows, NEG_BIG))
        kq = lax.dot_general(jnp.concatenate([kb, qn], axis=0).astype(BF16), kn.astype(BF16),
                             (((1,), (1,)), ((), ())), preferred_element_type=F32)
        m_k[h] = jnp.where(strict, -(kq[:tm] * decay), 0.0)
        attn[h] = (kq[tm:] * decay).astype(BF16)
    yield

    t_inv = {h: eye + m_k[h] for h in heads}
    p_k = {}
    for h in heads:
        m_hi, m_lo = _split_bf16(m_k[h])
        p_k[h] = _dot3(m_hi, m_lo, m_hi, m_lo)
    yield
    for _ in range(4):
        for h in heads:
            p_hi, p_lo = _split_bf16(p_k[h])
            t_hi, t_lo = _split_bf16(t_inv[h])
            r2 = _dot3(jnp.concatenate([t_hi, p_hi], axis=0), jnp.concatenate([t_lo, p_lo], axis=0), p_hi, p_lo)
            t_inv[h] = t_inv[h] + r2[:tm]
            p_k[h] = r2[tm:]
        yield
    u, w = {}, {}
    for h in heads:
        p_hi, p_lo = _split_bf16(p_k[h])
        t_hi, t_lo = _split_bf16(t_inv[h])
        t_fin = t_inv[h] + _dot3(t_hi, t_lo, p_hi, p_lo)
        sol = jnp.dot(t_fin.astype(BF16), rhs[h], preferred_element_type=F32)
        u[h] = sol[:, :GDN_DV]
        w[h] = sol[:, GDN_DV:]
    yield

    q_eff, o_free, kw = {}, {}, {}
    for h in heads:
        wu = jnp.concatenate([w[h], u[h]], axis=1).astype(BF16)
        aw = jnp.dot(attn[h], wu, preferred_element_type=F32)
        q_eff[h] = (q_dec[h] - aw[:, :GDN_DV]).astype(BF16)
        o_free[h] = aw[:, GDN_DV:]
        kd = k_dec[h].astype(BF16)
        kw[h] = [lax.dot_general(kd[c * C:(c + 1) * C], wu[c * C:(c + 1) * C], (((0,), (0,)), ((), ())),
                                 preferred_element_type=F32) for c in range(nC)]
    yield

    S = {h: S_s[h] for h in heads}
    for cc in range(nC):
        c = nC - 1 - cc if reverse else cc
        r0 = c * C
        for h in heads:
            s_b = S[h].astype(BF16)
            o_c = o_free[h][r0:r0 + C] + jnp.dot(q_eff[h][r0:r0 + C], s_b, preferred_element_type=F32)
            S[h] = (S[h] * _col(g_end[h][r0:r0 + 1, :], GDN_DK) + kw[h][c][:, GDN_DV:]
                    - jnp.dot(kw[h][c][:, :GDN_DV].astype(BF16), s_b, preferred_element_type=F32))
            if final:
                o_t = of_ref[0, r0:r0 + C, h * GDN_DV:(h + 1) * GDN_DV] + o_c
                ms = jnp.mean(o_t * o_t, axis=-1, keepdims=True)
                z = z_ref[0, r0:r0 + C, h * GDN_DV:(h + 1) * GDN_DV]
                o_ref[0, r0:r0 + C, h * GDN_DV:(h + 1) * GDN_DV] = (
                    o_t * lax.rsqrt(ms + EPS) * ng_ref[...] * (z * jax.nn.sigmoid(z)))
            else:
                o_ref[0, r0:r0 + C, h * GDN_DV:(h + 1) * GDN_DV] = o_c
        yield
    for h in heads:
        S_s[h] = S[h]


def _col(row_vec, n):
    return jnp.broadcast_to(row_vec, (n, row_vec.shape[1]))


def _gdn_pass(P, Psm, cols, conv_w, rate_row, dtb_row, n_ctx, reverse, d, of=None, norm_g=None):
    B, Tt, _ = P.shape
    tm = ROW_TILE
    W = BRANCH_WIDTH
    QW = GDN_HEADS * (2 * GDN_DK + GDN_DV)
    nt = Tt // tm
    final = of is not None

    def cur(b, j):
        return (b, _scan_tile(j, n_ctx, nt, reverse), 0)

    in_specs = _halo_specs(tm, QW, cols["gqkv"] // QW, n_ctx, nt, reverse)
    in_specs += [pl.BlockSpec((1, tm, 128), cur)]
    args = [P, P, P, Psm]
    if final:
        zb = cols["gz"] // W
        in_specs += [pl.BlockSpec((1, tm, W), lambda b, j: (b, _scan_tile(j, n_ctx, nt, reverse), zb)),
                     pl.BlockSpec((1, tm, W), cur)]
        args += [P, of]
    in_specs += [pl.BlockSpec((4, QW), lambda b, j: (0, 0)),
                 pl.BlockSpec((1, 128), lambda b, j: (0, 0)),
                 pl.BlockSpec((1, 128), lambda b, j: (0, 0))]
    args += [conv_w, rate_row, dtb_row]
    if final:
        in_specs += [pl.BlockSpec((1, GDN_DV), lambda b, j: (0, 0))]
        args += [norm_g.reshape(1, GDN_DV)]
    return pl.pallas_call(
        functools.partial(_gdn_kernel, n_ctx=n_ctx, nt=nt, reverse=reverse, final=final, d=d),
        grid=(B, nt),
        in_specs=in_specs,
        out_specs=pl.BlockSpec((1, tm, W), cur),
        out_shape=jax.ShapeDtypeStruct((B, Tt, W), F32),
        scratch_shapes=[pltpu.VMEM((GDN_HEADS, GDN_DK, GDN_DV), F32), pltpu.VMEM((tm, QW), F32)],
        compiler_params=_cparams(("parallel", "arbitrary")),
        name="gdn_bwd" if reverse else "gdn_fwd",
    )(*args)


def _post_kernel(x_ref, da_ref, lru_ref, gdn_ref, g0_ref, g1_ref, g2_ref, mod_ref, n2_ref, wb_ref, wo_ref,
                 w1_ref, w2_ref, o_ref, *, ff_chunk):
    outs = (da_ref, lru_ref, gdn_ref)
    gates = (g0_ref, g1_ref, g2_ref)
    merged = None
    for i in range(N_BRANCH):
        y = jnp.dot(outs[i][0].astype(BF16), wb_ref[i], preferred_element_type=F32)
        term = jax.nn.sigmoid(gates[i][0]) * y
        merged = term if merged is None else merged + term
    proj = jnp.dot(merged.astype(BF16), wo_ref[...], preferred_element_type=F32)
    x = x_ref[0] + mod_ref[0, 2:3, :] * proj

    ms = jnp.mean(x * x, axis=-1, keepdims=True)
    y = x * lax.rsqrt(ms + EPS) * n2_ref[...]
    h = (y * (1.0 + mod_ref[0, 4:5, :]) + mod_ref[0, 3:4, :]).astype(BF16)
    n_ff = w1_ref.shape[1]
    acc = None
    for c0 in range(0, n_ff, ff_chunk):
        a = jnp.dot(h, w1_ref[:, c0:c0 + ff_chunk], preferred_element_type=F32)
        a = jnp.square(jnp.maximum(a, 0.0)).astype(BF16)
        part = jnp.dot(a, w2_ref[c0:c0 + ff_chunk, :], preferred_element_type=F32)
        acc = part if acc is None else acc + part
    o_ref[0] = x + mod_ref[0, 5:6, :] * acc


def _post(xa, da, lru, gdn, P, mod3, n2_g, wb, wo, w1, w2, n_lat_tiles, ctx_row, n_rows):
    B, Tt, D = xa.shape
    tm = ROW_TILE
    W = BRANCH_WIDTH
    F = w1.shape[1]
    row = lambda b, t: (b, t, 0)
    once = pl.Buffered(1)
    return pl.pallas_call(
        functools.partial(_post_kernel, ff_chunk=min(F, 1024)),
        grid=(B, n_rows // tm),
        in_specs=[pl.BlockSpec((1, tm, D), row),
                  pl.BlockSpec((1, tm, W), row), pl.BlockSpec((1, tm, W), row), pl.BlockSpec((1, tm, W), row),
                  pl.BlockSpec((1, tm, D), lambda b, t: (b, t, 0)),
                  pl.BlockSpec((1, tm, D), lambda b, t: (b, t, 1)),
                  pl.BlockSpec((1, tm, D), lambda b, t: (b, t, 2)),
                  pl.BlockSpec((1, N_MOD, D), lambda b, t: (jnp.where(t >= n_lat_tiles, ctx_row, b), 0, 0)),
                  pl.BlockSpec((1, D), lambda b, t: (0, 0)),
                  pl.BlockSpec((N_BRANCH, W, D), lambda b, t: (0, 0, 0), pipeline_mode=once),
                  pl.BlockSpec((D, D), lambda b, t: (0, 0), pipeline_mode=once),
                  pl.BlockSpec((D, F), lambda b, t: (0, 0), pipeline_mode=once),
                  pl.BlockSpec((F, D), lambda b, t: (0, 0), pipeline_mode=once)],
        out_specs=pl.BlockSpec((1, tm, D), row),
        out_shape=jax.ShapeDtypeStruct((B, n_rows, D), F32),
        compiler_params=_cparams(("parallel", "parallel")),
        name="merge_mlp",
    )(xa, da, lru, gdn, P, P, P, mod3, n2_g.reshape(1, D), wb, wo, w1, w2)


def _proj_layout(D):
    W = BRANCH_WIDTH
    QW = GDN_HEADS * (2 * GDN_DK + GDN_DV)
    src = {}
    off = 0
    for name, width in (("q", W), ("k", W), ("v", W), ("lx", W), ("ly", W), ("gqkv", QW), ("gz", W),
                        ("gb", N_DIR * GDN_HEADS), ("ga", N_DIR * GDN_HEADS), ("gate", N_BRANCH * D)):
        src[name] = (off, width)
        off += width
    order = ("gate", "gqkv", "lx", "ly", "gz")
    cols = {}
    idx = []
    o = 0
    for name in order:
        s, w = src[name]
        cols[name] = o
        idx.append(jnp.arange(s, s + w))
        o += w
    qkv_idx = jnp.concatenate([jnp.arange(src[n][0], src[n][0] + src[n][1]) for n in ("q", "k", "v")])
    small_idx = jnp.concatenate([jnp.arange(src["gb"][0], src["gb"][0] + src["gb"][1]),
                                 jnp.arange(src["ga"][0], src["ga"][0] + src["ga"][1])])
    return cols, jnp.concatenate(idx), qkv_idx, small_idx


def _rope_tables(n_ctx_rows, n_lat_rows):
    n_freq = DA_QK_DIM // 4
    inv = ROPE_BASE ** (-jnp.arange(n_freq, dtype=F32) / n_freq)
    tpos = jnp.arange(n_lat_rows, dtype=jnp.int32)
    ang_r = (tpos // GRID_W).astype(F32)[:, None] * inv
    ang_c = (tpos % GRID_W).astype(F32)[:, None] * inv
    ang = jnp.concatenate([ang_r, ang_r, ang_c, ang_c], axis=-1)
    ang = jnp.concatenate([ang, ang], axis=-1)
    sign = jnp.where((jnp.arange(128) % 32) < 16, -1.0, 1.0).astype(F32)
    cos = jnp.concatenate([jnp.cos(ang), jnp.ones((n_ctx_rows, 128), F32)], axis=0)
    sin = jnp.concatenate([jnp.sin(ang) * sign, jnp.zeros((n_ctx_rows, 128), F32)], axis=0)
    return cos, sin


def kernel(x, c, ctx, c_ctx, ada_w, ada_b, norm1_g, norm2_g, w_in, da_q_norm_g, da_k_norm_g, da_lambda, da_sub_norm_g, lru_conv_w, lru_conv_b, lru_gate_w, lru_gate_b, lru_lambda, gdn_conv_w, gdn_A_log, gdn_dt_bias, gdn_norm_g, w_branch, w_out, mlp_w1, mlp_w2):
    B, T, D = x.shape
    Tc = ctx.shape[1]
    depth = ada_w.shape[0]
    tm = ROW_TILE
    assert Tc % tm == 0 and T % tm == 0 and B + 1 <= 8
    n_ctx = Tc // tm
    n_lat = T // tm
    Tt = Tc + T
    ctx_row = B

    cols, main_idx, qkv_idx, small_idx = _proj_layout(D)
    cos_t, sin_t = _rope_tables(Tc, T)
    xa = jnp.concatenate([x, ctx], axis=1)
    c_all = jnp.concatenate([c, c_ctx[None, :], jnp.zeros((8 - B - 1, D), F32)], axis=0)

    tq = _pick(T, (1024, 512, 256))
    n_sub = _pick(Tt // tm, (3, 2, 1))

    out = None
    for layer in range(depth):
        last = layer == depth - 1
        lam_init = 0.8 - 0.6 * math.exp(-0.3 * layer)
        w_main = w_in[layer][:, main_idx].astype(BF16)
        w_qkv = w_in[layer][:, qkv_idx].astype(BF16)
        w_small = jnp.pad(w_in[layer][:, small_idx], ((0, 0), (0, 128 - small_idx.shape[0]))).astype(BF16)

        mod3 = _adaln(c_all, ada_w[layer], ada_b[layer]).reshape(8, N_MOD, D)
        P, Psm, QT, Ks, VT = _front(xa, mod3, norm1_g[layer], w_main, w_qkv, w_small, cos_t, sin_t,
                                    da_q_norm_g[layer], da_k_norm_g[layer], n_lat, ctx_row)

        da_l = _flash(QT, Ks, VT, da_lambda[layer], da_sub_norm_g[layer], lam_init, 0, T, 0, Tt, tq, n_sub)
        if last:
            da = da_l
        else:
            da_c = _flash(QT, Ks, VT, da_lambda[layer], da_sub_norm_g[layer], lam_init, T, Tc, T, Tc, tm, 1)
            da = jnp.concatenate([da_l, da_c], axis=1)

        hf = _lru_pass(P, cols, lru_conv_w[layer], lru_conv_b[layer], _lru_gate_weights(lru_gate_w[layer, 0]),
                       lru_gate_b[layer, 0].reshape(-1), lru_lambda[layer, 0], n_ctx, False)
        lru = _lru_pass(P, cols, lru_conv_w[layer], lru_conv_b[layer], _lru_gate_weights(lru_gate_w[layer, 1]),
                        lru_gate_b[layer, 1].reshape(-1), lru_lambda[layer, 1], n_ctx, True, hf=hf)

        rate_row = jnp.zeros((1, 128), F32).at[0, N_DIR * GDN_HEADS:2 * N_DIR * GDN_HEADS].set(
            gdn_A_log[layer].astype(F32).reshape(-1))
        dtb_row = jnp.zeros((1, 128), F32).at[0, N_DIR * GDN_HEADS:2 * N_DIR * GDN_HEADS].set(
            gdn_dt_bias[layer].astype(F32).reshape(-1))
        of = _gdn_pass(P, Psm, cols, gdn_conv_w[layer], rate_row, dtb_row, n_ctx, False, 0)
        gdn = _gdn_pass(P, Psm, cols, gdn_conv_w[layer], rate_row, dtb_row, n_ctx, True, 1, of=of,
                        norm_g=gdn_norm_g[layer])

        res = _post(xa, da, lru, gdn, P, mod3, norm2_g[layer], w_branch[layer].astype(BF16),
                    w_out[layer].astype(BF16), mlp_w1[layer].astype(BF16), mlp_w2[layer].astype(BF16),
                    n_lat, ctx_row, T if last else Tt)
        if last:
            out = res
        else:
            xa = res
    return out
```

```python
import functools
import math

import jax
import jax.numpy as jnp
from jax import lax
from jax.experimental import pallas as pl
from jax.experimental.pallas import tpu as pltpu

F32 = jnp.float32
BF16 = jnp.bfloat16
HIGHEST = lax.Precision.HIGHEST

N_MOD = 6
BRANCH_WIDTH = 512
N_BRANCH = 3
DA_HEADS = 4
DA_QK_DIM = 64
DA_V_DIM = 128
ROPE_BASE = 10000.0
LRU_BLOCKS = 8
LRU_C = 8.0
GDN_HEADS = 4
GDN_DK = 128
GDN_DV = 128
GDN_CHUNK = 64
N_DIR = 2
GRID_W = 64
EPS = 1e-6
NEG_BIG = -1e30

ROW_TILE = 256
HALO_ROWS = 8
VT_PAD = 16
GDN_HEAD_GROUPS = ((0, 1), (2, 3))
GDN_STAGE_SKEW = 2
VMEM_LIMIT = 48 * 1024 * 1024


def _cparams(sem):
    return pltpu.CompilerParams(dimension_semantics=sem, vmem_limit_bytes=VMEM_LIMIT)


def _softplus(z):
    return jnp.maximum(z, 0.0) + jnp.log1p(jnp.exp(-jnp.abs(z)))


def _adaln_kernel(c_ref, w_ref, b_ref, o_ref):
    c = c_ref[...]
    a = c * jax.nn.sigmoid(c)
    o_ref[...] = jnp.dot(a, w_ref[...], preferred_element_type=F32, precision=HIGHEST) + b_ref[...]


def _adaln(c_all, w, b):
    R, D = c_all.shape
    N = w.shape[1]
    tn = 1536 if N % 1536 == 0 else N
    return pl.pallas_call(
        _adaln_kernel,
        grid=(N // tn,),
        in_specs=[pl.BlockSpec((R, D), lambda j: (0, 0)),
                  pl.BlockSpec((D, tn), lambda j: (0, j)),
                  pl.BlockSpec((1, tn), lambda j: (0, j))],
        out_specs=pl.BlockSpec((R, tn), lambda j: (0, j)),
        out_shape=jax.ShapeDtypeStruct((R, N), F32),
        compiler_params=_cparams(("arbitrary",)),
        name="adaln",
    )(c_all, w, b.reshape(1, N))


def _pick(n, cands):
    for c in cands:
        if n % c == 0:
            return c
    return n


def _seg_mean(x2, s_ref):
    hi = x2.astype(BF16)
    lo = (x2 - hi.astype(F32)).astype(BF16)
    s = s_ref[...]
    return jnp.dot(hi, s, preferred_element_type=F32) + jnp.dot(lo, s, preferred_element_type=F32)


def _rope(y, cos, sin_signed, first_half):
    rot = jnp.where(first_half, pltpu.roll(y, 128 - 16, 1), pltpu.roll(y, 16, 1))
    return y * cos + rot * sin_signed


def _front_kernel(x_ref, mod_ref, g_ref, wm_ref, wq_ref, ws_ref, cos_ref, sin_ref, gq_ref, gk_ref, s_ref,
                  p_ref, ps_ref, qo_ref, ko_ref, vo_ref, *, q_scale, col_chunk):
    x = x_ref[0]
    ms = jnp.mean(x * x, axis=-1, keepdims=True)
    y = x * lax.rsqrt(ms + EPS) * g_ref[...]
    hb = (y * (1.0 + mod_ref[0, 1:2, :]) + mod_ref[0, 0:1, :]).astype(BF16)
    qkv = jnp.dot(hb, wq_ref[...], preferred_element_type=F32)
    main_chunks = list(range(0, wm_ref.shape[1], col_chunk))

    def project(c0):
        p_ref[0, :, c0:c0 + col_chunk] = jnp.dot(hb, wm_ref[:, c0:c0 + col_chunk], preferred_element_type=F32)

    project(main_chunks.pop(0))
    W = BRANCH_WIDTH

    cos = cos_ref[...]
    sin = sin_ref[...]
    tm = cos.shape[0]
    lane = lax.broadcasted_iota(jnp.int32, (tm, 128), 1)
    first_half = (lane % 32) < 16
    comp0 = lane < DA_QK_DIM

    q = qkv[:, :W]
    qn = q * lax.rsqrt(_seg_mean(q * q, s_ref) + EPS) * gq_ref[...]
    k = qkv[:, W:2 * W]
    kn = k * lax.rsqrt(_seg_mean(k * k, s_ref) + EPS) * gk_ref[...]
    v = qkv[:, 2 * W:]
    ones_row = jnp.where(lax.broadcasted_iota(jnp.int32, (VT_PAD, tm), 0) == 0, 1.0, 0.0).astype(BF16)
    for h in range(DA_HEADS):
        qh = _rope(qn[:, h * 128:(h + 1) * 128], cos, sin, first_half) * q_scale
        qo_ref[0, h * 128:(h + 1) * 128, :] = qh.T.astype(BF16)
        kh = _rope(kn[:, h * 128:(h + 1) * 128], cos, sin, first_half)
        ko_ref[0, :, h * 256:h * 256 + 128] = jnp.where(comp0, kh, 0.0).astype(BF16)
        ko_ref[0, :, h * 256 + 128:(h + 1) * 256] = jnp.where(comp0, 0.0, kh).astype(BF16)
        vo_ref[0, h, 0, 0:DA_V_DIM, :] = v[:, h * DA_V_DIM:(h + 1) * DA_V_DIM].T.astype(BF16)
        vo_ref[0, h, 0, DA_V_DIM:DA_V_DIM + VT_PAD, :] = ones_row
        if main_chunks:
            project(main_chunks.pop(0))
    for c0 in main_chunks:
        project(c0)
    ps_ref[0] = jnp.dot(hb, ws_ref[...], preferred_element_type=F32)


def _front(xa, mod3, g, w_main, w_qkv, w_small, cos_t, sin_t, gq, gk, n_lat_tiles, ctx_row):
    B, Tt, D = xa.shape
    tm = ROW_TILE
    W = BRANCH_WIDTH
    NM = w_main.shape[1]
    seg = (jnp.arange(W)[:, None] // DA_QK_DIM == jnp.arange(W)[None, :] // DA_QK_DIM)
    smat = (seg.astype(F32) / DA_QK_DIM).astype(BF16)
    gq_t = jnp.tile(gq, W // DA_QK_DIM).reshape(1, W)
    gk_t = jnp.tile(gk, W // DA_QK_DIM).reshape(1, W)
    q_scale = (DA_QK_DIM ** -0.5) * math.log2(math.e)
    once = pl.Buffered(1)
    const = lambda b, t: (0, 0)
    return pl.pallas_call(
        functools.partial(_front_kernel, q_scale=q_scale, col_chunk=_pick(NM, (1536, 1024, 512))),
        grid=(B, Tt // tm),
        in_specs=[pl.BlockSpec((1, tm, D), lambda b, t: (b, t, 0)),
                  pl.BlockSpec((1, N_MOD, D), lambda b, t: (jnp.where(t >= n_lat_tiles, ctx_row, b), 0, 0)),
                  pl.BlockSpec((1, D), const),
                  pl.BlockSpec((D, NM), const, pipeline_mode=once),
                  pl.BlockSpec((D, 3 * W), const, pipeline_mode=once),
                  pl.BlockSpec((D, 128), const, pipeline_mode=once),
                  pl.BlockSpec((tm, 128), lambda b, t: (t, 0)),
                  pl.BlockSpec((tm, 128), lambda b, t: (t, 0)),
                  pl.BlockSpec((1, W), const),
                  pl.BlockSpec((1, W), const),
                  pl.BlockSpec((W, W), const)],
        out_specs=[pl.BlockSpec((1, tm, NM), lambda b, t: (b, t, 0)),
                   pl.BlockSpec((1, tm, 128), lambda b, t: (b, t, 0)),
                   pl.BlockSpec((1, W, tm), lambda b, t: (b, 0, t)),
                   pl.BlockSpec((1, tm, 2 * W), lambda b, t: (b, t, 0)),
                   pl.BlockSpec((1, DA_HEADS, 1, DA_V_DIM + VT_PAD, tm), lambda b, t: (b, 0, t, 0, 0))],
        out_shape=[jax.ShapeDtypeStruct((B, Tt, NM), F32),
                   jax.ShapeDtypeStruct((B, Tt, 128), F32),
                   jax.ShapeDtypeStruct((B, W, Tt), BF16),
                   jax.ShapeDtypeStruct((B, Tt, 2 * W), BF16),
                   jax.ShapeDtypeStruct((B, DA_HEADS, Tt // tm, DA_V_DIM + VT_PAD, tm), BF16)],
        compiler_params=_cparams(("parallel", "parallel")),
        name="front",
    )(xa, mod3, g.reshape(1, D), w_main, w_qkv, w_small, cos_t, sin_t, gq_t, gk_t, smat)


def _flash_kernel(qt_ref, k_ref, vt_ref, lam_ref, subg_ref, o_ref, acc_s,
                  *, tq, sub, n_sub, nk, chunk0, lam_init):
    qt = qt_ref[0]
    acc_s[...] = jnp.zeros(acc_s.shape, F32)
    tk = sub * n_sub

    def scores(j):
        r0 = (chunk0 + j * n_sub) * sub
        kk = k_ref[0, r0:r0 + tk, :]
        k2 = jnp.concatenate([kk[:, :128], kk[:, 128:]], axis=0)
        return jnp.dot(k2, qt, preferred_element_type=F32)

    def accumulate(pend):
        j, ps, alphas = pend
        c0 = chunk0 + j * n_sub
        vt = jnp.concatenate([vt_ref[0, 0, c0 + i] for i in range(n_sub)], axis=1)
        for c in range(2):
            acc_s[c] = alphas[c] * acc_s[c] + jnp.dot(vt, ps[c], preferred_element_type=F32)

    ms = [jnp.full((1, tq), -jnp.inf, F32)] * 2
    st = scores(0)
    pend = None
    for j in range(nk):
        st_next = scores(j + 1) if j + 1 < nk else None
        if pend is not None:
            accumulate(pend)
        ps, alphas = [], []
        for c in range(2):
            s_c = st[c * tk:(c + 1) * tk]
            m_new = jnp.maximum(ms[c], jnp.max(s_c, axis=0, keepdims=True))
            alphas.append(jnp.exp2(ms[c] - m_new))
            ps.append(jnp.exp2(s_c - m_new).astype(BF16))
            ms[c] = m_new
        pend = (j, ps, alphas)
        st = st_next
    accumulate(pend)

    a0 = acc_s[0]
    a1 = acc_s[1]
    o0 = a0[:DA_V_DIM] / a0[DA_V_DIM:DA_V_DIM + 1]
    o1 = a1[:DA_V_DIM] / a1[DA_V_DIM:DA_V_DIM + 1]
    lv = lam_ref[...]
    s01 = jnp.sum(lv[0:1] * lv[1:2], axis=-1, keepdims=True)
    s23 = jnp.sum(lv[2:3] * lv[3:4], axis=-1, keepdims=True)
    lam = jnp.exp(s01) - jnp.exp(s23) + lam_init
    d = o0 - lam * o1
    ms_ = jnp.mean(d * d, axis=0, keepdims=True)
    o_ref[0] = (d * lax.rsqrt(ms_ + EPS)).T * subg_ref[...] * (1.0 - lam_init)


def _flash(QT, Ks, VT, lam_vec, sub_g, lam_init, q_row0, n_q_rows, k_row0, n_k_rows, tq, n_sub):
    B, _, Tt = QT.shape
    sub = VT.shape[-1]
    tk = sub * n_sub
    nq, nk = n_q_rows // tq, n_k_rows // tk
    q0 = q_row0 // tq
    vrows = VT.shape[3]
    return pl.pallas_call(
        functools.partial(_flash_kernel, tq=tq, sub=sub, n_sub=n_sub, nk=nk, chunk0=k_row0 // sub,
                          lam_init=lam_init),
        grid=(B, DA_HEADS, nq),
        in_specs=[pl.BlockSpec((1, 128, tq), lambda b, h, i: (b, h, q0 + i)),
                  pl.BlockSpec((1, Tt, 256), lambda b, h, i: (b, 0, h)),
                  pl.BlockSpec((1, 1, VT.shape[2], vrows, sub), lambda b, h, i: (b, h, 0, 0, 0)),
                  pl.BlockSpec((4, DA_QK_DIM), lambda b, h, i: (0, 0)),
                  pl.BlockSpec((1, DA_V_DIM), lambda b, h, i: (0, 0))],
        out_specs=pl.BlockSpec((1, tq, 128), lambda b, h, i: (b, i, h)),
        out_shape=jax.ShapeDtypeStruct((B, n_q_rows, BRANCH_WIDTH), F32),
        scratch_shapes=[pltpu.VMEM((2, vrows, tq), F32)],
        compiler_params=_cparams(("parallel", "parallel", "arbitrary")),
        name="diff_attn",
    )(QT, Ks, VT, lam_vec, sub_g.reshape(1, DA_V_DIM))


def _scan_tile(j, n_ctx, nt, reverse):
    n_lat = nt - n_ctx
    if not reverse:
        return jnp.where(j < n_ctx, n_lat + j, j - n_ctx)
    return jnp.where(j < n_ctx, nt - 1 - j, n_lat - 1 - (j - n_ctx))


def _stream_edges(t, n_ctx, nt):
    n_lat = nt - n_ctx
    return jnp.logical_or(t == 0, t == n_lat), jnp.logical_or(t == n_lat - 1, t == nt - 1)


def _dwconv4(x, prev8, next8, w_ref, is_first, is_last):
    tm = x.shape[0]
    row = lax.broadcasted_iota(jnp.int32, x.shape, 0)
    pf = jnp.where(is_first, 0.0, 1.0)
    nf = jnp.where(is_last, 0.0, 1.0)
    prow = prev8[HALO_ROWS - 1:HALO_ROWS, :] * pf
    n0 = next8[0:1, :] * nf
    n1 = next8[1:2, :] * nf
    xm1 = jnp.where(row == 0, prow, pltpu.roll(x, 1, 0))
    xp1 = jnp.where(row == tm - 1, n0, pltpu.roll(x, tm - 1, 0))
    xp2 = jnp.where(row == tm - 2, n0, jnp.where(row == tm - 1, n1, pltpu.roll(x, tm - 2, 0)))
    return w_ref[0:1, :] * xm1 + w_ref[1:2, :] * x + w_ref[2:3, :] * xp1 + w_ref[3:4, :] * xp2


def _halo_specs(tm, width, col_blk, n_ctx, nt, reverse):
    r8 = tm // HALO_ROWS
    last8 = nt * r8 - 1

    def cur(b, j):
        return (b, _scan_tile(j, n_ctx, nt, reverse), col_blk)

    def prev(b, j):
        return (b, jnp.maximum(_scan_tile(j, n_ctx, nt, reverse) * r8 - 1, 0), col_blk)

    def nxt(b, j):
        return (b, jnp.minimum((_scan_tile(j, n_ctx, nt, reverse) + 1) * r8, last8), col_blk)

    return [pl.BlockSpec((1, tm, width), cur),
            pl.BlockSpec((1, HALO_ROWS, width), prev),
            pl.BlockSpec((1, HALO_ROWS, width), nxt)]


def _linear_scan_tile(a, b, h0, reverse):
    tm = a.shape[0]
    G = HALO_ROWS
    sub = lax.broadcasted_iota(jnp.int32, a.shape, 0) % G
    s = 1
    while s < G:
        if not reverse:
            keep = sub >= s
            ap = jnp.where(keep, pltpu.roll(a, s, 0), 1.0)
            bp = jnp.where(keep, pltpu.roll(b, s, 0), 0.0)
        else:
            keep = sub < G - s
            ap = jnp.where(keep, pltpu.roll(a, tm - s, 0), 1.0)
            bp = jnp.where(keep, pltpu.roll(b, tm - s, 0), 0.0)
        b = a * bp + b
        a = a * ap
        s *= 2
    n_groups = tm // G
    outs = [None] * n_groups
    carry = h0
    for g in (range(n_groups - 1, -1, -1) if reverse else range(n_groups)):
        hg = b[g * G:(g + 1) * G] + a[g * G:(g + 1) * G] * carry
        outs[g] = hg
        carry = hg[0:1] if reverse else hg[G - 1:G]
    return jnp.concatenate(outs, axis=0), carry


def _lru_kernel(*refs, n_ctx, nt, reverse, final):
    if final:
        (x_ref, xp_ref, xn_ref, y_ref, hf_ref, cw_ref, cb_ref, wg_ref, gb_ref, lam_ref, o_ref, carry) = refs
    else:
        (x_ref, xp_ref, xn_ref, cw_ref, cb_ref, wg_ref, gb_ref, lam_ref, o_ref, carry) = refs
    j = pl.program_id(1)
    t = _scan_tile(j, n_ctx, nt, reverse)
    is_first, is_last = _stream_edges(t, n_ctx, nt)

    @pl.when(j == 0)
    def _():
        carry[...] = jnp.zeros(carry.shape, F32)

    x = x_ref[0]
    tm, W = x.shape
    xc = _dwconv4(x, xp_ref[0], xn_ref[0], cw_ref, is_first, is_last) + cb_ref[...]
    g = jnp.dot(xc.astype(BF16), wg_ref[...], preferred_element_type=F32) + gb_ref[...]
    r = jax.nn.sigmoid(g[:, :W])
    i = jax.nn.sigmoid(g[:, W:])
    log_a = (-LRU_C) * r * _softplus(-lam_ref[...])
    a = jnp.exp(log_a)
    th = jnp.tanh(log_a)
    bb = jnp.sqrt(-2.0 * th / (1.0 - th)) * (i * xc)
    h, h_last = _linear_scan_tile(a, bb, carry[0:1, :], reverse)
    carry[0:1, :] = h_last
    if final:
        o_ref[0] = jax.nn.gelu(y_ref[0]) * (hf_ref[0] + h)
    else:
        o_ref[0] = h


def _lru_pass(P, cols, conv_w, conv_b, wg, gb, lam, n_ctx, reverse, hf=None):
    B, Tt, _ = P.shape
    tm = ROW_TILE
    W = BRANCH_WIDTH
    nt = Tt // tm
    final = hf is not None

    def cur(b, j):
        return (b, _scan_tile(j, n_ctx, nt, reverse), 0)

    in_specs = _halo_specs(tm, W, cols["lx"] // W, n_ctx, nt, reverse)
    args = [P, P, P]
    if final:
        yb = cols["ly"] // W
        in_specs += [pl.BlockSpec((1, tm, W), lambda b, j: (b, _scan_tile(j, n_ctx, nt, reverse), yb)),
                     pl.BlockSpec((1, tm, W), cur)]
        args += [P, hf]
    in_specs += [pl.BlockSpec((4, W), lambda b, j: (0, 0)),
                 pl.BlockSpec((1, W), lambda b, j: (0, 0)),
                 pl.BlockSpec((W, 2 * W), lambda b, j: (0, 0)),
                 pl.BlockSpec((1, 2 * W), lambda b, j: (0, 0)),
                 pl.BlockSpec((1, W), lambda b, j: (0, 0))]
    args += [conv_w, conv_b.reshape(1, W), wg, gb.reshape(1, 2 * W), lam.reshape(1, W)]
    return pl.pallas_call(
        functools.partial(_lru_kernel, n_ctx=n_ctx, nt=nt, reverse=reverse, final=final),
        grid=(B, nt),
        in_specs=in_specs,
        out_specs=pl.BlockSpec((1, tm, W), cur),
        out_shape=jax.ShapeDtypeStruct((B, Tt, W), F32),
        scratch_shapes=[pltpu.VMEM((HALO_ROWS, W), F32)],
        compiler_params=_cparams(("parallel", "arbitrary")),
        name="rglru_bwd" if reverse else "rglru_fwd",
    )(*args)


def _lru_gate_weights(gate_w_d):
    mats = []
    for g in range(2):
        mats.append(jax.scipy.linalg.block_diag(*[gate_w_d[g, n] for n in range(LRU_BLOCKS)]))
    return jnp.concatenate(mats, axis=1).astype(BF16)


def _split_bf16(x):
    hi = x.astype(BF16)
    lo = (x - hi.astype(F32)).astype(BF16)
    return hi, lo


def _dot3(a_hi, a_lo, b_hi, b_lo):
    return (jnp.dot(a_hi, b_hi, preferred_element_type=F32)
            + (jnp.dot(a_hi, b_lo, preferred_element_type=F32) + jnp.dot(a_lo, b_hi, preferred_element_type=F32)))


def _gdn_kernel(*refs, n_ctx, nt, reverse, final, d):
    if final:
        (x_ref, xp_ref, xn_ref, sm_ref, z_ref, of_ref, cw_ref, rate_ref, dtb_ref, ng_ref, o_ref, S_s, qkv_s) = refs
    else:
        (x_ref, xp_ref, xn_ref, sm_ref, cw_ref, rate_ref, dtb_ref, o_ref, S_s, qkv_s) = refs
    j = pl.program_id(1)
    t = _scan_tile(j, n_ctx, nt, reverse)
    is_first, is_last = _stream_edges(t, n_ctx, nt)

    @pl.when(j == 0)
    def _():
        S_s[...] = jnp.zeros(S_s.shape, F32)

    x = x_ref[0]
    tm = x.shape[0]
    C = GDN_CHUNK
    nC = tm // C
    qkv = _dwconv4(x, xp_ref[0], xn_ref[0], cw_ref, is_first, is_last)
    qkv_s[...] = qkv * jax.nn.sigmoid(qkv)

    sm = sm_ref[0]
    beta_all = jax.nn.sigmoid(sm)
    g_all = -jnp.exp(rate_ref[...]) * _softplus(sm + dtb_ref[...])

    ii = lax.broadcasted_iota(jnp.int32, (tm, tm), 0)
    jj = lax.broadcasted_iota(jnp.int32, (tm, tm), 1)
    same = (ii // C) == (jj // C)
    if not reverse:
        incl = jnp.logical_and(same, ii >= jj)
        strict = jnp.logical_and(same, ii > jj)
    else:
        incl = jnp.logical_and(same, ii <= jj)
        strict = jnp.logical_and(same, ii < jj)
    incl_b = incl.astype(F32).astype(BF16)
    eye = (ii == jj).astype(F32)

    g1 = g_all.astype(BF16)
    g2 = (g_all - g1.astype(F32)).astype(BF16)
    g3 = (g_all - g1.astype(F32) - g2.astype(F32)).astype(BF16)
    gc_all = (jnp.dot(incl_b, g1, preferred_element_type=F32)
              + (jnp.dot(incl_b, g2, preferred_element_type=F32) + jnp.dot(incl_b, g3, preferred_element_type=F32)))
    last = 0 if reverse else C - 1
    gtot_all = jnp.concatenate(
        [jnp.broadcast_to(gc_all[c * C + last:c * C + last + 1, :], (C, 128)) for c in range(nC)], axis=0)

    groups = [_gdn_group_stages(hs, d=d, tm=tm, reverse=reverse, final=final, qkv_s=qkv_s, S_s=S_s, o_ref=o_ref,
                                of_ref=of_ref if final else None, z_ref=z_ref if final else None,
                                ng_ref=ng_ref if final else None, beta_all=beta_all, gc_all=gc_all,
                                gtot_all=gtot_all, incl=incl, strict=strict, eye=eye)
              for hs in GDN_HEAD_GROUPS]
    _run_skewed(groups, GDN_STAGE_SKEW)


def _run_skewed(gens, skew):
    done = [False] * len(gens)
    rnd = 0
    while not all(done):
        for k, g in enumerate(gens):
            if done[k] or rnd < k * skew:
                continue
            try:
                next(g)
            except StopIteration:
                done[k] = True
        rnd += 1


def _gdn_group_stages(heads, *, d, tm, reverse, final, qkv_s, S_s, o_ref, of_ref, z_ref, ng_ref, beta_all, gc_all,
                      gtot_all, incl, strict, eye):
    C = GDN_CHUNK
    nC = tm // C
    HK = GDN_HEADS * GDN_DK
    rhs, q_dec, k_dec, g_end, m_k, attn = {}, {}, {}, {}, {}, {}
    for h in heads:
        q = qkv_s[:, h * GDN_DK:(h + 1) * GDN_DK]
        k = qkv_s[:, HK + h * GDN_DK:HK + (h + 1) * GDN_DK]
        v = qkv_s[:, 2 * HK + h * GDN_DV:2 * HK + (h + 1) * GDN_DV]
        qn = q * lax.rsqrt(jnp.sum(q * q, axis=-1, keepdims=True) + EPS) * (GDN_DK ** -0.5)
        kn = k * lax.rsqrt(jnp.sum(k * k, axis=-1, keepdims=True) + EPS)
        cb = d * GDN_HEADS + h
        cg = N_DIR * GDN_HEADS + d * GDN_HEADS + h
        beta_b = jnp.broadcast_to(beta_all[:, cb:cb + 1], (tm, 128))
        gc_b = jnp.broadcast_to(gc_all[:, cg:cg + 1], (tm, 128))
        gtot_b = jnp.broadcast_to(gtot_all[:, cg:cg + 1], (tm, 128))
        eg = jnp.exp(gc_b)
        kb = kn * beta_b
        rhs[h] = jnp.concatenate([v * beta_b, kb * eg], axis=1).astype(BF16)
        q_dec[h] = qn * eg
        k_dec[h] = kn * jnp.exp(gtot_b - gc_b)
        g_end[h] = jnp.exp(gtot_b)
        gc_cols = jnp.concatenate([gc_b] * (tm // 128), axis=1)
        gc_rows = gc_b.T[0:1, :]
        decay = jnp.exp(jnp.where(incl, gc_cols - gc_rows, NEG_BIG))
        kq = lax.dot_general(jnp.concatenate([kb, qn], axis=0).astype(BF16), kn.astype(BF16),
                             (((1,), (1,)), ((), ())), preferred_element_type=F32)
        m_k[h] = jnp.where(strict, -(kq[:tm] * decay), 0.0)
        attn[h] = (kq[tm:] * decay).astype(BF16)
    yield

    t_inv = {h: eye + m_k[h] for h in heads}
    p_k = {}
    for h in heads:
        m_hi, m_lo = _split_bf16(m_k[h])
        p_k[h] = _dot3(m_hi, m_lo, m_hi, m_lo)
    yield
    for _ in range(4):
        for h in heads:
            p_hi, p_lo = _split_bf16(p_k[h])
            t_hi, t_lo = _split_bf16(t_inv[h])
            r2 = _dot3(jnp.concatenate([t_hi, p_hi], axis=0), jnp.concatenate([t_lo, p_lo], axis=0), p_hi, p_lo)
            t_inv[h] = t_inv[h] + r2[:tm]
            p_k[h] = r2[tm:]
        yield
    u, w = {}, {}
    for h in heads:
        p_hi, p_lo = _split_bf16(p_k[h])
        t_hi, t_lo = _split_bf16(t_inv[h])
        t_fin = t_inv[h] + _dot3(t_hi, t_lo, p_hi, p_lo)
        sol = jnp.dot(t_fin.astype(BF16), rhs[h], preferred_element_type=F32)
        u[h] = sol[:, :GDN_DV]
        w[h] = sol[:, GDN_DV:]
    yield

    q_eff, o_free, kw = {}, {}, {}
    for h in heads:
        wu = jnp.concatenate([w[h], u[h]], axis=1).astype(BF16)
        aw = jnp.dot(attn[h], wu, preferred_element_type=F32)
        q_eff[h] = (q_dec[h] - aw[:, :GDN_DV]).astype(BF16)
        o_free[h] = aw[:, GDN_DV:]
        kd = k_dec[h].astype(BF16)
        kw[h] = [lax.dot_general(kd[c * C:(c + 1) * C], wu[c * C:(c + 1) * C], (((0,), (0,)), ((), ())),
                                 preferred_element_type=F32) for c in range(nC)]
    yield

    S = {h: S_s[h] for h in heads}
    for cc in range(nC):
        c = nC - 1 - cc if reverse else cc
        r0 = c * C
        for h in heads:
            s_b = S[h].astype(BF16)
            o_c = o_free[h][r0:r0 + C] + jnp.dot(q_eff[h][r0:r0 + C], s_b, preferred_element_type=F32)
            S[h] = (S[h] * _col(g_end[h][r0:r0 + 1, :], GDN_DK) + kw[h][c][:, GDN_DV:]
                    - jnp.dot(kw[h][c][:, :GDN_DV].astype(BF16), s_b, preferred_element_type=F32))
            if final:
                o_t = of_ref[0, r0:r0 + C, h * GDN_DV:(h + 1) * GDN_DV] + o_c
                ms = jnp.mean(o_t * o_t, axis=-1, keepdims=True)
                z = z_ref[0, r0:r0 + C, h * GDN_DV:(h + 1) * GDN_DV]
                o_ref[0, r0:r0 + C, h * GDN_DV:(h + 1) * GDN_DV] = (
                    o_t * lax.rsqrt(ms + EPS) * ng_ref[...] * (z * jax.nn.sigmoid(z)))
            else:
                o_ref[0, r0:r0 + C, h * GDN_DV:(h + 1) * GDN_DV] = o_c
        yield
    for h in heads:
        S_s[h] = S[h]


def _col(row_vec, n):
    return jnp.broadcast_to(row_vec, (n, row_vec.shape[1]))


def _gdn_pass(P, Psm, cols, conv_w, rate_row, dtb_row, n_ctx, reverse, d, of=None, norm_g=None):
    B, Tt, _ = P.shape
    tm = ROW_TILE
    W = BRANCH_WIDTH
    QW = GDN_HEADS * (2 * GDN_DK + GDN_DV)
    nt = Tt // tm
    final = of is not None

    def cur(b, j):
        return (b, _scan_tile(j, n_ctx, nt, reverse), 0)

    in_specs = _halo_specs(tm, QW, cols["gqkv"] // QW, n_ctx, nt, reverse)
    in_specs += [pl.BlockSpec((1, tm, 128), cur)]
    args = [P, P, P, Psm]
    if final:
        zb = cols["gz"] // W
        in_specs += [pl.BlockSpec((1, tm, W), lambda b, j: (b, _scan_tile(j, n_ctx, nt, reverse), zb)),
                     pl.BlockSpec((1, tm, W), cur)]
        args += [P, of]
    in_specs += [pl.BlockSpec((4, QW), lambda b, j: (0, 0)),
                 pl.BlockSpec((1, 128), lambda b, j: (0, 0)),
                 pl.BlockSpec((1, 128), lambda b, j: (0, 0))]
    args += [conv_w, rate_row, dtb_row]
    if final:
        in_specs += [pl.BlockSpec((1, GDN_DV), lambda b, j: (0, 0))]
        args += [norm_g.reshape(1, GDN_DV)]
    return pl.pallas_call(
        functools.partial(_gdn_kernel, n_ctx=n_ctx, nt=nt, reverse=reverse, final=final, d=d),
        grid=(B, nt),
        in_specs=in_specs,
        out_specs=pl.BlockSpec((1, tm, W), cur),
        out_shape=jax.ShapeDtypeStruct((B, Tt, W), F32),
        scratch_shapes=[pltpu.VMEM((GDN_HEADS, GDN_DK, GDN_DV), F32), pltpu.VMEM((tm, QW), F32)],
        compiler_params=_cparams(("parallel", "arbitrary")),
        name="gdn_bwd" if reverse else "gdn_fwd",
    )(*args)


def _post_kernel(x_ref, da_ref, lru_ref, gdn_ref, g0_ref, g1_ref, g2_ref, mod_ref, n2_ref, wb_ref, wo_ref,
                 w1_ref, w2_ref, o_ref, *, ff_chunk):
    outs = (da_ref, lru_ref, gdn_ref)
    gates = (g0_ref, g1_ref, g2_ref)
    merged = None
    for i in range(N_BRANCH):
        y = jnp.dot(outs[i][0].astype(BF16), wb_ref[i], preferred_element_type=F32)
        term = jax.nn.sigmoid(gates[i][0]) * y
        merged = term if merged is None else merged + term
    proj = jnp.dot(merged.astype(BF16), wo_ref[...], preferred_element_type=F32)
    x = x_ref[0] + mod_ref[0, 2:3, :] * proj

    ms = jnp.mean(x * x, axis=-1, keepdims=True)
    y = x * lax.rsqrt(ms + EPS) * n2_ref[...]
    h = (y * (1.0 + mod_ref[0, 4:5, :]) + mod_ref[0, 3:4, :]).astype(BF16)
    n_ff = w1_ref.shape[1]
    acc = None
    for c0 in range(0, n_ff, ff_chunk):
        a = jnp.dot(h, w1_ref[:, c0:c0 + ff_chunk], preferred_element_type=F32)
        a = jnp.square(jnp.maximum(a, 0.0)).astype(BF16)
        part = jnp.dot(a, w2_ref[c0:c0 + ff_chunk, :], preferred_element_type=F32)
        acc = part if acc is None else acc + part
    o_ref[0] = x + mod_ref[0, 5:6, :] * acc


def _post(xa, da, lru, gdn, P, mod3, n2_g, wb, wo, w1, w2, n_lat_tiles, ctx_row, n_rows):
    B, Tt, D = xa.shape
    tm = ROW_TILE
    W = BRANCH_WIDTH
    F = w1.shape[1]
    row = lambda b, t: (b, t, 0)
    once = pl.Buffered(1)
    return pl.pallas_call(
        functools.partial(_post_kernel, ff_chunk=min(F, 1024)),
        grid=(B, n_rows // tm),
        in_specs=[pl.BlockSpec((1, tm, D), row),
                  pl.BlockSpec((1, tm, W), row), pl.BlockSpec((1, tm, W), row), pl.BlockSpec((1, tm, W), row),
                  pl.BlockSpec((1, tm, D), lambda b, t: (b, t, 0)),
                  pl.BlockSpec((1, tm, D), lambda b, t: (b, t, 1)),
                  pl.BlockSpec((1, tm, D), lambda b, t: (b, t, 2)),
                  pl.BlockSpec((1, N_MOD, D), lambda b, t: (jnp.where(t >= n_lat_tiles, ctx_row, b), 0, 0)),
                  pl.BlockSpec((1, D), lambda b, t: (0, 0)),
                  pl.BlockSpec((N_BRANCH, W, D), lambda b, t: (0, 0, 0), pipeline_mode=once),
                  pl.BlockSpec((D, D), lambda b, t: (0, 0), pipeline_mode=once),
                  pl.BlockSpec((D, F), lambda b, t: (0, 0), pipeline_mode=once),
                  pl.BlockSpec((F, D), lambda b, t: (0, 0), pipeline_mode=once)],
        out_specs=pl.BlockSpec((1, tm, D), row),
        out_shape=jax.ShapeDtypeStruct((B, n_rows, D), F32),
        compiler_params=_cparams(("parallel", "parallel")),
        name="merge_mlp",
    )(xa, da, lru, gdn, P, P, P, mod3, n2_g.reshape(1, D), wb, wo, w1, w2)


def _proj_layout(D):
    W = BRANCH_WIDTH
    QW = GDN_HEADS * (2 * GDN_DK + GDN_DV)
    src = {}
    off = 0
    for name, width in (("q", W), ("k", W), ("v", W), ("lx", W), ("ly", W), ("gqkv", QW), ("gz", W),
                        ("gb", N_DIR * GDN_HEADS), ("ga", N_DIR * GDN_HEADS), ("gate", N_BRANCH * D)):
        src[name] = (off, width)
        off += width
    order = ("gate", "gqkv", "lx", "ly", "gz")
    cols = {}
    o = 0
    for name in order:
        cols[name] = o
        o += src[name][1]
    return cols, [src[n] for n in order], [src[n] for n in ("q", "k", "v")], [src[n] for n in ("gb", "ga")]


def _take_cols(w, ranges):
    return jnp.concatenate([w[:, s:s + n] for s, n in ranges], axis=1)


def _rope_tables(n_ctx_rows, n_lat_rows):
    n_freq = DA_QK_DIM // 4
    inv = ROPE_BASE ** (-jnp.arange(n_freq, dtype=F32) / n_freq)
    tpos = jnp.arange(n_lat_rows, dtype=jnp.int32)
    ang_r = (tpos // GRID_W).astype(F32)[:, None] * inv
    ang_c = (tpos % GRID_W).astype(F32)[:, None] * inv
    ang = jnp.concatenate([ang_r, ang_r, ang_c, ang_c], axis=-1)
    ang = jnp.concatenate([ang, ang], axis=-1)
    sign = jnp.where((jnp.arange(128) % 32) < 16, -1.0, 1.0).astype(F32)
    cos = jnp.concatenate([jnp.cos(ang), jnp.ones((n_ctx_rows, 128), F32)], axis=0)
    sin = jnp.concatenate([jnp.sin(ang) * sign, jnp.zeros((n_ctx_rows, 128), F32)], axis=0)
    return cos, sin


def kernel(x, c, ctx, c_ctx, ada_w, ada_b, norm1_g, norm2_g, w_in, da_q_norm_g, da_k_norm_g, da_lambda, da_sub_norm_g, lru_conv_w, lru_conv_b, lru_gate_w, lru_gate_b, lru_lambda, gdn_conv_w, gdn_A_log, gdn_dt_bias, gdn_norm_g, w_branch, w_out, mlp_w1, mlp_w2):
    B, T, D = x.shape
    Tc = ctx.shape[1]
    depth = ada_w.shape[0]
    tm = ROW_TILE
    assert Tc % tm == 0 and T % tm == 0 and B + 1 <= 8
    n_ctx = Tc // tm
    n_lat = T // tm
    Tt = Tc + T
    ctx_row = B

    cols, main_cols, qkv_cols, small_cols = _proj_layout(D)
    cos_t, sin_t = _rope_tables(Tc, T)
    xa = jnp.concatenate([x, ctx], axis=1)
    c_all = jnp.concatenate([c, c_ctx[None, :], jnp.zeros((8 - B - 1, D), F32)], axis=0)

    tq = _pick(T, (1024, 512, 256))
    n_sub = _pick(Tt // tm, (3, 2, 1))

    out = None
    for layer in range(depth):
        last = layer == depth - 1
        lam_init = 0.8 - 0.6 * math.exp(-0.3 * layer)
        w_main = _take_cols(w_in[layer], main_cols).astype(BF16)
        w_qkv = _take_cols(w_in[layer], qkv_cols).astype(BF16)
        w_small = _take_cols(w_in[layer], small_cols)
        w_small = jnp.pad(w_small, ((0, 0), (0, 128 - w_small.shape[1]))).astype(BF16)

        mod3 = _adaln(c_all, ada_w[layer], ada_b[layer]).reshape(8, N_MOD, D)
        P, Psm, QT, Ks, VT = _front(xa, mod3, norm1_g[layer], w_main, w_qkv, w_small, cos_t, sin_t,
                                    da_q_norm_g[layer], da_k_norm_g[layer], n_lat, ctx_row)

        da_l = _flash(QT, Ks, VT, da_lambda[layer], da_sub_norm_g[layer], lam_init, 0, T, 0, Tt, tq, n_sub)
        if last:
            da = da_l
        else:
            da_c = _flash(QT, Ks, VT, da_lambda[layer], da_sub_norm_g[layer], lam_init, T, Tc, T, Tc, tm, 1)
            da = jnp.concatenate([da_l, da_c], axis=1)

        hf = _lru_pass(P, cols, lru_conv_w[layer], lru_conv_b[layer], _lru_gate_weights(lru_gate_w[layer, 0]),
                       lru_gate_b[layer, 0].reshape(-1), lru_lambda[layer, 0], n_ctx, False)
        lru = _lru_pass(P, cols, lru_conv_w[layer], lru_conv_b[layer], _lru_gate_weights(lru_gate_w[layer, 1]),
                        lru_gate_b[layer, 1].reshape(-1), lru_lambda[layer, 1], n_ctx, True, hf=hf)

        rate_row = jnp.zeros((1, 128), F32).at[0, N_DIR * GDN_HEADS:2 * N_DIR * GDN_HEADS].set(
            gdn_A_log[layer].astype(F32).reshape(-1))
        dtb_row = jnp.zeros((1, 128), F32).at[0, N_DIR * GDN_HEADS:2 * N_DIR * GDN_HEADS].set(
            gdn_dt_bias[layer].astype(F32).reshape(-1))
        of = _gdn_pass(P, Psm, cols, gdn_conv_w[layer], rate_row, dtb_row, n_ctx, False, 0)
        gdn = _gdn_pass(P, Psm, cols, gdn_conv_w[layer], rate_row, dtb_row, n_ctx, True, 1, of=of,
                        norm_g=gdn_norm_g[layer])

        res = _post(xa, da, lru, gdn, P, mod3, norm2_g[layer], w_branch[layer].astype(BF16),
                    w_out[layer].astype(BF16), mlp_w1[layer].astype(BF16), mlp_w2[layer].astype(BF16),
                    n_lat, ctx_row, T if last else Tt)
        if last:
            out = res
        else:
            xa = res
    return out
```

```python
import functools
import math

import jax
import jax.numpy as jnp
from jax import lax
from jax.experimental import pallas as pl
from jax.experimental.pallas import tpu as pltpu

F32 = jnp.float32
BF16 = jnp.bfloat16
HIGHEST = lax.Precision.HIGHEST

N_MOD = 6
BRANCH_WIDTH = 512
N_BRANCH = 3
DA_HEADS = 4
DA_QK_DIM = 64
DA_V_DIM = 128
ROPE_BASE = 10000.0
LRU_BLOCKS = 8
LRU_C = 8.0
GDN_HEADS = 4
GDN_DK = 128
GDN_DV = 128
GDN_CHUNK = 64
N_DIR = 2
GRID_W = 64
EPS = 1e-6
NEG_BIG = -1e30

ROW_TILE = 256
HALO_ROWS = 8
VT_PAD = 16
GDN_HEAD_GROUPS = ((0, 1), (2, 3))
GDN_STAGE_SKEW = 2
VMEM_LIMIT = 48 * 1024 * 1024


def _cparams(sem):
    return pltpu.CompilerParams(dimension_semantics=sem, vmem_limit_bytes=VMEM_LIMIT)


def _softplus(z):
    return jnp.maximum(z, 0.0) + jnp.log1p(jnp.exp(-jnp.abs(z)))


def _adaln_kernel(c_ref, w_ref, b_ref, o_ref):
    c = c_ref[...]
    a = c * jax.nn.sigmoid(c)
    o_ref[...] = jnp.dot(a, w_ref[...], preferred_element_type=F32, precision=HIGHEST) + b_ref[...]


def _adaln(c_all, w, b):
    R, D = c_all.shape
    N = w.shape[1]
    tn = 1536 if N % 1536 == 0 else N
    return pl.pallas_call(
        _adaln_kernel,
        grid=(N // tn,),
        in_specs=[pl.BlockSpec((R, D), lambda j: (0, 0)),
                  pl.BlockSpec((D, tn), lambda j: (0, j)),
                  pl.BlockSpec((1, tn), lambda j: (0, j))],
        out_specs=pl.BlockSpec((R, tn), lambda j: (0, j)),
        out_shape=jax.ShapeDtypeStruct((R, N), F32),
        compiler_params=_cparams(("arbitrary",)),
        name="adaln",
    )(c_all, w, b.reshape(1, N))


def _pick(n, cands):
    for c in cands:
        if n % c == 0:
            return c
    return n


def _seg_mean(x2, s_ref):
    hi = x2.astype(BF16)
    lo = (x2 - hi.astype(F32)).astype(BF16)
    s = s_ref[...]
    return jnp.dot(hi, s, preferred_element_type=F32) + jnp.dot(lo, s, preferred_element_type=F32)


def _rope(y, cos, sin_signed, first_half):
    rot = jnp.where(first_half, pltpu.roll(y, 128 - 16, 1), pltpu.roll(y, 16, 1))
    return y * cos + rot * sin_signed


def _front_kernel(x_ref, mod_ref, g_ref, wm_ref, wq_ref, ws_ref, cos_ref, sin_ref, gq_ref, gk_ref, s_ref,
                  p_ref, ps_ref, qo_ref, ko_ref, vo_ref, *, q_scale, col_chunk):
    x = x_ref[0]
    ms = jnp.mean(x * x, axis=-1, keepdims=True)
    y = x * lax.rsqrt(ms + EPS) * g_ref[...]
    hb = (y * (1.0 + mod_ref[0, 1:2, :]) + mod_ref[0, 0:1, :]).astype(BF16)
    qkv = jnp.dot(hb, wq_ref[...], preferred_element_type=F32)
    main_chunks = list(range(0, wm_ref.shape[1], col_chunk))

    def project(c0):
        p_ref[0, :, c0:c0 + col_chunk] = jnp.dot(hb, wm_ref[:, c0:c0 + col_chunk], preferred_element_type=F32)

    project(main_chunks.pop(0))
    W = BRANCH_WIDTH

    cos = cos_ref[...]
    sin = sin_ref[...]
    tm = cos.shape[0]
    lane = lax.broadcasted_iota(jnp.int32, (tm, 128), 1)
    first_half = (lane % 32) < 16
    comp0 = lane < DA_QK_DIM

    q = qkv[:, :W]
    qn = q * lax.rsqrt(_seg_mean(q * q, s_ref) + EPS) * gq_ref[...]
    k = qkv[:, W:2 * W]
    kn = k * lax.rsqrt(_seg_mean(k * k, s_ref) + EPS) * gk_ref[...]
    v = qkv[:, 2 * W:]
    ones_row = jnp.where(lax.broadcasted_iota(jnp.int32, (VT_PAD, tm), 0) == 0, 1.0, 0.0).astype(BF16)
    for h in range(DA_HEADS):
        qh = _rope(qn[:, h * 128:(h + 1) * 128], cos, sin, first_half) * q_scale
        qo_ref[0, h * 128:(h + 1) * 128, :] = qh.T.astype(BF16)
        kh = _rope(kn[:, h * 128:(h + 1) * 128], cos, sin, first_half)
        ko_ref[0, :, h * 256:h * 256 + 128] = jnp.where(comp0, kh, 0.0).astype(BF16)
        ko_ref[0, :, h * 256 + 128:(h + 1) * 256] = jnp.where(comp0, 0.0, kh).astype(BF16)
        vo_ref[0, h, 0, 0:DA_V_DIM, :] = v[:, h * DA_V_DIM:(h + 1) * DA_V_DIM].T.astype(BF16)
        vo_ref[0, h, 0, DA_V_DIM:DA_V_DIM + VT_PAD, :] = ones_row
        if main_chunks:
            project(main_chunks.pop(0))
    for c0 in main_chunks:
        project(c0)
    ps_ref[0] = jnp.dot(hb, ws_ref[...], preferred_element_type=F32)


def _front(xa, mod3, g, w_main, w_qkv, w_small, cos_t, sin_t, gq, gk, n_lat_tiles, ctx_row):
    B, Tt, D = xa.shape
    tm = ROW_TILE
    W = BRANCH_WIDTH
    NM = w_main.shape[1]
    seg = (jnp.arange(W)[:, None] // DA_QK_DIM == jnp.arange(W)[None, :] // DA_QK_DIM)
    smat = (seg.astype(F32) / DA_QK_DIM).astype(BF16)
    gq_t = jnp.tile(gq, W // DA_QK_DIM).reshape(1, W)
    gk_t = jnp.tile(gk, W // DA_QK_DIM).reshape(1, W)
    q_scale = (DA_QK_DIM ** -0.5) * math.log2(math.e)
    once = pl.Buffered(1)
    const = lambda b, t: (0, 0)
    return pl.pallas_call(
        functools.partial(_front_kernel, q_scale=q_scale, col_chunk=_pick(NM, (1536, 1024, 512))),
        grid=(B, Tt // tm),
        in_specs=[pl.BlockSpec((1, tm, D), lambda b, t: (b, t, 0)),
                  pl.BlockSpec((1, N_MOD, D), lambda b, t: (jnp.where(t >= n_lat_tiles, ctx_row, b), 0, 0)),
                  pl.BlockSpec((1, D), const),
                  pl.BlockSpec((D, NM), const, pipeline_mode=once),
                  pl.BlockSpec((D, 3 * W), const, pipeline_mode=once),
                  pl.BlockSpec((D, 128), const, pipeline_mode=once),
                  pl.BlockSpec((tm, 128), lambda b, t: (t, 0)),
                  pl.BlockSpec((tm, 128), lambda b, t: (t, 0)),
                  pl.BlockSpec((1, W), const),
                  pl.BlockSpec((1, W), const),
                  pl.BlockSpec((W, W), const)],
        out_specs=[pl.BlockSpec((1, tm, NM), lambda b, t: (b, t, 0)),
                   pl.BlockSpec((1, tm, 128), lambda b, t: (b, t, 0)),
                   pl.BlockSpec((1, W, tm), lambda b, t: (b, 0, t)),
                   pl.BlockSpec((1, tm, 2 * W), lambda b, t: (b, t, 0)),
                   pl.BlockSpec((1, DA_HEADS, 1, DA_V_DIM + VT_PAD, tm), lambda b, t: (b, 0, t, 0, 0))],
        out_shape=[jax.ShapeDtypeStruct((B, Tt, NM), F32),
                   jax.ShapeDtypeStruct((B, Tt, 128), F32),
                   jax.ShapeDtypeStruct((B, W, Tt), BF16),
                   jax.ShapeDtypeStruct((B, Tt, 2 * W), BF16),
                   jax.ShapeDtypeStruct((B, DA_HEADS, Tt // tm, DA_V_DIM + VT_PAD, tm), BF16)],
        compiler_params=_cparams(("parallel", "parallel")),
        name="front",
    )(xa, mod3, g.reshape(1, D), w_main, w_qkv, w_small, cos_t, sin_t, gq_t, gk_t, smat)


def _flash_kernel(qt_ref, k_ref, vt_ref, lam_ref, subg_ref, o_ref, acc_s,
                  *, tq, sub, n_sub, nk, chunk0, lam_init):
    qt = qt_ref[0]
    acc_s[...] = jnp.zeros(acc_s.shape, F32)
    tk = sub * n_sub

    def scores(j):
        r0 = (chunk0 + j * n_sub) * sub
        kk = k_ref[0, r0:r0 + tk, :]
        k2 = jnp.concatenate([kk[:, :128], kk[:, 128:]], axis=0)
        return jnp.dot(k2, qt, preferred_element_type=F32)

    def accumulate(pend):
        j, ps, alphas = pend
        c0 = chunk0 + j * n_sub
        vt = jnp.concatenate([vt_ref[0, 0, c0 + i] for i in range(n_sub)], axis=1)
        for c in range(2):
            acc_s[c] = alphas[c] * acc_s[c] + jnp.dot(vt, ps[c], preferred_element_type=F32)

    ms = [jnp.full((1, tq), -jnp.inf, F32)] * 2
    st = scores(0)
    pend = None
    for j in range(nk):
        st_next = scores(j + 1) if j + 1 < nk else None
        if pend is not None:
            accumulate(pend)
        ps, alphas = [], []
        for c in range(2):
            s_c = st[c * tk:(c + 1) * tk]
            m_new = jnp.maximum(ms[c], jnp.max(s_c, axis=0, keepdims=True))
            alphas.append(jnp.exp2(ms[c] - m_new))
            ps.append(jnp.exp2(s_c - m_new).astype(BF16))
            ms[c] = m_new
        pend = (j, ps, alphas)
        st = st_next
    accumulate(pend)

    a0 = acc_s[0]
    a1 = acc_s[1]
    o0 = a0[:DA_V_DIM] / a0[DA_V_DIM:DA_V_DIM + 1]
    o1 = a1[:DA_V_DIM] / a1[DA_V_DIM:DA_V_DIM + 1]
    lv = lam_ref[...]
    s01 = jnp.sum(lv[0:1] * lv[1:2], axis=-1, keepdims=True)
    s23 = jnp.sum(lv[2:3] * lv[3:4], axis=-1, keepdims=True)
    lam = jnp.exp(s01) - jnp.exp(s23) + lam_init
    d = o0 - lam * o1
    ms_ = jnp.mean(d * d, axis=0, keepdims=True)
    o_ref[0] = (d * lax.rsqrt(ms_ + EPS)).T * subg_ref[...] * (1.0 - lam_init)


def _flash(QT, Ks, VT, lam_vec, sub_g, lam_init, q_row0, n_q_rows, k_row0, n_k_rows, tq, n_sub):
    B, _, Tt = QT.shape
    sub = VT.shape[-1]
    tk = sub * n_sub
    nq, nk = n_q_rows // tq, n_k_rows // tk
    q0 = q_row0 // tq
    vrows = VT.shape[3]
    return pl.pallas_call(
        functools.partial(_flash_kernel, tq=tq, sub=sub, n_sub=n_sub, nk=nk, chunk0=k_row0 // sub,
                          lam_init=lam_init),
        grid=(B, DA_HEADS, nq),
        in_specs=[pl.BlockSpec((1, 128, tq), lambda b, h, i: (b, h, q0 + i)),
                  pl.BlockSpec((1, Tt, 256), lambda b, h, i: (b, 0, h)),
                  pl.BlockSpec((1, 1, VT.shape[2], vrows, sub), lambda b, h, i: (b, h, 0, 0, 0)),
                  pl.BlockSpec((4, DA_QK_DIM), lambda b, h, i: (0, 0)),
                  pl.BlockSpec((1, DA_V_DIM), lambda b, h, i: (0, 0))],
        out_specs=pl.BlockSpec((1, tq, 128), lambda b, h, i: (b, i, h)),
        out_shape=jax.ShapeDtypeStruct((B, n_q_rows, BRANCH_WIDTH), F32),
        scratch_shapes=[pltpu.VMEM((2, vrows, tq), F32)],
        compiler_params=_cparams(("parallel", "parallel", "arbitrary")),
        name="diff_attn",
    )(QT, Ks, VT, lam_vec, sub_g.reshape(1, DA_V_DIM))


def _scan_tile(j, n_ctx, nt, reverse):
    n_lat = nt - n_ctx
    if not reverse:
        return jnp.where(j < n_ctx, n_lat + j, j - n_ctx)
    return jnp.where(j < n_ctx, nt - 1 - j, n_lat - 1 - (j - n_ctx))


def _stream_edges(t, n_ctx, nt):
    n_lat = nt - n_ctx
    return jnp.logical_or(t == 0, t == n_lat), jnp.logical_or(t == n_lat - 1, t == nt - 1)


def _dwconv4(x, prev8, next8, w_ref, is_first, is_last):
    tm = x.shape[0]
    row = lax.broadcasted_iota(jnp.int32, x.shape, 0)
    pf = jnp.where(is_first, 0.0, 1.0)
    nf = jnp.where(is_last, 0.0, 1.0)
    prow = prev8[HALO_ROWS - 1:HALO_ROWS, :] * pf
    n0 = next8[0:1, :] * nf
    n1 = next8[1:2, :] * nf
    xm1 = jnp.where(row == 0, prow, pltpu.roll(x, 1, 0))
    xp1 = jnp.where(row == tm - 1, n0, pltpu.roll(x, tm - 1, 0))
    xp2 = jnp.where(row == tm - 2, n0, jnp.where(row == tm - 1, n1, pltpu.roll(x, tm - 2, 0)))
    return w_ref[0:1, :] * xm1 + w_ref[1:2, :] * x + w_ref[2:3, :] * xp1 + w_ref[3:4, :] * xp2


def _halo_specs(tm, width, col_blk, n_ctx, nt, reverse):
    r8 = tm // HALO_ROWS
    last8 = nt * r8 - 1

    def cur(b, j):
        return (b, _scan_tile(j, n_ctx, nt, reverse), col_blk)

    def prev(b, j):
        return (b, jnp.maximum(_scan_tile(j, n_ctx, nt, reverse) * r8 - 1, 0), col_blk)

    def nxt(b, j):
        return (b, jnp.minimum((_scan_tile(j, n_ctx, nt, reverse) + 1) * r8, last8), col_blk)

    return [pl.BlockSpec((1, tm, width), cur),
            pl.BlockSpec((1, HALO_ROWS, width), prev),
            pl.BlockSpec((1, HALO_ROWS, width), nxt)]


def _linear_scan_tile(a, b, h0, reverse):
    tm, W = a.shape
    G = HALO_ROWS
    n_groups = tm // G
    a = a.reshape(n_groups, G, W)
    b = b.reshape(n_groups, G, W)
    sub = lax.broadcasted_iota(jnp.int32, a.shape, 1)
    s = 1
    while s < G:
        if not reverse:
            keep = sub >= s
            ap = jnp.where(keep, pltpu.roll(a, s, 1), 1.0)
            bp = jnp.where(keep, pltpu.roll(b, s, 1), 0.0)
        else:
            keep = sub < G - s
            ap = jnp.where(keep, pltpu.roll(a, G - s, 1), 1.0)
            bp = jnp.where(keep, pltpu.roll(b, G - s, 1), 0.0)
        b = a * bp + b
        a = a * ap
        s *= 2
    outs = [None] * n_groups
    carry = h0
    for g in (range(n_groups - 1, -1, -1) if reverse else range(n_groups)):
        hg = b[g] + a[g] * carry
        outs[g] = hg
        carry = hg[0:1] if reverse else hg[G - 1:G]
    return jnp.concatenate(outs, axis=0), carry


def _lru_kernel(*refs, n_ctx, nt, reverse, final):
    if final:
        (x_ref, xp_ref, xn_ref, y_ref, hf_ref, cw_ref, cb_ref, wg_ref, gb_ref, lam_ref, o_ref, carry) = refs
    else:
        (x_ref, xp_ref, xn_ref, cw_ref, cb_ref, wg_ref, gb_ref, lam_ref, o_ref, carry) = refs
    j = pl.program_id(1)
    t = _scan_tile(j, n_ctx, nt, reverse)
    is_first, is_last = _stream_edges(t, n_ctx, nt)

    @pl.when(j == 0)
    def _():
        carry[...] = jnp.zeros(carry.shape, F32)

    x = x_ref[0]
    tm, W = x.shape
    xc = _dwconv4(x, xp_ref[0], xn_ref[0], cw_ref, is_first, is_last) + cb_ref[...]
    g = jnp.dot(xc.astype(BF16), wg_ref[...], preferred_element_type=F32) + gb_ref[...]
    r = jax.nn.sigmoid(g[:, :W])
    i = jax.nn.sigmoid(g[:, W:])
    log_a = (-LRU_C) * r * _softplus(-lam_ref[...])
    a = jnp.exp(log_a)
    th = jnp.tanh(log_a)
    bb = jnp.sqrt(-2.0 * th / (1.0 - th)) * (i * xc)
    h, h_last = _linear_scan_tile(a, bb, carry[0:1, :], reverse)
    carry[0:1, :] = h_last
    if final:
        o_ref[0] = jax.nn.gelu(y_ref[0]) * (hf_ref[0] + h)
    else:
        o_ref[0] = h


def _lru_pass(P, cols, conv_w, conv_b, wg, gb, lam, n_ctx, reverse, hf=None):
    B, Tt, _ = P.shape
    tm = ROW_TILE
    W = BRANCH_WIDTH
    nt = Tt // tm
    final = hf is not None

    def cur(b, j):
        return (b, _scan_tile(j, n_ctx, nt, reverse), 0)

    in_specs = _halo_specs(tm, W, cols["lx"] // W, n_ctx, nt, reverse)
    args = [P, P, P]
    if final:
        yb = cols["ly"] // W
        in_specs += [pl.BlockSpec((1, tm, W), lambda b, j: (b, _scan_tile(j, n_ctx, nt, reverse), yb)),
                     pl.BlockSpec((1, tm, W), cur)]
        args += [P, hf]
    in_specs += [pl.BlockSpec((4, W), lambda b, j: (0, 0)),
                 pl.BlockSpec((1, W), lambda b, j: (0, 0)),
                 pl.BlockSpec((W, 2 * W), lambda b, j: (0, 0)),
                 pl.BlockSpec((1, 2 * W), lambda b, j: (0, 0)),
                 pl.BlockSpec((1, W), lambda b, j: (0, 0))]
    args += [conv_w, conv_b.reshape(1, W), wg, gb.reshape(1, 2 * W), lam.reshape(1, W)]
    return pl.pallas_call(
        functools.partial(_lru_kernel, n_ctx=n_ctx, nt=nt, reverse=reverse, final=final),
        grid=(B, nt),
        in_specs=in_specs,
        out_specs=pl.BlockSpec((1, tm, W), cur),
        out_shape=jax.ShapeDtypeStruct((B, Tt, W), F32),
        scratch_shapes=[pltpu.VMEM((HALO_ROWS, W), F32)],
        compiler_params=_cparams(("parallel", "arbitrary")),
        name="rglru_bwd" if reverse else "rglru_fwd",
    )(*args)


def _lru_gate_weights(gate_w_d):
    mats = []
    for g in range(2):
        mats.append(jax.scipy.linalg.block_diag(*[gate_w_d[g, n] for n in range(LRU_BLOCKS)]))
    return jnp.concatenate(mats, axis=1).astype(BF16)


def _split_bf16(x):
    hi = x.astype(BF16)
    lo = (x - hi.astype(F32)).astype(BF16)
    return hi, lo


def _dot3(a_hi, a_lo, b_hi, b_lo):
    return (jnp.dot(a_hi, b_hi, preferred_element_type=F32)
            + (jnp.dot(a_hi, b_lo, preferred_element_type=F32) + jnp.dot(a_lo, b_hi, preferred_element_type=F32)))


def _gdn_kernel(*refs, n_ctx, nt, reverse, final, d):
    if final:
        (x_ref, xp_ref, xn_ref, sm_ref, z_ref, of_ref, cw_ref, rate_ref, dtb_ref, ng_ref, o_ref, S_s, qkv_s) = refs
    else:
        (x_ref, xp_ref, xn_ref, sm_ref, cw_ref, rate_ref, dtb_ref, o_ref, S_s, qkv_s) = refs
    j = pl.program_id(1)
    t = _scan_tile(j, n_ctx, nt, reverse)
    is_first, is_last = _stream_edges(t, n_ctx, nt)

    @pl.when(j == 0)
    def _():
        S_s[...] = jnp.zeros(S_s.shape, F32)

    x = x_ref[0]
    tm = x.shape[0]
    C = GDN_CHUNK
    nC = tm // C
    qkv = _dwconv4(x, xp_ref[0], xn_ref[0], cw_ref, is_first, is_last)
    qkv_s[...] = qkv * jax.nn.sigmoid(qkv)

    sm = sm_ref[0]
    beta_all = jax.nn.sigmoid(sm)
    g_all = -jnp.exp(rate_ref[...]) * _softplus(sm + dtb_ref[...])

    ii = lax.broadcasted_iota(jnp.int32, (tm, tm), 0)
    jj = lax.broadcasted_iota(jnp.int32, (tm, tm), 1)
    same = (ii // C) == (jj // C)
    if not reverse:
        incl = jnp.logical_and(same, ii >= jj)
        strict = jnp.logical_and(same, ii > jj)
    else:
        incl = jnp.logical_and(same, ii <= jj)
        strict = jnp.logical_and(same, ii < jj)
    incl_b = incl.astype(F32).astype(BF16)
    eye = (ii == jj).astype(F32)

    g1 = g_all.astype(BF16)
    g2 = (g_all - g1.astype(F32)).astype(BF16)
    g3 = (g_all - g1.astype(F32) - g2.astype(F32)).astype(BF16)
    gc_all = (jnp.dot(incl_b, g1, preferred_element_type=F32)
              + (jnp.dot(incl_b, g2, preferred_element_type=F32) + jnp.dot(incl_b, g3, preferred_element_type=F32)))
    last = 0 if reverse else C - 1
    gtot_all = jnp.concatenate(
        [jnp.broadcast_to(gc_all[c * C + last:c * C + last + 1, :], (C, 128)) for c in range(nC)], axis=0)

    groups = [_gdn_group_stages(hs, d=d, tm=tm, reverse=reverse, final=final, qkv_s=qkv_s, S_s=S_s, o_ref=o_ref,
                                of_ref=of_ref if final else None, z_ref=z_ref if final else None,
                                ng_ref=ng_ref if final else None, beta_all=beta_all, gc_all=gc_all,
                                gtot_all=gtot_all, incl=incl, strict=strict, eye=eye)
              for hs in GDN_HEAD_GROUPS]
    _run_skewed(groups, GDN_STAGE_SKEW)


def _run_skewed(gens, skew):
    done = [False] * len(gens)
    rnd = 0
    while not all(done):
        for k, g in enumerate(gens):
            if done[k] or rnd < k * skew:
                continue
            try:
                next(g)
            except StopIteration:
                done[k] = True
        rnd += 1


def _gdn_group_stages(heads, *, d, tm, reverse, final, qkv_s, S_s, o_ref, of_ref, z_ref, ng_ref, beta_all, gc_all,
                      gtot_all, incl, strict, eye):
    C = GDN_CHUNK
    nC = tm // C
    HK = GDN_HEADS * GDN_DK
    rhs, q_dec, k_dec, g_end, m_k, attn = {}, {}, {}, {}, {}, {}
    for h in heads:
        q = qkv_s[:, h * GDN_DK:(h + 1) * GDN_DK]
        k = qkv_s[:, HK + h * GDN_DK:HK + (h + 1) * GDN_DK]
        v = qkv_s[:, 2 * HK + h * GDN_DV:2 * HK + (h + 1) * GDN_DV]
        qn = q * lax.rsqrt(jnp.sum(q * q, axis=-1, keepdims=True) + EPS) * (GDN_DK ** -0.5)
        kn = k * lax.rsqrt(jnp.sum(k * k, axis=-1, keepdims=True) + EPS)
        cb = d * GDN_HEADS + h
        cg = N_DIR * GDN_HEADS + d * GDN_HEADS + h
        beta_b = jnp.broadcast_to(beta_all[:, cb:cb + 1], (tm, 128))
        gc_b = jnp.broadcast_to(gc_all[:, cg:cg + 1], (tm, 128))
        gtot_b = jnp.broadcast_to(gtot_all[:, cg:cg + 1], (tm, 128))
        eg = jnp.exp(gc_b)
        kb = kn * beta_b
        rhs[h] = jnp.concatenate([v * beta_b, kb * eg], axis=1).astype(BF16)
        q_dec[h] = qn * eg
        k_dec[h] = kn * jnp.exp(gtot_b - gc_b)
        g_end[h] = jnp.exp(gtot_b)
        gc_cols = jnp.concatenate([gc_b] * (tm // 128), axis=1)
        gc_rows = gc_b.T[0:1, :]
        decay = jnp.exp(jnp.where(incl, gc_cols - gc_rows, NEG_BIG))
        kq = lax.dot_general(jnp.concatenate([kb, qn], axis=0).astype(BF16), kn.astype(BF16),
                             (((1,), (1,)), ((), ())), preferred_element_type=F32)
        m_k[h] = jnp.where(strict, -(kq[:tm] * decay), 0.0)
        attn[h] = (kq[tm:] * decay).astype(BF16)
    yield

    t_inv = {h: eye + m_k[h] for h in heads}
    p_k = {}
    for h in heads:
        m_hi, m_lo = _split_bf16(m_k[h])
        p_k[h] = _dot3(m_hi, m_lo, m_hi, m_lo)
    yield
    for _ in range(4):
        for h in heads:
            p_hi, p_lo = _split_bf16(p_k[h])
            t_hi, t_lo = _split_bf16(t_inv[h])
            r2 = _dot3(jnp.concatenate([t_hi, p_hi], axis=0), jnp.concatenate([t_lo, p_lo], axis=0), p_hi, p_lo)
            t_inv[h] = t_inv[h] + r2[:tm]
            p_k[h] = r2[tm:]
        yield
    u, w = {}, {}
    for h in heads:
        p_hi, p_lo = _split_bf16(p_k[h])
        t_hi, t_lo = _split_bf16(t_inv[h])
        t_fin = t_inv[h] + _dot3(t_hi, t_lo, p_hi, p_lo)
        sol = jnp.dot(t_fin.astype(BF16), rhs[h], preferred_element_type=F32)
        u[h] = sol[:, :GDN_DV]
        w[h] = sol[:, GDN_DV:]
    yield

    q_eff, o_free, kw = {}, {}, {}
    for h in heads:
        wu = jnp.concatenate([w[h], u[h]], axis=1).astype(BF16)
        aw = jnp.dot(attn[h], wu, preferred_element_type=F32)
        q_eff[h] = (q_dec[h] - aw[:, :GDN_DV]).astype(BF16)
        o_free[h] = aw[:, GDN_DV:]
        kd = k_dec[h].astype(BF16)
        kw[h] = [lax.dot_general(kd[c * C:(c + 1) * C], wu[c * C:(c + 1) * C], (((0,), (0,)), ((), ())),
                                 preferred_element_type=F32) for c in range(nC)]
    yield

    S = {h: S_s[h] for h in heads}
    for cc in range(nC):
        c = nC - 1 - cc if reverse else cc
        r0 = c * C
        for h in heads:
            s_b = S[h].astype(BF16)
            o_c = o_free[h][r0:r0 + C] + jnp.dot(q_eff[h][r0:r0 + C], s_b, preferred_element_type=F32)
            S[h] = (S[h] * _col(g_end[h][r0:r0 + 1, :], GDN_DK) + kw[h][c][:, GDN_DV:]
                    - jnp.dot(kw[h][c][:, :GDN_DV].astype(BF16), s_b, preferred_element_type=F32))
            if final:
                o_t = of_ref[0, r0:r0 + C, h * GDN_DV:(h + 1) * GDN_DV] + o_c
                ms = jnp.mean(o_t * o_t, axis=-1, keepdims=True)
                z = z_ref[0, r0:r0 + C, h * GDN_DV:(h + 1) * GDN_DV]
                o_ref[0, r0:r0 + C, h * GDN_DV:(h + 1) * GDN_DV] = (
                    o_t * lax.rsqrt(ms + EPS) * ng_ref[...] * (z * jax.nn.sigmoid(z)))
            else:
                o_ref[0, r0:r0 + C, h * GDN_DV:(h + 1) * GDN_DV] = o_c
        yield
    for h in heads:
        S_s[h] = S[h]


def _col(row_vec, n):
    return jnp.broadcast_to(row_vec, (n, row_vec.shape[1]))


def _gdn_pass(P, Psm, cols, conv_w, rate_row, dtb_row, n_ctx, reverse, d, of=None, norm_g=None):
    B, Tt, _ = P.shape
    tm = ROW_TILE
    W = BRANCH_WIDTH
    QW = GDN_HEADS * (2 * GDN_DK + GDN_DV)
    nt = Tt // tm
    final = of is not None

    def cur(b, j):
        return (b, _scan_tile(j, n_ctx, nt, reverse), 0)

    in_specs = _halo_specs(tm, QW, cols["gqkv"] // QW, n_ctx, nt, reverse)
    in_specs += [pl.BlockSpec((1, tm, 128), cur)]
    args = [P, P, P, Psm]
    if final:
        zb = cols["gz"] // W
        in_specs += [pl.BlockSpec((1, tm, W), lambda b, j: (b, _scan_tile(j, n_ctx, nt, reverse), zb)),
                     pl.BlockSpec((1, tm, W), cur)]
        args += [P, of]
    in_specs += [pl.BlockSpec((4, QW), lambda b, j: (0, 0)),
                 pl.BlockSpec((1, 128), lambda b, j: (0, 0)),
                 pl.BlockSpec((1, 128), lambda b, j: (0, 0))]
    args += [conv_w, rate_row, dtb_row]
    if final:
        in_specs += [pl.BlockSpec((1, GDN_DV), lambda b, j: (0, 0))]
        args += [norm_g.reshape(1, GDN_DV)]
    return pl.pallas_call(
        functools.partial(_gdn_kernel, n_ctx=n_ctx, nt=nt, reverse=reverse, final=final, d=d),
        grid=(B, nt),
        in_specs=in_specs,
        out_specs=pl.BlockSpec((1, tm, W), cur),
        out_shape=jax.ShapeDtypeStruct((B, Tt, W), F32),
        scratch_shapes=[pltpu.VMEM((GDN_HEADS, GDN_DK, GDN_DV), F32), pltpu.VMEM((tm, QW), F32)],
        compiler_params=_cparams(("parallel", "arbitrary")),
        name="gdn_bwd" if reverse else "gdn_fwd",
    )(*args)


def _post_kernel(x_ref, da_ref, lru_ref, gdn_ref, g0_ref, g1_ref, g2_ref, mod_ref, n2_ref, wb_ref, wo_ref,
                 w1_ref, w2_ref, o_ref, *, ff_chunk):
    outs = (da_ref, lru_ref, gdn_ref)
    gates = (g0_ref, g1_ref, g2_ref)
    merged = None
    for i in range(N_BRANCH):
        y = jnp.dot(outs[i][0].astype(BF16), wb_ref[i], preferred_element_type=F32)
        term = jax.nn.sigmoid(gates[i][0]) * y
        merged = term if merged is None else merged + term
    proj = jnp.dot(merged.astype(BF16), wo_ref[...], preferred_element_type=F32)
    x = x_ref[0] + mod_ref[0, 2:3, :] * proj

    ms = jnp.mean(x * x, axis=-1, keepdims=True)
    y = x * lax.rsqrt(ms + EPS) * n2_ref[...]
    h = (y * (1.0 + mod_ref[0, 4:5, :]) + mod_ref[0, 3:4, :]).astype(BF16)
    n_ff = w1_ref.shape[1]
    acc = None
    for c0 in range(0, n_ff, ff_chunk):
        a = jnp.dot(h, w1_ref[:, c0:c0 + ff_chunk], preferred_element_type=F32)
        a = jnp.square(jnp.maximum(a, 0.0)).astype(BF16)
        part = jnp.dot(a, w2_ref[c0:c0 + ff_chunk, :], preferred_element_type=F32)
        acc = part if acc is None else acc + part
    o_ref[0] = x + mod_ref[0, 5:6, :] * acc


def _post(xa, da, lru, gdn, P, mod3, n2_g, wb, wo, w1, w2, n_lat_tiles, ctx_row, n_rows):
    B, Tt, D = xa.shape
    tm = ROW_TILE
    W = BRANCH_WIDTH
    F = w1.shape[1]
    row = lambda b, t: (b, t, 0)
    once = pl.Buffered(1)
    return pl.pallas_call(
        functools.partial(_post_kernel, ff_chunk=min(F, 1024)),
        grid=(B, n_rows // tm),
        in_specs=[pl.BlockSpec((1, tm, D), row),
                  pl.BlockSpec((1, tm, W), row), pl.BlockSpec((1, tm, W), row), pl.BlockSpec((1, tm, W), row),
                  pl.BlockSpec((1, tm, D), lambda b, t: (b, t, 0)),
                  pl.BlockSpec((1, tm, D), lambda b, t: (b, t, 1)),
                  pl.BlockSpec((1, tm, D), lambda b, t: (b, t, 2)),
                  pl.BlockSpec((1, N_MOD, D), lambda b, t: (jnp.where(t >= n_lat_tiles, ctx_row, b), 0, 0)),
                  pl.BlockSpec((1, D), lambda b, t: (0, 0)),
                  pl.BlockSpec((N_BRANCH, W, D), lambda b, t: (0, 0, 0), pipeline_mode=once),
                  pl.BlockSpec((D, D), lambda b, t: (0, 0), pipeline_mode=once),
                  pl.BlockSpec((D, F), lambda b, t: (0, 0), pipeline_mode=once),
                  pl.BlockSpec((F, D), lambda b, t: (0, 0), pipeline_mode=once)],
        out_specs=pl.BlockSpec((1, tm, D), row),
        out_shape=jax.ShapeDtypeStruct((B, n_rows, D), F32),
        compiler_params=_cparams(("parallel", "parallel")),
        name="merge_mlp",
    )(xa, da, lru, gdn, P, P, P, mod3, n2_g.reshape(1, D), wb, wo, w1, w2)


def _proj_layout(D):
    W = BRANCH_WIDTH
    QW = GDN_HEADS * (2 * GDN_DK + GDN_DV)
    src = {}
    off = 0
    for name, width in (("q", W), ("k", W), ("v", W), ("lx", W), ("ly", W), ("gqkv", QW), ("gz", W),
                        ("gb", N_DIR * GDN_HEADS), ("ga", N_DIR * GDN_HEADS), ("gate", N_BRANCH * D)):
        src[name] = (off, width)
        off += width
    order = ("gate", "gqkv", "lx", "ly", "gz")
    cols = {}
    o = 0
    for name in order:
        cols[name] = o
        o += src[name][1]
    return cols, [src[n] for n in order], [src[n] for n in ("q", "k", "v")], [src[n] for n in ("gb", "ga")]


def _take_cols(w, ranges):
    return jnp.concatenate([w[:, s:s + n] for s, n in ranges], axis=1)


def _rope_tables(n_ctx_rows, n_lat_rows):
    n_freq = DA_QK_DIM // 4
    inv = ROPE_BASE ** (-jnp.arange(n_freq, dtype=F32) / n_freq)
    tpos = jnp.arange(n_lat_rows, dtype=jnp.int32)
    ang_r = (tpos // GRID_W).astype(F32)[:, None] * inv
    ang_c = (tpos % GRID_W).astype(F32)[:, None] * inv
    ang = jnp.concatenate([ang_r, ang_r, ang_c, ang_c], axis=-1)
    ang = jnp.concatenate([ang, ang], axis=-1)
    sign = jnp.where((jnp.arange(128) % 32) < 16, -1.0, 1.0).astype(F32)
    cos = jnp.concatenate([jnp.cos(ang), jnp.ones((n_ctx_rows, 128), F32)], axis=0)
    sin = jnp.concatenate([jnp.sin(ang) * sign, jnp.zeros((n_ctx_rows, 128), F32)], axis=0)
    return cos, sin


def kernel(x, c, ctx, c_ctx, ada_w, ada_b, norm1_g, norm2_g, w_in, da_q_norm_g, da_k_norm_g, da_lambda, da_sub_norm_g, lru_conv_w, lru_conv_b, lru_gate_w, lru_gate_b, lru_lambda, gdn_conv_w, gdn_A_log, gdn_dt_bias, gdn_norm_g, w_branch, w_out, mlp_w1, mlp_w2):
    B, T, D = x.shape
    Tc = ctx.shape[1]
    depth = ada_w.shape[0]
    tm = ROW_TILE
    assert Tc % tm == 0 and T % tm == 0 and B + 1 <= 8
    n_ctx = Tc // tm
    n_lat = T // tm
    Tt = Tc + T
    ctx_row = B

    cols, main_cols, qkv_cols, small_cols = _proj_layout(D)
    cos_t, sin_t = _rope_tables(Tc, T)
    xa = jnp.concatenate([x, ctx], axis=1)
    c_all = jnp.concatenate([c, c_ctx[None, :], jnp.zeros((8 - B - 1, D), F32)], axis=0)

    tq = _pick(T, (1024, 512, 256))
    n_sub = _pick(Tt // tm, (3, 2, 1))

    out = None
    for layer in range(depth):
        last = layer == depth - 1
        lam_init = 0.8 - 0.6 * math.exp(-0.3 * layer)
        w_main = _take_cols(w_in[layer], main_cols).astype(BF16)
        w_qkv = _take_cols(w_in[layer], qkv_cols).astype(BF16)
        w_small = _take_cols(w_in[layer], small_cols)
        w_small = jnp.pad(w_small, ((0, 0), (0, 128 - w_small.shape[1]))).astype(BF16)

        mod3 = _adaln(c_all, ada_w[layer], ada_b[layer]).reshape(8, N_MOD, D)
        P, Psm, QT, Ks, VT = _front(xa, mod3, norm1_g[layer], w_main, w_qkv, w_small, cos_t, sin_t,
                                    da_q_norm_g[layer], da_k_norm_g[layer], n_lat, ctx_row)

        da_l = _flash(QT, Ks, VT, da_lambda[layer], da_sub_norm_g[layer], lam_init, 0, T, 0, Tt, tq, n_sub)
        if last:
            da = da_l
        else:
            da_c = _flash(QT, Ks, VT, da_lambda[layer], da_sub_norm_g[layer], lam_init, T, Tc, T, Tc, tm, 1)
            da = jnp.concatenate([da_l, da_c], axis=1)

        hf = _lru_pass(P, cols, lru_conv_w[layer], lru_conv_b[layer], _lru_gate_weights(lru_gate_w[layer, 0]),
                       lru_gate_b[layer, 0].reshape(-1), lru_lambda[layer, 0], n_ctx, False)
        lru = _lru_pass(P, cols, lru_conv_w[layer], lru_conv_b[layer], _lru_gate_weights(lru_gate_w[layer, 1]),
                        lru_gate_b[layer, 1].reshape(-1), lru_lambda[layer, 1], n_ctx, True, hf=hf)

        rate_row = jnp.zeros((1, 128), F32).at[0, N_DIR * GDN_HEADS:2 * N_DIR * GDN_HEADS].set(
            gdn_A_log[layer].astype(F32).reshape(-1))
        dtb_row = jnp.zeros((1, 128), F32).at[0, N_DIR * GDN_HEADS:2 * N_DIR * GDN_HEADS].set(
            gdn_dt_bias[layer].astype(F32).reshape(-1))
        of = _gdn_pass(P, Psm, cols, gdn_conv_w[layer], rate_row, dtb_row, n_ctx, False, 0)
        gdn = _gdn_pass(P, Psm, cols, gdn_conv_w[layer], rate_row, dtb_row, n_ctx, True, 1, of=of,
                        norm_g=gdn_norm_g[layer])

        res = _post(xa, da, lru, gdn, P, mod3, norm2_g[layer], w_branch[layer].astype(BF16),
                    w_out[layer].astype(BF16), mlp_w1[layer].astype(BF16), mlp_w2[layer].astype(BF16),
                    n_lat, ctx_row, T if last else Tt)
        if last:
            out = res
        else:
            xa = res
    return out
```

```python
import functools
import math

import jax
import jax.numpy as jnp
from jax import lax
from jax.experimental import pallas as pl
from jax.experimental.pallas import tpu as pltpu

F32 = jnp.float32
BF16 = jnp.bfloat16
HIGHEST = lax.Precision.HIGHEST

N_MOD = 6
BRANCH_WIDTH = 512
N_BRANCH = 3
DA_HEADS = 4
DA_QK_DIM = 64
DA_V_DIM = 128
ROPE_BASE = 10000.0
LRU_BLOCKS = 8
LRU_C = 8.0
GDN_HEADS = 4
GDN_DK = 128
GDN_DV = 128
GDN_CHUNK = 64
N_DIR = 2
GRID_W = 64
EPS = 1e-6
NEG_BIG = -1e30

ROW_TILE = 256
HALO_ROWS = 8
VT_PAD = 16
GDN_HEAD_GROUPS = ((0, 1), (2, 3))
GDN_STAGE_SKEW = 2
VMEM_LIMIT = 48 * 1024 * 1024


def _cparams(sem):
    return pltpu.CompilerParams(dimension_semantics=sem, vmem_limit_bytes=VMEM_LIMIT)


def _softplus(z):
    return jnp.maximum(z, 0.0) + jnp.log1p(jnp.exp(-jnp.abs(z)))


def _adaln_kernel(c_ref, w_ref, b_ref, o_ref):
    c = c_ref[...]
    a = c * jax.nn.sigmoid(c)
    o_ref[...] = jnp.dot(a, w_ref[...], preferred_element_type=F32, precision=HIGHEST) + b_ref[...]


def _adaln(c_all, w, b):
    R, D = c_all.shape
    N = w.shape[1]
    tn = 1536 if N % 1536 == 0 else N
    return pl.pallas_call(
        _adaln_kernel,
        grid=(N // tn,),
        in_specs=[pl.BlockSpec((R, D), lambda j: (0, 0)),
                  pl.BlockSpec((D, tn), lambda j: (0, j)),
                  pl.BlockSpec((1, tn), lambda j: (0, j))],
        out_specs=pl.BlockSpec((R, tn), lambda j: (0, j)),
        out_shape=jax.ShapeDtypeStruct((R, N), F32),
        compiler_params=_cparams(("arbitrary",)),
        name="adaln",
    )(c_all, w, b.reshape(1, N))


def _pick(n, cands):
    for c in cands:
        if n % c == 0:
            return c
    return n


def _seg_mean(x2, s_ref):
    hi = x2.astype(BF16)
    lo = (x2 - hi.astype(F32)).astype(BF16)
    s = s_ref[...]
    return jnp.dot(hi, s, preferred_element_type=F32) + jnp.dot(lo, s, preferred_element_type=F32)


def _rope(y, cos, sin_signed, first_half):
    rot = jnp.where(first_half, pltpu.roll(y, 128 - 16, 1), pltpu.roll(y, 16, 1))
    return y * cos + rot * sin_signed


def _front_kernel(x_ref, xc_ref, mod_ref, g_ref, wm_ref, wq_ref, ws_ref, cos_ref, sin_ref, gq_ref, gk_ref, s_ref,
                  p_ref, ps_ref, qo_ref, ko_ref, vo_ref, *, q_scale, col_chunk, n_lat_tiles):
    x = jnp.where(pl.program_id(1) >= n_lat_tiles, xc_ref[0], x_ref[0])
    ms = jnp.mean(x * x, axis=-1, keepdims=True)
    y = x * lax.rsqrt(ms + EPS) * g_ref[...]
    hb = (y * (1.0 + mod_ref[0, 1:2, :]) + mod_ref[0, 0:1, :]).astype(BF16)
    qkv = jnp.dot(hb, wq_ref[...], preferred_element_type=F32)
    main_chunks = list(range(0, wm_ref.shape[1], col_chunk))

    def project(c0):
        p_ref[0, :, c0:c0 + col_chunk] = jnp.dot(hb, wm_ref[:, c0:c0 + col_chunk], preferred_element_type=F32)

    project(main_chunks.pop(0))
    W = BRANCH_WIDTH

    cos = cos_ref[...]
    sin = sin_ref[...]
    tm = cos.shape[0]
    lane = lax.broadcasted_iota(jnp.int32, (tm, 128), 1)
    first_half = (lane % 32) < 16
    comp0 = lane < DA_QK_DIM

    q = qkv[:, :W]
    qn = q * lax.rsqrt(_seg_mean(q * q, s_ref) + EPS) * gq_ref[...]
    k = qkv[:, W:2 * W]
    kn = k * lax.rsqrt(_seg_mean(k * k, s_ref) + EPS) * gk_ref[...]
    v = qkv[:, 2 * W:]
    ones_row = jnp.where(lax.broadcasted_iota(jnp.int32, (VT_PAD, tm), 0) == 0, 1.0, 0.0).astype(BF16)
    for h in range(DA_HEADS):
        qh = _rope(qn[:, h * 128:(h + 1) * 128], cos, sin, first_half) * q_scale
        qo_ref[0, h * 128:(h + 1) * 128, :] = qh.T.astype(BF16)
        kh = _rope(kn[:, h * 128:(h + 1) * 128], cos, sin, first_half)
        ko_ref[0, :, h * 256:h * 256 + 128] = jnp.where(comp0, kh, 0.0).astype(BF16)
        ko_ref[0, :, h * 256 + 128:(h + 1) * 256] = jnp.where(comp0, 0.0, kh).astype(BF16)
        vo_ref[0, h, 0, 0:DA_V_DIM, :] = v[:, h * DA_V_DIM:(h + 1) * DA_V_DIM].T.astype(BF16)
        vo_ref[0, h, 0, DA_V_DIM:DA_V_DIM + VT_PAD, :] = ones_row
        if main_chunks:
            project(main_chunks.pop(0))
    for c0 in main_chunks:
        project(c0)
    ps_ref[0] = jnp.dot(hb, ws_ref[...], preferred_element_type=F32)


def _front(x_lat, x_ctx, ctx_blk0, mod3, g, w_main, w_qkv, w_small, cos_t, sin_t, gq, gk, n_lat_tiles, ctx_row):
    B, _, D = x_lat.shape
    Tt = cos_t.shape[0]
    tm = ROW_TILE
    W = BRANCH_WIDTH
    NM = w_main.shape[1]
    seg = (jnp.arange(W)[:, None] // DA_QK_DIM == jnp.arange(W)[None, :] // DA_QK_DIM)
    smat = (seg.astype(F32) / DA_QK_DIM).astype(BF16)
    gq_t = jnp.tile(gq, W // DA_QK_DIM).reshape(1, W)
    gk_t = jnp.tile(gk, W // DA_QK_DIM).reshape(1, W)
    q_scale = (DA_QK_DIM ** -0.5) * math.log2(math.e)
    once = pl.Buffered(1)
    const = lambda b, t: (0, 0)
    return pl.pallas_call(
        functools.partial(_front_kernel, q_scale=q_scale, col_chunk=_pick(NM, (1536, 1024, 512)),
                          n_lat_tiles=n_lat_tiles),
        grid=(B, Tt // tm),
        in_specs=[pl.BlockSpec((1, tm, D), lambda b, t: (b, jnp.minimum(t, n_lat_tiles - 1), 0)),
                  pl.BlockSpec((1, tm, D), lambda b, t: (b, ctx_blk0 + jnp.maximum(t - n_lat_tiles, 0), 0)),
                  pl.BlockSpec((1, N_MOD, D), lambda b, t: (jnp.where(t >= n_lat_tiles, ctx_row, b), 0, 0)),
                  pl.BlockSpec((1, D), const),
                  pl.BlockSpec((D, NM), const, pipeline_mode=once),
                  pl.BlockSpec((D, 3 * W), const, pipeline_mode=once),
                  pl.BlockSpec((D, 128), const, pipeline_mode=once),
                  pl.BlockSpec((tm, 128), lambda b, t: (t, 0)),
                  pl.BlockSpec((tm, 128), lambda b, t: (t, 0)),
                  pl.BlockSpec((1, W), const),
                  pl.BlockSpec((1, W), const),
                  pl.BlockSpec((W, W), const)],
        out_specs=[pl.BlockSpec((1, tm, NM), lambda b, t: (b, t, 0)),
                   pl.BlockSpec((1, tm, 128), lambda b, t: (b, t, 0)),
                   pl.BlockSpec((1, W, tm), lambda b, t: (b, 0, t)),
                   pl.BlockSpec((1, tm, 2 * W), lambda b, t: (b, t, 0)),
                   pl.BlockSpec((1, DA_HEADS, 1, DA_V_DIM + VT_PAD, tm), lambda b, t: (b, 0, t, 0, 0))],
        out_shape=[jax.ShapeDtypeStruct((B, Tt, NM), F32),
                   jax.ShapeDtypeStruct((B, Tt, 128), F32),
                   jax.ShapeDtypeStruct((B, W, Tt), BF16),
                   jax.ShapeDtypeStruct((B, Tt, 2 * W), BF16),
                   jax.ShapeDtypeStruct((B, DA_HEADS, Tt // tm, DA_V_DIM + VT_PAD, tm), BF16)],
        compiler_params=_cparams(("parallel", "parallel")),
        name="front",
    )(x_lat, x_ctx, mod3, g.reshape(1, D), w_main, w_qkv, w_small, cos_t, sin_t, gq_t, gk_t, smat)


def _flash_kernel(qt_ref, k_ref, vt_ref, lam_ref, subg_ref, o_ref, acc_s,
                  *, tq, sub, n_sub, nk, chunk0, lam_init):
    qt = qt_ref[0]
    acc_s[...] = jnp.zeros(acc_s.shape, F32)
    tk = sub * n_sub

    def scores(j):
        r0 = (chunk0 + j * n_sub) * sub
        kk = k_ref[0, r0:r0 + tk, :]
        k2 = jnp.concatenate([kk[:, :128], kk[:, 128:]], axis=0)
        return jnp.dot(k2, qt, preferred_element_type=F32)

    def accumulate(pend):
        j, ps, alphas = pend
        c0 = chunk0 + j * n_sub
        vt = jnp.concatenate([vt_ref[0, 0, c0 + i] for i in range(n_sub)], axis=1)
        for c in range(2):
            acc_s[c] = alphas[c] * acc_s[c] + jnp.dot(vt, ps[c], preferred_element_type=F32)

    ms = [jnp.full((1, tq), -jnp.inf, F32)] * 2
    st = scores(0)
    pend = None
    for j in range(nk):
        st_next = scores(j + 1) if j + 1 < nk else None
        if pend is not None:
            accumulate(pend)
        ps, alphas = [], []
        for c in range(2):
            s_c = st[c * tk:(c + 1) * tk]
            m_new = jnp.maximum(ms[c], jnp.max(s_c, axis=0, keepdims=True))
            alphas.append(jnp.exp2(ms[c] - m_new))
            ps.append(jnp.exp2(s_c - m_new).astype(BF16))
            ms[c] = m_new
        pend = (j, ps, alphas)
        st = st_next
    accumulate(pend)

    a0 = acc_s[0]
    a1 = acc_s[1]
    o0 = a0[:DA_V_DIM] / a0[DA_V_DIM:DA_V_DIM + 1]
    o1 = a1[:DA_V_DIM] / a1[DA_V_DIM:DA_V_DIM + 1]
    lv = lam_ref[...]
    s01 = jnp.sum(lv[0:1] * lv[1:2], axis=-1, keepdims=True)
    s23 = jnp.sum(lv[2:3] * lv[3:4], axis=-1, keepdims=True)
    lam = jnp.exp(s01) - jnp.exp(s23) + lam_init
    d = o0 - lam * o1
    ms_ = jnp.mean(d * d, axis=0, keepdims=True)
    o_ref[0] = (d * lax.rsqrt(ms_ + EPS)).T * subg_ref[...] * (1.0 - lam_init)


def _flash(QT, Ks, VT, lam_vec, sub_g, lam_init, q_row0, n_q_rows, k_row0, n_k_rows, tq, n_sub):
    B, _, Tt = QT.shape
    sub = VT.shape[-1]
    tk = sub * n_sub
    nq, nk = n_q_rows // tq, n_k_rows // tk
    q0 = q_row0 // tq
    vrows = VT.shape[3]
    return pl.pallas_call(
        functools.partial(_flash_kernel, tq=tq, sub=sub, n_sub=n_sub, nk=nk, chunk0=k_row0 // sub,
                          lam_init=lam_init),
        grid=(B, DA_HEADS, nq),
        in_specs=[pl.BlockSpec((1, 128, tq), lambda b, h, i: (b, h, q0 + i)),
                  pl.BlockSpec((1, Tt, 256), lambda b, h, i: (b, 0, h)),
                  pl.BlockSpec((1, 1, VT.shape[2], vrows, sub), lambda b, h, i: (b, h, 0, 0, 0)),
                  pl.BlockSpec((4, DA_QK_DIM), lambda b, h, i: (0, 0)),
                  pl.BlockSpec((1, DA_V_DIM), lambda b, h, i: (0, 0))],
        out_specs=pl.BlockSpec((1, tq, 128), lambda b, h, i: (b, i, h)),
        out_shape=jax.ShapeDtypeStruct((B, n_q_rows, BRANCH_WIDTH), F32),
        scratch_shapes=[pltpu.VMEM((2, vrows, tq), F32)],
        compiler_params=_cparams(("parallel", "parallel", "arbitrary")),
        name="diff_attn",
    )(QT, Ks, VT, lam_vec, sub_g.reshape(1, DA_V_DIM))


def _scan_tile(j, n_ctx, nt, reverse):
    n_lat = nt - n_ctx
    if not reverse:
        return jnp.where(j < n_ctx, n_lat + j, j - n_ctx)
    return jnp.where(j < n_ctx, nt - 1 - j, n_lat - 1 - (j - n_ctx))


def _stream_edges(t, n_ctx, nt):
    n_lat = nt - n_ctx
    return jnp.logical_or(t == 0, t == n_lat), jnp.logical_or(t == n_lat - 1, t == nt - 1)


def _dwconv4(x, prev8, next8, w_ref, is_first, is_last):
    tm = x.shape[0]
    row = lax.broadcasted_iota(jnp.int32, x.shape, 0)
    pf = jnp.where(is_first, 0.0, 1.0)
    nf = jnp.where(is_last, 0.0, 1.0)
    prow = prev8[HALO_ROWS - 1:HALO_ROWS, :] * pf
    n0 = next8[0:1, :] * nf
    n1 = next8[1:2, :] * nf
    xm1 = jnp.where(row == 0, prow, pltpu.roll(x, 1, 0))
    xp1 = jnp.where(row == tm - 1, n0, pltpu.roll(x, tm - 1, 0))
    xp2 = jnp.where(row == tm - 2, n0, jnp.where(row == tm - 1, n1, pltpu.roll(x, tm - 2, 0)))
    return w_ref[0:1, :] * xm1 + w_ref[1:2, :] * x + w_ref[2:3, :] * xp1 + w_ref[3:4, :] * xp2


def _halo_specs(tm, width, col_blk, n_ctx, nt, reverse):
    r8 = tm // HALO_ROWS
    last8 = nt * r8 - 1

    def cur(b, j):
        return (b, _scan_tile(j, n_ctx, nt, reverse), col_blk)

    def prev(b, j):
        return (b, jnp.maximum(_scan_tile(j, n_ctx, nt, reverse) * r8 - 1, 0), col_blk)

    def nxt(b, j):
        return (b, jnp.minimum((_scan_tile(j, n_ctx, nt, reverse) + 1) * r8, last8), col_blk)

    return [pl.BlockSpec((1, tm, width), cur),
            pl.BlockSpec((1, HALO_ROWS, width), prev),
            pl.BlockSpec((1, HALO_ROWS, width), nxt)]


def _linear_scan_tile(a, b, h0, reverse):
    tm, W = a.shape
    G = HALO_ROWS
    n_groups = tm // G
    a = a.reshape(n_groups, G, W)
    b = b.reshape(n_groups, G, W)
    sub = lax.broadcasted_iota(jnp.int32, a.shape, 1)
    s = 1
    while s < G:
        if not reverse:
            keep = sub >= s
            ap = jnp.where(keep, pltpu.roll(a, s, 1), 1.0)
            bp = jnp.where(keep, pltpu.roll(b, s, 1), 0.0)
        else:
            keep = sub < G - s
            ap = jnp.where(keep, pltpu.roll(a, G - s, 1), 1.0)
            bp = jnp.where(keep, pltpu.roll(b, G - s, 1), 0.0)
        b = a * bp + b
        a = a * ap
        s *= 2
    outs = [None] * n_groups
    carry = h0
    for g in (range(n_groups - 1, -1, -1) if reverse else range(n_groups)):
        hg = b[g] + a[g] * carry
        outs[g] = hg
        carry = hg[0:1] if reverse else hg[G - 1:G]
    return jnp.concatenate(outs, axis=0), carry


def _lru_kernel(*refs, n_ctx, nt, reverse, final):
    if final:
        (x_ref, xp_ref, xn_ref, y_ref, hf_ref, cw_ref, cb_ref, wg_ref, gb_ref, lam_ref, o_ref, carry) = refs
    else:
        (x_ref, xp_ref, xn_ref, cw_ref, cb_ref, wg_ref, gb_ref, lam_ref, o_ref, carry) = refs
    j = pl.program_id(1)
    t = _scan_tile(j, n_ctx, nt, reverse)
    is_first, is_last = _stream_edges(t, n_ctx, nt)

    @pl.when(j == 0)
    def _():
        carry[...] = jnp.zeros(carry.shape, F32)

    x = x_ref[0]
    tm, W = x.shape
    xc = _dwconv4(x, xp_ref[0], xn_ref[0], cw_ref, is_first, is_last) + cb_ref[...]
    g = jnp.dot(xc.astype(BF16), wg_ref[...], preferred_element_type=F32) + gb_ref[...]
    r = jax.nn.sigmoid(g[:, :W])
    i = jax.nn.sigmoid(g[:, W:])
    log_a = (-LRU_C) * r * _softplus(-lam_ref[...])
    a = jnp.exp(log_a)
    th = jnp.tanh(log_a)
    bb = jnp.sqrt(-2.0 * th / (1.0 - th)) * (i * xc)
    h, h_last = _linear_scan_tile(a, bb, carry[0:1, :], reverse)
    carry[0:1, :] = h_last
    if final:
        o_ref[0] = jax.nn.gelu(y_ref[0]) * (hf_ref[0] + h)
    else:
        o_ref[0] = h


def _lru_pass(P, cols, conv_w, conv_b, wg, gb, lam, n_ctx, reverse, hf=None):
    B, Tt, _ = P.shape
    tm = ROW_TILE
    W = BRANCH_WIDTH
    nt = Tt // tm
    final = hf is not None

    def cur(b, j):
        return (b, _scan_tile(j, n_ctx, nt, reverse), 0)

    in_specs = _halo_specs(tm, W, cols["lx"] // W, n_ctx, nt, reverse)
    args = [P, P, P]
    if final:
        yb = cols["ly"] // W
        in_specs += [pl.BlockSpec((1, tm, W), lambda b, j: (b, _scan_tile(j, n_ctx, nt, reverse), yb)),
                     pl.BlockSpec((1, tm, W), cur)]
        args += [P, hf]
    in_specs += [pl.BlockSpec((4, W), lambda b, j: (0, 0)),
                 pl.BlockSpec((1, W), lambda b, j: (0, 0)),
                 pl.BlockSpec((W, 2 * W), lambda b, j: (0, 0)),
                 pl.BlockSpec((1, 2 * W), lambda b, j: (0, 0)),
                 pl.BlockSpec((1, W), lambda b, j: (0, 0))]
    args += [conv_w, conv_b.reshape(1, W), wg, gb.reshape(1, 2 * W), lam.reshape(1, W)]
    return pl.pallas_call(
        functools.partial(_lru_kernel, n_ctx=n_ctx, nt=nt, reverse=reverse, final=final),
        grid=(B, nt),
        in_specs=in_specs,
        out_specs=pl.BlockSpec((1, tm, W), cur),
        out_shape=jax.ShapeDtypeStruct((B, Tt, W), F32),
        scratch_shapes=[pltpu.VMEM((HALO_ROWS, W), F32)],
        compiler_params=_cparams(("parallel", "arbitrary")),
        name="rglru_bwd" if reverse else "rglru_fwd",
    )(*args)


def _lru_gate_weights(gate_w_d):
    mats = []
    for g in range(2):
        mats.append(jax.scipy.linalg.block_diag(*[gate_w_d[g, n] for n in range(LRU_BLOCKS)]))
    return jnp.concatenate(mats, axis=1).astype(BF16)


def _split_bf16(x):
    hi = x.astype(BF16)
    lo = (x - hi.astype(F32)).astype(BF16)
    return hi, lo


def _dot3(a_hi, a_lo, b_hi, b_lo):
    return (jnp.dot(a_hi, b_hi, preferred_element_type=F32)
            + (jnp.dot(a_hi, b_lo, preferred_element_type=F32) + jnp.dot(a_lo, b_hi, preferred_element_type=F32)))


def _gdn_kernel(*refs, n_ctx, nt, reverse, final, d):
    if final:
        (x_ref, xp_ref, xn_ref, sm_ref, z_ref, of_ref, cw_ref, rate_ref, dtb_ref, ng_ref, o_ref, S_s, qkv_s) = refs
    else:
        (x_ref, xp_ref, xn_ref, sm_ref, cw_ref, rate_ref, dtb_ref, o_ref, S_s, qkv_s) = refs
    j = pl.program_id(1)
    t = _scan_tile(j, n_ctx, nt, reverse)
    is_first, is_last = _stream_edges(t, n_ctx, nt)

    @pl.when(j == 0)
    def _():
        S_s[...] = jnp.zeros(S_s.shape, F32)

    x = x_ref[0]
    tm = x.shape[0]
    C = GDN_CHUNK
    nC = tm // C
    qkv = _dwconv4(x, xp_ref[0], xn_ref[0], cw_ref, is_first, is_last)
    qkv_s[...] = qkv * jax.nn.sigmoid(qkv)

    sm = sm_ref[0]
    beta_all = jax.nn.sigmoid(sm)
    g_all = -jnp.exp(rate_ref[...]) * _softplus(sm + dtb_ref[...])

    ii = lax.broadcasted_iota(jnp.int32, (tm, tm), 0)
    jj = lax.broadcasted_iota(jnp.int32, (tm, tm), 1)
    same = (ii // C) == (jj // C)
    if not reverse:
        incl = jnp.logical_and(same, ii >= jj)
        strict = jnp.logical_and(same, ii > jj)
    else:
        incl = jnp.logical_and(same, ii <= jj)
        strict = jnp.logical_and(same, ii < jj)
    incl_b = incl.astype(F32).astype(BF16)
    eye = (ii == jj).astype(F32)

    g1 = g_all.astype(BF16)
    g2 = (g_all - g1.astype(F32)).astype(BF16)
    g3 = (g_all - g1.astype(F32) - g2.astype(F32)).astype(BF16)
    gc_all = (jnp.dot(incl_b, g1, preferred_element_type=F32)
              + (jnp.dot(incl_b, g2, preferred_element_type=F32) + jnp.dot(incl_b, g3, preferred_element_type=F32)))
    last = 0 if reverse else C - 1
    gtot_all = jnp.concatenate(
        [jnp.broadcast_to(gc_all[c * C + last:c * C + last + 1, :], (C, 128)) for c in range(nC)], axis=0)

    groups = [_gdn_group_stages(hs, d=d, tm=tm, reverse=reverse, final=final, qkv_s=qkv_s, S_s=S_s, o_ref=o_ref,
                                of_ref=of_ref if final else None, z_ref=z_ref if final else None,
                                ng_ref=ng_ref if final else None, beta_all=beta_all, gc_all=gc_all,
                                gtot_all=gtot_all, incl=incl, strict=strict, eye=eye)
              for hs in GDN_HEAD_GROUPS]
    _run_skewed(groups, GDN_STAGE_SKEW)


def _run_skewed(gens, skew):
    done = [False] * len(gens)
    rnd = 0
    while not all(done):
        for k, g in enumerate(gens):
            if done[k] or rnd < k * skew:
                continue
            try:
                next(g)
            except StopIteration:
                done[k] = True
        rnd += 1


def _gdn_group_stages(heads, *, d, tm, reverse, final, qkv_s, S_s, o_ref, of_ref, z_ref, ng_ref, beta_all, gc_all,
                      gtot_all, incl, strict, eye):
    C = GDN_CHUNK
    nC = tm // C
    HK = GDN_HEADS * GDN_DK
    rhs, q_dec, k_dec, g_end, m_k, attn = {}, {}, {}, {}, {}, {}
    for h in heads:
        q = qkv_s[:, h * GDN_DK:(h + 1) * GDN_DK]
        k = qkv_s[:, HK + h * GDN_DK:HK + (h + 1) * GDN_DK]
        v = qkv_s[:, 2 * HK + h * GDN_DV:2 * HK + (h + 1) * GDN_DV]
        qn = q * lax.rsqrt(jnp.sum(q * q, axis=-1, keepdims=True) + EPS) * (GDN_DK ** -0.5)
        kn = k * lax.rsqrt(jnp.sum(k * k, axis=-1, keepdims=True) + EPS)
        cb = d * GDN_HEADS + h
        cg = N_DIR * GDN_HEADS + d * GDN_HEADS + h
        beta_b = jnp.broadcast_to(beta_all[:, cb:cb + 1], (tm, 128))
        gc_b = jnp.broadcast_to(gc_all[:, cg:cg + 1], (tm, 128))
        gtot_b = jnp.broadcast_to(gtot_all[:, cg:cg + 1], (tm, 128))
        eg = jnp.exp(gc_b)
        kb = kn * beta_b
        rhs[h] = jnp.concatenate([v * beta_b, kb * eg], axis=1).astype(BF16)
        q_dec[h] = qn * eg
        k_dec[h] = kn * jnp.exp(gtot_b - gc_b)
        g_end[h] = jnp.exp(gtot_b)
        gc_cols = jnp.concatenate([gc_b] * (tm // 128), axis=1)
        gc_rows = gc_b.T[0:1, :]
        decay = jnp.exp(jnp.where(incl, gc_cols - gc_rows, NEG_BIG))
        kq = lax.dot_general(jnp.concatenate([kb, qn], axis=0).astype(BF16), kn.astype(BF16),
                             (((1,), (1,)), ((), ())), preferred_element_type=F32)
        m_k[h] = jnp.where(strict, -(kq[:tm] * decay), 0.0)
        attn[h] = (kq[tm:] * decay).astype(BF16)
    yield

    t_inv = {h: eye + m_k[h] for h in heads}
    p_k = {}
    for h in heads:
        m_hi, m_lo = _split_bf16(m_k[h])
        p_k[h] = _dot3(m_hi, m_lo, m_hi, m_lo)
    yield
    for _ in range(4):
        for h in heads:
            p_hi, p_lo = _split_bf16(p_k[h])
            t_hi, t_lo = _split_bf16(t_inv[h])
            r2 = _dot3(jnp.concatenate([t_hi, p_hi], axis=0), jnp.concatenate([t_lo, p_lo], axis=0), p_hi, p_lo)
            t_inv[h] = t_inv[h] + r2[:tm]
            p_k[h] = r2[tm:]
        yield
    u, w = {}, {}
    for h in heads:
        p_hi, p_lo = _split_bf16(p_k[h])
        t_hi, t_lo = _split_bf16(t_inv[h])
        t_fin = t_inv[h] + _dot3(t_hi, t_lo, p_hi, p_lo)
        sol = jnp.dot(t_fin.astype(BF16), rhs[h], preferred_element_type=F32)
        u[h] = sol[:, :GDN_DV]
        w[h] = sol[:, GDN_DV:]
    yield

    q_eff, o_free, kw = {}, {}, {}
    for h in heads:
        wu = jnp.concatenate([w[h], u[h]], axis=1).astype(BF16)
        aw = jnp.dot(attn[h], wu, preferred_element_type=F32)
        q_eff[h] = (q_dec[h] - aw[:, :GDN_DV]).astype(BF16)
        o_free[h] = aw[:, GDN_DV:]
        kd = k_dec[h].astype(BF16)
        kw[h] = [lax.dot_general(kd[c * C:(c + 1) * C], wu[c * C:(c + 1) * C], (((0,), (0,)), ((), ())),
                                 preferred_element_type=F32) for c in range(nC)]
    yield

    S = {h: S_s[h] for h in heads}
    for cc in range(nC):
        c = nC - 1 - cc if reverse else cc
        r0 = c * C
        for h in heads:
            s_b = S[h].astype(BF16)
            o_c = o_free[h][r0:r0 + C] + jnp.dot(q_eff[h][r0:r0 + C], s_b, preferred_element_type=F32)
            S[h] = (S[h] * _col(g_end[h][r0:r0 + 1, :], GDN_DK) + kw[h][c][:, GDN_DV:]
                    - jnp.dot(kw[h][c][:, :GDN_DV].astype(BF16), s_b, preferred_element_type=F32))
            if final:
                o_t = of_ref[0, r0:r0 + C, h * GDN_DV:(h + 1) * GDN_DV] + o_c
                ms = jnp.mean(o_t * o_t, axis=-1, keepdims=True)
                z = z_ref[0, r0:r0 + C, h * GDN_DV:(h + 1) * GDN_DV]
                o_ref[0, r0:r0 + C, h * GDN_DV:(h + 1) * GDN_DV] = (
                    o_t * lax.rsqrt(ms + EPS) * ng_ref[...] * (z * jax.nn.sigmoid(z)))
            else:
                o_ref[0, r0:r0 + C, h * GDN_DV:(h + 1) * GDN_DV] = o_c
        yield
    for h in heads:
        S_s[h] = S[h]


def _col(row_vec, n):
    return jnp.broadcast_to(row_vec, (n, row_vec.shape[1]))


def _gdn_pass(P, Psm, cols, conv_w, rate_row, dtb_row, n_ctx, reverse, d, of=None, norm_g=None):
    B, Tt, _ = P.shape
    tm = ROW_TILE
    W = BRANCH_WIDTH
    QW = GDN_HEADS * (2 * GDN_DK + GDN_DV)
    nt = Tt // tm
    final = of is not None

    def cur(b, j):
        return (b, _scan_tile(j, n_ctx, nt, reverse), 0)

    in_specs = _halo_specs(tm, QW, cols["gqkv"] // QW, n_ctx, nt, reverse)
    in_specs += [pl.BlockSpec((1, tm, 128), cur)]
    args = [P, P, P, Psm]
    if final:
        zb = cols["gz"] // W
        in_specs += [pl.BlockSpec((1, tm, W), lambda b, j: (b, _scan_tile(j, n_ctx, nt, reverse), zb)),
                     pl.BlockSpec((1, tm, W), cur)]
        args += [P, of]
    in_specs += [pl.BlockSpec((4, QW), lambda b, j: (0, 0)),
                 pl.BlockSpec((1, 128), lambda b, j: (0, 0)),
                 pl.BlockSpec((1, 128), lambda b, j: (0, 0))]
    args += [conv_w, rate_row, dtb_row]
    if final:
        in_specs += [pl.BlockSpec((1, GDN_DV), lambda b, j: (0, 0))]
        args += [norm_g.reshape(1, GDN_DV)]
    return pl.pallas_call(
        functools.partial(_gdn_kernel, n_ctx=n_ctx, nt=nt, reverse=reverse, final=final, d=d),
        grid=(B, nt),
        in_specs=in_specs,
        out_specs=pl.BlockSpec((1, tm, W), cur),
        out_shape=jax.ShapeDtypeStruct((B, Tt, W), F32),
        scratch_shapes=[pltpu.VMEM((GDN_HEADS, GDN_DK, GDN_DV), F32), pltpu.VMEM((tm, QW), F32)],
        compiler_params=_cparams(("parallel", "arbitrary")),
        name="gdn_bwd" if reverse else "gdn_fwd",
    )(*args)


def _post_kernel(x_ref, xc_ref, da_ref, dac_ref, lru_ref, gdn_ref, g0_ref, g1_ref, g2_ref, mod_ref, n2_ref,
                 wb_ref, wo_ref, w1_ref, w2_ref, o_ref, *, ff_chunk, n_lat_tiles):
    is_ctx = pl.program_id(1) >= n_lat_tiles
    outs = (jnp.where(is_ctx, dac_ref[0], da_ref[0]), lru_ref[0], gdn_ref[0])
    gates = (g0_ref, g1_ref, g2_ref)
    merged = None
    for i in range(N_BRANCH):
        y = jnp.dot(outs[i].astype(BF16), wb_ref[i], preferred_element_type=F32)
        term = jax.nn.sigmoid(gates[i][0]) * y
        merged = term if merged is None else merged + term
    proj = jnp.dot(merged.astype(BF16), wo_ref[...], preferred_element_type=F32)
    x = jnp.where(is_ctx, xc_ref[0], x_ref[0]) + mod_ref[0, 2:3, :] * proj

    ms = jnp.mean(x * x, axis=-1, keepdims=True)
    y = x * lax.rsqrt(ms + EPS) * n2_ref[...]
    h = (y * (1.0 + mod_ref[0, 4:5, :]) + mod_ref[0, 3:4, :]).astype(BF16)
    n_ff = w1_ref.shape[1]
    acc = None
    for c0 in range(0, n_ff, ff_chunk):
        a = jnp.dot(h, w1_ref[:, c0:c0 + ff_chunk], preferred_element_type=F32)
        a = jnp.square(jnp.maximum(a, 0.0)).astype(BF16)
        part = jnp.dot(a, w2_ref[c0:c0 + ff_chunk, :], preferred_element_type=F32)
        acc = part if acc is None else acc + part
    o_ref[0] = x + mod_ref[0, 5:6, :] * acc


def _post(x_lat, x_ctx, ctx_blk0, da_l, da_c, lru, gdn, P, mod3, n2_g, wb, wo, w1, w2, n_lat_tiles, ctx_row,
          n_rows):
    B, _, D = x_lat.shape
    tm = ROW_TILE
    W = BRANCH_WIDTH
    F = w1.shape[1]
    row = lambda b, t: (b, t, 0)
    lat = lambda b, t: (b, jnp.minimum(t, n_lat_tiles - 1), 0)
    once = pl.Buffered(1)
    return pl.pallas_call(
        functools.partial(_post_kernel, ff_chunk=min(F, 1024), n_lat_tiles=n_lat_tiles),
        grid=(B, n_rows // tm),
        in_specs=[pl.BlockSpec((1, tm, D), lat),
                  pl.BlockSpec((1, tm, D), lambda b, t: (b, ctx_blk0 + jnp.maximum(t - n_lat_tiles, 0), 0)),
                  pl.BlockSpec((1, tm, W), lat),
                  pl.BlockSpec((1, tm, W), lambda b, t: (b, jnp.maximum(t - n_lat_tiles, 0), 0)),
                  pl.BlockSpec((1, tm, W), row), pl.BlockSpec((1, tm, W), row),
                  pl.BlockSpec((1, tm, D), lambda b, t: (b, t, 0)),
                  pl.BlockSpec((1, tm, D), lambda b, t: (b, t, 1)),
                  pl.BlockSpec((1, tm, D), lambda b, t: (b, t, 2)),
                  pl.BlockSpec((1, N_MOD, D), lambda b, t: (jnp.where(t >= n_lat_tiles, ctx_row, b), 0, 0)),
                  pl.BlockSpec((1, D), lambda b, t: (0, 0)),
                  pl.BlockSpec((N_BRANCH, W, D), lambda b, t: (0, 0, 0), pipeline_mode=once),
                  pl.BlockSpec((D, D), lambda b, t: (0, 0), pipeline_mode=once),
                  pl.BlockSpec((D, F), lambda b, t: (0, 0), pipeline_mode=once),
                  pl.BlockSpec((F, D), lambda b, t: (0, 0), pipeline_mode=once)],
        out_specs=pl.BlockSpec((1, tm, D), row),
        out_shape=jax.ShapeDtypeStruct((B, n_rows, D), F32),
        compiler_params=_cparams(("parallel", "parallel")),
        name="merge_mlp",
    )(x_lat, x_ctx, da_l, da_c, lru, gdn, P, P, P, mod3, n2_g.reshape(1, D), wb, wo, w1, w2)


def _proj_layout(D):
    W = BRANCH_WIDTH
    QW = GDN_HEADS * (2 * GDN_DK + GDN_DV)
    src = {}
    off = 0
    for name, width in (("q", W), ("k", W), ("v", W), ("lx", W), ("ly", W), ("gqkv", QW), ("gz", W),
                        ("gb", N_DIR * GDN_HEADS), ("ga", N_DIR * GDN_HEADS), ("gate", N_BRANCH * D)):
        src[name] = (off, width)
        off += width
    order = ("gate", "gqkv", "lx", "ly", "gz")
    cols = {}
    o = 0
    for name in order:
        cols[name] = o
        o += src[name][1]
    return cols, [src[n] for n in order], [src[n] for n in ("q", "k", "v")], [src[n] for n in ("gb", "ga")]


def _take_cols(w, ranges):
    return jnp.concatenate([w[:, s:s + n] for s, n in ranges], axis=1)


def _rope_tables(n_ctx_rows, n_lat_rows):
    n_freq = DA_QK_DIM // 4
    inv = ROPE_BASE ** (-jnp.arange(n_freq, dtype=F32) / n_freq)
    tpos = jnp.arange(n_lat_rows, dtype=jnp.int32)
    ang_r = (tpos // GRID_W).astype(F32)[:, None] * inv
    ang_c = (tpos % GRID_W).astype(F32)[:, None] * inv
    ang = jnp.concatenate([ang_r, ang_r, ang_c, ang_c], axis=-1)
    ang = jnp.concatenate([ang, ang], axis=-1)
    sign = jnp.where((jnp.arange(128) % 32) < 16, -1.0, 1.0).astype(F32)
    cos = jnp.concatenate([jnp.cos(ang), jnp.ones((n_ctx_rows, 128), F32)], axis=0)
    sin = jnp.concatenate([jnp.sin(ang) * sign, jnp.zeros((n_ctx_rows, 128), F32)], axis=0)
    return cos, sin


def kernel(x, c, ctx, c_ctx, ada_w, ada_b, norm1_g, norm2_g, w_in, da_q_norm_g, da_k_norm_g, da_lambda, da_sub_norm_g, lru_conv_w, lru_conv_b, lru_gate_w, lru_gate_b, lru_lambda, gdn_conv_w, gdn_A_log, gdn_dt_bias, gdn_norm_g, w_branch, w_out, mlp_w1, mlp_w2):
    B, T, D = x.shape
    Tc = ctx.shape[1]
    depth = ada_w.shape[0]
    tm = ROW_TILE
    assert Tc % tm == 0 and T % tm == 0 and B + 1 <= 8
    n_ctx = Tc // tm
    n_lat = T // tm
    Tt = Tc + T
    ctx_row = B

    cols, main_cols, qkv_cols, small_cols = _proj_layout(D)
    cos_t, sin_t = _rope_tables(Tc, T)
    x_lat, x_ctx, ctx_blk0 = x, ctx, 0
    c_all = jnp.concatenate([c, c_ctx[None, :], jnp.zeros((8 - B - 1, D), F32)], axis=0)

    tq = _pick(T, (1024, 512, 256))
    n_sub = _pick(Tt // tm, (3, 2, 1))

    out = None
    for layer in range(depth):
        last = layer == depth - 1
        lam_init = 0.8 - 0.6 * math.exp(-0.3 * layer)
        w_main = _take_cols(w_in[layer], main_cols).astype(BF16)
        w_qkv = _take_cols(w_in[layer], qkv_cols).astype(BF16)
        w_small = _take_cols(w_in[layer], small_cols)
        w_small = jnp.pad(w_small, ((0, 0), (0, 128 - w_small.shape[1]))).astype(BF16)

        mod3 = _adaln(c_all, ada_w[layer], ada_b[layer]).reshape(8, N_MOD, D)
        P, Psm, QT, Ks, VT = _front(x_lat, x_ctx, ctx_blk0, mod3, norm1_g[layer], w_main, w_qkv, w_small, cos_t,
                                    sin_t, da_q_norm_g[layer], da_k_norm_g[layer], n_lat, ctx_row)

        da_l = _flash(QT, Ks, VT, da_lambda[layer], da_sub_norm_g[layer], lam_init, 0, T, 0, Tt, tq, n_sub)
        if last:
            da_c = da_l
        else:
            da_c = _flash(QT, Ks, VT, da_lambda[layer], da_sub_norm_g[layer], lam_init, T, Tc, T, Tc, tm, 1)

        hf = _lru_pass(P, cols, lru_conv_w[layer], lru_conv_b[layer], _lru_gate_weights(lru_gate_w[layer, 0]),
                       lru_gate_b[layer, 0].reshape(-1), lru_lambda[layer, 0], n_ctx, False)
        lru = _lru_pass(P, cols, lru_conv_w[layer], lru_conv_b[layer], _lru_gate_weights(lru_gate_w[layer, 1]),
                        lru_gate_b[layer, 1].reshape(-1), lru_lambda[layer, 1], n_ctx, True, hf=hf)

        rate_row = jnp.zeros((1, 128), F32).at[0, N_DIR * GDN_HEADS:2 * N_DIR * GDN_HEADS].set(
            gdn_A_log[layer].astype(F32).reshape(-1))
        dtb_row = jnp.zeros((1, 128), F32).at[0, N_DIR * GDN_HEADS:2 * N_DIR * GDN_HEADS].set(
            gdn_dt_bias[layer].astype(F32).reshape(-1))
        of = _gdn_pass(P, Psm, cols, gdn_conv_w[layer], rate_row, dtb_row, n_ctx, False, 0)
        gdn = _gdn_pass(P, Psm, cols, gdn_conv_w[layer], rate_row, dtb_row, n_ctx, True, 1, of=of,
                        norm_g=gdn_norm_g[layer])

        res = _post(x_lat, x_ctx, ctx_blk0, da_l, da_c, lru, gdn, P, mod3, norm2_g[layer],
                    w_branch[layer].astype(BF16), w_out[layer].astype(BF16), mlp_w1[layer].astype(BF16),
                    mlp_w2[layer].astype(BF16), n_lat, ctx_row, T if last else Tt)
        if last:
            out = res
        else:
            x_lat, x_ctx, ctx_blk0 = res, res, n_lat
    return out
```
